```python
import jax
import jax.numpy as jnp
from jax import lax
import numpy as np

D_MODEL = 2048
BATCH = 2
SEQ = 4096
DEPTH = 4
DEC_BATCH = 8
DEC_SEQ = 8
PAST_LEN = 16384
PAGE_SIZE = 128

N_A_LAYERS = DEPTH // 2
N_B_LAYERS = DEPTH - N_A_LAYERS
POOL_WINDOWS = (2, 4, 8, 16)
POOL_GROUP = D_MODEL // len(POOL_WINDOWS)
POOL_KEEP = max(POOL_WINDOWS) - 1
N_HEADS = 16
N_KV_HEADS = 4
HEADS_PER_GROUP = N_HEADS // N_KV_HEADS
HEAD_DIM = D_MODEL // N_HEADS
N_BRANCH = 3
CMP_BLOCK = 32
CMP_STRIDE = 16
CMP_HIDDEN = HEAD_DIM
SEL_BLOCK = 64
N_SEL = 16
WINDOW = 512
FORCE_BONUS = 1e4
D_FF = 5632
CONV_WIDTH = 3
Q_BLOCK = 128
EPS = 1e-6
NEG_INF = -1e30
Q_WIDTH = N_HEADS * HEAD_DIM
KV_WIDTH = N_BRANCH * 2 * N_KV_HEADS * HEAD_DIM

kernel_name = "yoco_pool_nsa_convffn_step"


def rmsnorm(x, g):
    x32 = x.astype(jnp.float32)
    y = x32 * lax.rsqrt(jnp.mean(x32 * x32, axis=-1, keepdims=True) + EPS)
    return (y * g.astype(jnp.float32)).astype(x.dtype)


def alibi_slopes():
    s = np.array([2.0 ** (-8.0 * (h + 1) / N_HEADS) for h in range(N_HEADS)], np.float32)
    return s.reshape(N_KV_HEADS, HEADS_PER_GROUP)


def masked_softmax(s, mask, axes):
    s = jnp.where(mask, s, NEG_INF)
    m = jnp.max(s, axis=axes, keepdims=True)
    p = jnp.where(mask, jnp.exp(s - m), 0.0)
    den = jnp.sum(p, axis=axes, keepdims=True)
    return p / jnp.where(den > 0, den, 1.0)


def pool_mixer(u, prefix, n_prev, w_pool, scale):
    B, T, _ = u.shape
    u32 = u.astype(jnp.float32)
    ctx = jnp.concatenate([prefix.astype(jnp.float32), u32], axis=1)
    cs = jnp.concatenate([jnp.zeros((B, 1, D_MODEL), jnp.float32), jnp.cumsum(ctx, axis=1)], axis=1)
    end = cs[:, POOL_KEEP + 1:]
    t = jnp.arange(T)
    outs = []
    for g, w in enumerate(POOL_WINDOWS):
        sl = slice(g * POOL_GROUP, (g + 1) * POOL_GROUP)
        start = cs[:, POOL_KEEP + 1 - w:POOL_KEEP + 1 - w + T, sl]
        cnt = jnp.minimum(w, n_prev + t + 1).astype(jnp.float32)[None, :, None]
        outs.append((end[..., sl] - start) / cnt - u32[..., sl])
    pooled = jnp.stack(outs, axis=2).astype(u.dtype)
    mixed = jnp.einsum('btgc,gcd->btgd', pooled, w_pool).reshape(B, T, D_MODEL)
    return mixed * scale, ctx[:, -POOL_KEEP:].astype(u.dtype)


def conv_ffn(u, prefix, w_in, conv_w, conv_b, w_out):
    T = u.shape[1]
    h = jnp.einsum('btd,df->btf', u, w_in)
    hp = jnp.concatenate([prefix.astype(h.dtype), h], axis=1)
    c = conv_b + conv_w[0] * hp[:, 0:T] + conv_w[1] * hp[:, 1:T + 1] + conv_w[2] * hp[:, 2:T + 2]
    gate, val = jnp.split(c, 2, axis=-1)
    y = jnp.einsum('btf,fd->btd', jax.nn.silu(gate) * val, w_out)
    return y, hp[:, -(CONV_WIDTH - 1):]


def kv_project(x, ln_kv, w_kv, k_norm):
    B, T, _ = x.shape
    h = rmsnorm(x, ln_kv)
    kv = jnp.einsum('btd,de->bte', h, w_kv).reshape(B, T, N_BRANCH, 2, N_KV_HEADS, HEAD_DIM)
    kv_cmp = kv[:, :, 0]
    kv_sel = jnp.stack([rmsnorm(kv[:, :, 1, 0], k_norm[1]), kv[:, :, 1, 1]], axis=2)
    kv_win = jnp.stack([rmsnorm(kv[:, :, 2, 0], k_norm[2]), kv[:, :, 2, 1]], axis=2)
    return kv_cmp, kv_sel, kv_win


def compress(kv_full, cmp_w1, cmp_pos, cmp_w2, k_norm_cmp):
    B, T = kv_full.shape[:2]
    n_ch = T // CMP_STRIDE
    ch = kv_full[:, :n_ch * CMP_STRIDE].reshape(B, n_ch, CMP_STRIDE, 2, N_KV_HEADS, HEAD_DIM)
    pa = jnp.einsum('bnsvgd,vsdh->bnvgh', ch, cmp_w1[:, :CMP_STRIDE])
    pb = jnp.einsum('bnsvgd,vsdh->bnvgh', ch, cmp_w1[:, CMP_STRIDE:])
    bias = jnp.einsum('vsd,vsdh->vh', cmp_pos, cmp_w1)
    hid = jax.nn.gelu(pa[:, :-1] + pb[:, 1:] + bias[:, None, :])
    out = jnp.einsum('bnvgh,vhd->bnvgd', hid, cmp_w2)
    kc = rmsnorm(out[:, :, 0], k_norm_cmp)
    vc = out[:, :, 1]
    return kc, vc


def to_blocks(kv_sel):
    B, T = kv_sel.shape[:2]
    nb = -(-T // SEL_BLOCK)
    kvp = jnp.pad(kv_sel, ((0, 0), (0, nb * SEL_BLOCK - T), (0, 0), (0, 0), (0, 0)))
    kvb = kvp.reshape(B, nb, SEL_BLOCK, 2, N_KV_HEADS, HEAD_DIM).transpose(3, 0, 4, 1, 2, 5)
    return kvb[0], kvb[1]


def cmp_to_sel_map(nc, nb):
    c_s = np.arange(nc) * CMP_STRIDE
    c_e = c_s + CMP_BLOCK - 1
    s_s = np.arange(nb) * SEL_BLOCK
    s_e = s_s + SEL_BLOCK - 1
    return ((c_s[:, None] <= s_e[None]) & (c_e[:, None] >= s_s[None])).astype(np.float32)


def nsa_queries(h, w_qg, q_norm):
    B, T, _ = h.shape
    qg = jnp.einsum('btd,de->bte', h, w_qg)
    q = rmsnorm(qg[..., :Q_WIDTH].reshape(B, T, N_KV_HEADS, HEADS_PER_GROUP, HEAD_DIM), q_norm)
    q = (q * (HEAD_DIM ** -0.5)).transpose(0, 2, 3, 1, 4)
    gates = jax.nn.sigmoid(qg[..., Q_WIDTH:].astype(jnp.float32)).astype(h.dtype)
    return q, gates.reshape(B, T, N_HEADS, N_BRANCH)


def nsa_attend(q, gates, pos_q, kc, vc, k_blocks, v_blocks, k_w, v_w, pos_w):
    B, G, R, Q, _ = q.shape
    slopes = jnp.asarray(alibi_slopes())[None, :, :, None, None]
    nc = kc.shape[1]
    nb = k_blocks.shape[2]
    pos_c = jnp.arange(nc) * CMP_STRIDE + (CMP_BLOCK - 1)
    dist_c = pos_q[:, None] - pos_c[None, :]
    s_c = jnp.einsum('bgrqd,bcgd->bgrqc', q, kc).astype(jnp.float32) - slopes * dist_c.astype(jnp.float32)
    p_c = masked_softmax(s_c, dist_c >= 0, -1)
    o_c = jnp.einsum('bgrqc,bcgd->bgrqd', p_c.astype(vc.dtype), vc)
    imp = jnp.einsum('bgrqc,cj->bgqj', p_c, jnp.asarray(cmp_to_sel_map(nc, nb)))
    blk = jnp.arange(nb)
    cur = pos_q // SEL_BLOCK
    forced = (blk[None] == 0) | (blk[None] == cur[:, None]) | (blk[None] == cur[:, None] - 1)
    valid = blk[None] * SEL_BLOCK <= pos_q[:, None]
    imp = jnp.where(valid, jnp.where(forced, imp + FORCE_BONUS, imp), NEG_INF)
    _, idx = lax.top_k(imp, min(N_SEL, nb))
    bi = jnp.arange(B)[:, None, None, None]
    gi = jnp.arange(G)[None, :, None, None]
    k_s = k_blocks[bi, gi, idx]
    v_s = v_blocks[bi, gi, idx]
    pos_s = idx[..., None] * SEL_BLOCK + jnp.arange(SEL_BLOCK)
    dist_s = (pos_q[None, None, :, None, None] - pos_s)[:, :, None]
    s_s = jnp.einsum('bgrqd,bgqksd->bgrqks', q, k_s).astype(jnp.float32) - slopes[..., None] * dist_s.astype(jnp.float32)
    p_s = masked_softmax(s_s, dist_s >= 0, (-2, -1))
    o_s = jnp.einsum('bgrqks,bgqksd->bgrqd', p_s.astype(v_s.dtype), v_s)
    dist_w = pos_q[:, None] - pos_w[None, :]
    mask_w = (dist_w >= 0) & (dist_w < WINDOW) & (pos_w[None, :] >= 0)
    s_w = jnp.einsum('bgrqd,bkgd->bgrqk', q, k_w).astype(jnp.float32) - slopes * dist_w.astype(jnp.float32)
    p_w = masked_softmax(s_w, mask_w, -1)
    o_w = jnp.einsum('bgrqk,bkgd->bgrqd', p_w.astype(v_w.dtype), v_w)
    g = gates.reshape(B, Q, G, R, N_BRANCH).transpose(0, 2, 3, 1, 4)
    o = g[..., 0:1] * o_c + g[..., 1:2] * o_s + g[..., 2:3] * o_w
    return o.transpose(0, 3, 1, 2, 4).reshape(B, Q, Q_WIDTH)


def nsa_prompt(q, gates, kc, vc, kb, vb, kv_win):
    B = q.shape[0]
    T = q.shape[3]
    kw = jnp.pad(kv_win, ((0, 0), (WINDOW, 0), (0, 0), (0, 0), (0, 0)))

    def one_block(i):
        s0 = i * Q_BLOCK
        qb = lax.dynamic_slice_in_dim(q, s0, Q_BLOCK, axis=3)
        gb = lax.dynamic_slice_in_dim(gates, s0, Q_BLOCK, axis=1)
        wb = lax.dynamic_slice_in_dim(kw, s0, WINDOW + Q_BLOCK, axis=1)
        pos_q = s0 + jnp.arange(Q_BLOCK)
        pos_w = s0 - WINDOW + jnp.arange(WINDOW + Q_BLOCK)
        return nsa_attend(qb, gb, pos_q, kc, vc, kb, vb, wb[:, :, 0], wb[:, :, 1], pos_w)

    out = lax.map(one_block, jnp.arange(T // Q_BLOCK))
    return out.transpose(1, 0, 2, 3).reshape(B, T, Q_WIDTH)


def trunk(x, n_prev, pool_prefix, conv_prefix, kv_cmp_past, kv_sel_past, kv_win_past, p):
    is_prompt = kv_cmp_past is None
    B, T, _ = x.shape
    pool_rows, conv_rows = [], []
    for l in range(DEPTH):
        h = rmsnorm(x, p['ln_mix'][l])
        if l < N_A_LAYERS:
            mix, rows = pool_mixer(h, pool_prefix[l], n_prev, p['w_pool'][l], p['pool_scale'][l])
            pool_rows.append(rows)
        else:
            if l == N_A_LAYERS:
                cmp_new, sel_new, win_new = kv_project(x, p['ln_kv'], p['w_kv'], p['k_norm'])
                if is_prompt:
                    cmp_full, sel_full, win_keys = cmp_new, sel_new, win_new
                else:
                    cmp_full = jnp.concatenate([kv_cmp_past, cmp_new], axis=1)
                    sel_full = jnp.concatenate([kv_sel_past, sel_new], axis=1)
                    win_keys = jnp.concatenate([kv_win_past, win_new], axis=1)
                kc, vc = compress(cmp_full, p['cmp_w1'], p['cmp_pos'], p['cmp_w2'], p['k_norm'][0])
                kb, vb = to_blocks(sel_full)
                win_state = win_keys[:, win_keys.shape[1] - min(WINDOW, n_prev + T):]
            j = l - N_A_LAYERS
            q, gates = nsa_queries(h, p['w_qg'][j], p['q_norm'][j])
            if is_prompt:
                o = nsa_prompt(q, gates, kc, vc, kb, vb, win_keys)
            else:
                pos_q = n_prev + jnp.arange(T)
                pos_w = n_prev - kv_win_past.shape[1] + jnp.arange(win_keys.shape[1])
                o = nsa_attend(q, gates, pos_q, kc, vc, kb, vb, win_keys[:, :, 0], win_keys[:, :, 1], pos_w)
            mix = jnp.einsum('bte,ed->btd', o, p['w_o'][j])
        x = x + mix
        h = rmsnorm(x, p['ln_ffn'][l])
        y, rows = conv_ffn(h, conv_prefix[l], p['w_in'][l], p['conv_w'][l], p['conv_b'][l], p['w_out'][l])
        conv_rows.append(rows)
        x = x + y
    return x, cmp_new, sel_new, win_state, jnp.stack(pool_rows), jnp.stack(conv_rows)


def setup_inputs(seed: int = 0) -> dict:
    key = jax.random.key(seed)
    ks = jax.random.split(key, 26)
    f32 = jnp.float32

    def nrm(k, shape, scale=1.0):
        return jax.random.normal(k, shape, f32) * scale

    n_pages = PAST_LEN // PAGE_SIZE
    n_used = DEC_BATCH * n_pages
    n_phys = n_used + max(1, n_used // 4)
    win_buf = min(WINDOW, PAST_LEN)
    page_table = jax.random.permutation(ks[0], n_phys)[:n_used].reshape(DEC_BATCH, n_pages).astype(jnp.int32)
    row = (2, N_KV_HEADS, HEAD_DIM)
    return {
        'x_prompt': nrm(ks[1], (BATCH, SEQ, D_MODEL)),
        'x_sample': nrm(ks[2], (DEC_BATCH, DEC_SEQ, D_MODEL)),
        'cache_kv_cmp': nrm(ks[3], (n_phys, PAGE_SIZE) + row),
        'cache_kv_sel': nrm(ks[4], (n_phys, PAGE_SIZE) + row),
        'page_table': page_table,
        'state_kv_win': nrm(ks[5], (DEC_BATCH, win_buf) + row),
        'state_pool': nrm(ks[6], (N_A_LAYERS, DEC_BATCH, POOL_KEEP, D_MODEL)),
        'state_conv': nrm(ks[7], (DEPTH, DEC_BATCH, CONV_WIDTH - 1, 2 * D_FF)),
        'ln_mix': 1.0 + nrm(ks[8], (DEPTH, D_MODEL), 0.02),
        'ln_ffn': 1.0 + nrm(ks[9], (DEPTH, D_MODEL), 0.02),
        'w_pool': nrm(ks[10], (N_A_LAYERS, len(POOL_WINDOWS), POOL_GROUP, POOL_GROUP), POOL_GROUP ** -0.5),
        'pool_scale': 0.3 + nrm(ks[11], (N_A_LAYERS, D_MODEL), 0.02),
        'ln_kv': 1.0 + nrm(ks[12], (D_MODEL,), 0.02),
        'w_kv': nrm(ks[13], (D_MODEL, KV_WIDTH), D_MODEL ** -0.5),
        'k_norm': 1.0 + nrm(ks[14], (N_BRANCH, HEAD_DIM), 0.02),
        'cmp_w1': nrm(ks[15], (2, CMP_BLOCK, HEAD_DIM, CMP_HIDDEN), (CMP_BLOCK * HEAD_DIM) ** -0.5),
        'cmp_pos': nrm(ks[16], (2, CMP_BLOCK, HEAD_DIM), 0.1),
        'cmp_w2': nrm(ks[17], (2, CMP_HIDDEN, HEAD_DIM), CMP_HIDDEN ** -0.5),
        'w_qg': nrm(ks[18], (N_B_LAYERS, D_MODEL, Q_WIDTH + N_BRANCH * N_HEADS), D_MODEL ** -0.5),
        'q_norm': 1.0 + nrm(ks[19], (N_B_LAYERS, HEAD_DIM), 0.02),
        'w_o': nrm(ks[20], (N_B_LAYERS, Q_WIDTH, D_MODEL), Q_WIDTH ** -0.5),
        'w_in': nrm(ks[21], (DEPTH, D_MODEL, 2 * D_FF), D_MODEL ** -0.5),
        'conv_w': nrm(ks[22], (DEPTH, CONV_WIDTH, 2 * D_FF), CONV_WIDTH ** -0.5),
        'conv_b': nrm(ks[23], (DEPTH, 2 * D_FF), 0.01),
        'w_out': nrm(ks[24], (DEPTH, D_FF, D_MODEL), D_FF ** -0.5),
    }


def reference(x_prompt, x_sample, cache_kv_cmp, cache_kv_sel, page_table, state_kv_win, state_pool, state_conv,
              ln_mix, ln_ffn, w_pool, pool_scale, ln_kv, w_kv, k_norm, cmp_w1, cmp_pos, cmp_w2,
              w_qg, q_norm, w_o, w_in, conv_w, conv_b, w_out):
    p = dict(ln_mix=ln_mix, ln_ffn=ln_ffn, w_pool=w_pool, pool_scale=pool_scale, ln_kv=ln_kv, w_kv=w_kv,
             k_norm=k_norm, cmp_w1=cmp_w1, cmp_pos=cmp_pos, cmp_w2=cmp_w2, w_qg=w_qg, q_norm=q_norm,
             w_o=w_o, w_in=w_in, conv_w=conv_w, conv_b=conv_b, w_out=w_out)
    b_p = x_prompt.shape[0]
    b_d, n_pages = page_table.shape
    past_len = n_pages * PAGE_SIZE
    pool_zero = jnp.zeros((N_A_LAYERS, b_p, POOL_KEEP, D_MODEL), x_prompt.dtype)
    conv_zero = jnp.zeros((DEPTH, b_p, CONV_WIDTH - 1, 2 * D_FF), x_prompt.dtype)
    y_prompt, cmp_p, sel_p, win_p, pool_p, conv_p = trunk(x_prompt, 0, pool_zero, conv_zero, None, None, None, p)
    past_cmp = cache_kv_cmp[page_table].reshape((b_d, past_len) + cache_kv_cmp.shape[2:])
    past_sel = cache_kv_sel[page_table].reshape((b_d, past_len) + cache_kv_sel.shape[2:])
    y_sample, cmp_s, sel_s, win_s, pool_s, conv_s = trunk(x_sample, past_len, state_pool, state_conv,
                                                          past_cmp, past_sel, state_kv_win, p)
    return (y_prompt, y_sample, cmp_p, cmp_s, sel_p, sel_s, win_p, win_s, pool_p, pool_s, conv_p, conv_s)
```

```python
import functools

import numpy as np
import jax
import jax.numpy as jnp
from jax import lax
from jax.experimental import pallas as pl
from jax.experimental.pallas import tpu as pltpu

F32 = jnp.float32
BF16 = jnp.bfloat16

POOL_WINDOWS = (2, 4, 8, 16)
POOL_KEEP = max(POOL_WINDOWS) - 1
N_HEADS = 16
N_KV_HEADS = 4
HEADS_PER_GROUP = N_HEADS // N_KV_HEADS
N_BRANCH = 3
CMP_BLOCK = 32
CMP_STRIDE = 16
SEL_BLOCK = 64
N_SEL = 16
WINDOW = 512
FORCE_BONUS = 1e4
CONV_WIDTH = 3
Q_BLOCK = 128
PAGE_SIZE = 128
EPS = 1e-6
NEG_INF = -1e30

LANES = 128
SUBLANES = 8
BF16_ROWS = 16
MIB = 2 ** 20


def _cparams(n_axes, vmem_mib):
    return pltpu.CompilerParams(dimension_semantics=("arbitrary",) * n_axes,
                                vmem_limit_bytes=vmem_mib * MIB)


def _rms(x, g):
    return (x * lax.rsqrt(jnp.mean(x * x, axis=-1, keepdims=True) + EPS)) * g


def _alibi_slope(h):
    return float(np.float32(2.0 ** (-8.0 * (h + 1) / N_HEADS)))


def _norm_linear_kernel(x_ref, g_ref, w_ref, hg_ref, of_ref, ob_ref, *, n_norm, norm_scale, sigmoid_from):
    h = _rms(x_ref[...], g_ref[...]).astype(BF16)
    y = jnp.dot(h, w_ref[...], preferred_element_type=F32)
    for c in range(y.shape[1] // LANES):
        cols = slice(c * LANES, (c + 1) * LANES)
        yc = y[:, cols]
        if c < n_norm:
            yc = _rms(yc, hg_ref[...])
            if norm_scale is not None:
                yc = yc * norm_scale
        elif sigmoid_from is not None and c >= sigmoid_from:
            yc = jax.nn.sigmoid(yc)
        of_ref[:, cols] = yc
        ob_ref[:, cols] = yc.astype(BF16)


def norm_linear(x, gain, w_bf16, head_gain, *, n_norm=0, norm_scale=None, sigmoid_from=None, tm=512):
    m, k = x.shape
    n = w_bf16.shape[1]
    tm = min(tm, m)
    assert m % tm == 0 and n % LANES == 0
    kern = functools.partial(_norm_linear_kernel, n_norm=n_norm, norm_scale=norm_scale, sigmoid_from=sigmoid_from)
    return pl.pallas_call(
        kern,
        out_shape=(jax.ShapeDtypeStruct((m, n), F32), jax.ShapeDtypeStruct((m, n), BF16)),
        grid=(m // tm,),
        in_specs=[pl.BlockSpec((tm, k), lambda i: (i, 0)),
                  pl.BlockSpec((1, k), lambda i: (0, 0)),
                  pl.BlockSpec((k, n), lambda i: (0, 0)),
                  pl.BlockSpec((1, LANES), lambda i: (0, 0))],
        out_specs=(pl.BlockSpec((tm, n), lambda i: (i, 0)), pl.BlockSpec((tm, n), lambda i: (i, 0))),
        compiler_params=_cparams(1, 48),
        name="norm_linear",
    )(x, gain.reshape(1, k), w_bf16, head_gain.reshape(1, LANES))


def _linear_residual_kernel(a_ref, w_ref, r_ref, o_ref):
    o_ref[...] = r_ref[...] + jnp.dot(a_ref[...], w_ref[...], preferred_element_type=F32)


def linear_residual(a_bf16, w_bf16, res, *, tm=512):
    m, k = a_bf16.shape
    n = w_bf16.shape[1]
    tm = min(tm, m)
    assert m % tm == 0
    return pl.pallas_call(
        _linear_residual_kernel,
        out_shape=jax.ShapeDtypeStruct((m, n), F32),
        grid=(m // tm,),
        in_specs=[pl.BlockSpec((tm, k), lambda i: (i, 0)),
                  pl.BlockSpec((k, n), lambda i: (0, 0)),
                  pl.BlockSpec((tm, n), lambda i: (i, 0))],
        out_specs=pl.BlockSpec((tm, n), lambda i: (i, 0)),
        compiler_params=_cparams(1, 48),
        name="linear_residual",
    )(a_bf16, w_bf16, res)


POOL_HALO = 16


def _pool_kernel(x_ref, xh_ref, pre_ref, g_ref, w_ref, sc_ref, o_ref, st_ref, ctx_scr, *, tm, n_prev):
    i = pl.program_id(1)
    x = x_ref[...]
    h = _rms(x, g_ref[...])
    halo = _rms(xh_ref[...], g_ref[...])
    ctx_scr[0:POOL_HALO, :] = jnp.where(i == 0, pre_ref[...], halo)
    ctx_scr[POOL_HALO:POOL_HALO + tm, :] = h
    t = i * tm + lax.broadcasted_iota(jnp.int32, (tm, 1), 0)
    gw = x.shape[1] // len(POOL_WINDOWS)
    for gi, w in enumerate(POOL_WINDOWS):
        cols = slice(gi * gw, (gi + 1) * gw)
        hs = h[:, cols]
        acc = hs
        for k in range(1, w):
            acc = acc + ctx_scr[POOL_HALO - k:POOL_HALO - k + tm, cols]
        cnt = jnp.minimum(w, n_prev + t + 1).astype(F32)
        pooled = acc / cnt - hs
        mixed = jnp.dot(pooled.astype(BF16), w_ref[gi], preferred_element_type=F32)
        o_ref[:, cols] = x[:, cols] + mixed * sc_ref[:, cols]
    st_ref[...] = ctx_scr[tm + POOL_HALO - POOL_KEEP:tm + POOL_HALO, :]


def pool_layer(x, x_halo_src, prefix16, gain, w_pool_bf16, scale, *, n_prev, tm):
    b, t, d = x.shape
    assert t % tm == 0 and tm % SUBLANES == 0
    hb = tm // POOL_HALO
    kern = functools.partial(_pool_kernel, tm=tm, n_prev=n_prev)
    return pl.pallas_call(
        kern,
        out_shape=(jax.ShapeDtypeStruct((b, t, d), F32), jax.ShapeDtypeStruct((b, POOL_KEEP, d), F32)),
        grid=(b, t // tm),
        in_specs=[pl.BlockSpec((None, tm, d), lambda bi, i: (bi, i, 0)),
                  pl.BlockSpec((None, POOL_HALO, d), lambda bi, i: (bi, jnp.maximum(i * hb - 1, 0), 0)),
                  pl.BlockSpec((None, POOL_HALO, d), lambda bi, i: (bi, 0, 0)),
                  pl.BlockSpec((1, d), lambda bi, i: (0, 0)),
                  pl.BlockSpec(w_pool_bf16.shape, lambda bi, i: (0, 0, 0)),
                  pl.BlockSpec((1, d), lambda bi, i: (0, 0))],
        out_specs=(pl.BlockSpec((None, tm, d), lambda bi, i: (bi, i, 0)),
                   pl.BlockSpec((None, POOL_KEEP, d), lambda bi, i: (bi, 0, 0))),
        scratch_shapes=[pltpu.VMEM((tm + POOL_HALO, d), F32)],
        compiler_params=_cparams(2, 48),
        name="pool_layer",
    )(x, x_halo_src, prefix16, gain.reshape(1, d), w_pool_bf16, scale.reshape(1, d))


FFN_HALO = 16


def _ffn_kernel(x_ref, xh_ref, pg_ref, pv_ref, g_ref, wig_ref, wiv_ref, cwg_ref, cwv_ref, cbg_ref, cbv_ref, wo_ref,
                o_ref, sg_ref, sv_ref, u_scr, hg_scr, hv_scr, acc_scr, *, nb, tm, te):
    i = pl.program_id(1)
    f = pl.program_id(2)
    n_ext = nb * te
    n_out = n_ext - FFN_HALO

    @pl.when(f == 0)
    def _():
        for s in range(nb):
            u_scr[s * te:s * te + FFN_HALO, :] = _rms(xh_ref[s], g_ref[...]).astype(BF16)
            u_scr[s * te + FFN_HALO:s * te + FFN_HALO + tm, :] = _rms(x_ref[s], g_ref[...]).astype(BF16)
            if te > FFN_HALO + tm:
                u_scr[s * te + FFN_HALO + tm:(s + 1) * te, :] = jnp.zeros((te - FFN_HALO - tm, u_scr.shape[1]), BF16)
        acc_scr[...] = jnp.zeros_like(acc_scr)

    u = u_scr[...]
    hg_scr[...] = jnp.dot(u, wig_ref[...], preferred_element_type=F32)
    hv_scr[...] = jnp.dot(u, wiv_ref[...], preferred_element_type=F32)

    @pl.when(i == 0)
    def _():
        for s in range(nb):
            hg_scr[s * te:s * te + FFN_HALO, :] = pg_ref[s]
            hv_scr[s * te:s * te + FFN_HALO, :] = pv_ref[s]

    def conv(h_scr, cw_ref, cb_ref):
        c = cb_ref[...] + cw_ref[0:1, :] * h_scr[FFN_HALO - 2:FFN_HALO - 2 + n_out, :]
        c = c + cw_ref[1:2, :] * h_scr[FFN_HALO - 1:FFN_HALO - 1 + n_out, :]
        return c + cw_ref[2:3, :] * h_scr[FFN_HALO:FFN_HALO + n_out, :]

    cg = conv(hg_scr, cwg_ref, cbg_ref)
    cv = conv(hv_scr, cwv_ref, cbv_ref)
    act = (cg * jax.nn.sigmoid(cg)) * cv
    acc_scr[...] += jnp.dot(act.astype(BF16), wo_ref[...], preferred_element_type=F32)

    for s in range(nb):
        sg_ref[s] = hg_scr[s * te + tm:s * te + tm + FFN_HALO, :]
        sv_ref[s] = hv_scr[s * te + tm:s * te + tm + FFN_HALO, :]

    @pl.when(f == pl.num_programs(2) - 1)
    def _():
        for s in range(nb):
            o_ref[s] = x_ref[s] + acc_scr[s * te:s * te + tm, :]


def ffn_layer(x, x_halo_src, prefix16, gain, w_in_bf16, conv_w, conv_b, w_out_bf16, *, nb, tm, tf=512):
    b, t, d = x.shape
    ff = w_out_bf16.shape[0]
    assert t % tm == 0 and b % nb == 0 and ff % tf == 0 and tm % SUBLANES == 0
    nf = ff // tf
    hb = tm // FFN_HALO
    te = FFN_HALO + -(-tm // BF16_ROWS) * BF16_ROWS
    n_ext = nb * te
    kern = functools.partial(_ffn_kernel, nb=nb, tm=tm, te=te)
    gate_col = lambda bi, i, f: (0, f)
    val_col = lambda bi, i, f: (0, nf + f)
    y, sg, sv = pl.pallas_call(
        kern,
        out_shape=(jax.ShapeDtypeStruct((b, t, d), F32),
                   jax.ShapeDtypeStruct((b, t // tm, FFN_HALO, ff), F32),
                   jax.ShapeDtypeStruct((b, t // tm, FFN_HALO, ff), F32)),
        grid=(b // nb, t // tm, nf),
        in_specs=[pl.BlockSpec((nb, tm, d), lambda bi, i, f: (bi, i, 0)),
                  pl.BlockSpec((nb, FFN_HALO, d), lambda bi, i, f: (bi, jnp.maximum(i * hb - 1, 0), 0)),
                  pl.BlockSpec((nb, FFN_HALO, tf), lambda bi, i, f: (bi, 0, f)),
                  pl.BlockSpec((nb, FFN_HALO, tf), lambda bi, i, f: (bi, 0, nf + f)),
                  pl.BlockSpec((1, d), lambda bi, i, f: (0, 0)),
                  pl.BlockSpec((d, tf), gate_col),
                  pl.BlockSpec((d, tf), val_col),
                  pl.BlockSpec((CONV_WIDTH, tf), gate_col),
                  pl.BlockSpec((CONV_WIDTH, tf), val_col),
                  pl.BlockSpec((1, tf), gate_col),
                  pl.BlockSpec((1, tf), val_col),
                  pl.BlockSpec((tf, d), lambda bi, i, f: (f, 0))],
        out_specs=(pl.BlockSpec((nb, tm, d), lambda bi, i, f: (bi, i, 0)),
                   pl.BlockSpec((nb, None, FFN_HALO, tf), lambda bi, i, f: (bi, i, 0, f)),
                   pl.BlockSpec((nb, None, FFN_HALO, tf), lambda bi, i, f: (bi, i, 0, f))),
        scratch_shapes=[pltpu.VMEM((n_ext, d), BF16),
                        pltpu.VMEM((n_ext, tf), F32),
                        pltpu.VMEM((n_ext, tf), F32),
                        pltpu.VMEM((n_ext - FFN_HALO, d), F32)],
        compiler_params=_cparams(3, 56),
        name="ffn_layer",
    )(x, x_halo_src, prefix16, prefix16, gain.reshape(1, d), w_in_bf16, w_in_bf16, conv_w, conv_w,
      conv_b.reshape(1, 2 * ff), conv_b.reshape(1, 2 * ff), w_out_bf16)
    return y, jnp.concatenate([sg[:, -1], sv[:, -1]], axis=-1)


CMP_PAGES = 16
CHUNKS_PER_PAGE = PAGE_SIZE // CMP_STRIDE


def _compress_kernel(pt_ref, *refs, n_pages):
    del pt_ref
    page_refs = refs[:n_pages]
    w_ref, o_ref = refs[n_pages], refs[n_pages + 1]
    hd = w_ref.shape[2]
    rows = n_pages * CHUNKS_PER_PAGE
    width = page_refs[0].shape[1] // CMP_STRIDE
    for v in range(2):
        acc = None
        for s in range(CMP_STRIDE):
            pieces = []
            for g in range(N_KV_HEADS):
                c0 = s * width + (v * N_KV_HEADS + g) * hd
                for k in range(n_pages):
                    pieces.append(page_refs[k][:, c0:c0 + hd])
            lhs = jnp.concatenate(pieces, axis=0).astype(BF16)
            d = jnp.dot(lhs, w_ref[v, s], preferred_element_type=F32)
            acc = d if acc is None else acc + d
        for g in range(N_KV_HEADS):
            o_ref[v, g] = acc[g * rows:(g + 1) * rows, :]


def compress_partial(pages, page_table, w1cat_bf16):
    s, npg = page_table.shape
    chunk_w = CMP_STRIDE * pages.shape[2]
    pages = pages.reshape(pages.shape[0], CHUNKS_PER_PAGE, chunk_w)
    hid2 = w1cat_bf16.shape[3]
    n_p = min(CMP_PAGES, npg)
    assert npg % n_p == 0
    n_ch = npg * CHUNKS_PER_PAGE

    def page_spec(k):
        return pl.BlockSpec((None, CHUNKS_PER_PAGE, chunk_w), lambda si, j, pt: (pt[si * npg + j * n_p + k], 0, 0))

    grid_spec = pltpu.PrefetchScalarGridSpec(
        num_scalar_prefetch=1,
        grid=(s, npg // n_p),
        in_specs=[page_spec(k) for k in range(n_p)]
        + [pl.BlockSpec(w1cat_bf16.shape, lambda si, j, pt: (0, 0, 0, 0))],
        out_specs=pl.BlockSpec((None, 2, N_KV_HEADS, n_p * CHUNKS_PER_PAGE, hid2), lambda si, j, pt: (si, 0, 0, j, 0)),
    )
    return pl.pallas_call(
        functools.partial(_compress_kernel, n_pages=n_p),
        out_shape=jax.ShapeDtypeStruct((s, 2, N_KV_HEADS, n_ch, hid2), F32),
        grid_spec=grid_spec,
        compiler_params=_cparams(2, 48),
        name="compress_partial",
    )(page_table.reshape(-1), *([pages] * n_p), w1cat_bf16)


def _compress_finish_kernel(pab_ref, pos_ref, w1_ref, w2_ref, kn_ref, kc_ref, vc_ref):
    n = pab_ref.shape[1]
    hid = w2_ref.shape[1]
    for v in range(2):
        blk = pab_ref[v]
        nxt = pltpu.roll(blk[:, hid:], n - 1, 0)
        pos = jnp.broadcast_to(pos_ref[v], (SUBLANES, pos_ref.shape[2])).astype(BF16)
        bias = jnp.dot(pos, w1_ref[v], preferred_element_type=F32)[0:1, :]
        hdn = jax.nn.gelu(blk[:, :hid] + nxt + bias)
        out = jnp.dot(hdn.astype(BF16), w2_ref[v], preferred_element_type=F32)
        if v == 0:
            kc_ref[...] = _rms(out, kn_ref[...]).astype(BF16)
        else:
            vc_ref[...] = out.astype(BF16)


def compress_finish(pab, cmp_pos, w1flat_bf16, w2_bf16, k_gain):
    s, _, g, n_ch, hid2 = pab.shape
    hd = w2_bf16.shape[2]
    kdim = w1flat_bf16.shape[1]
    out = jax.ShapeDtypeStruct((s, n_ch, g * hd), BF16)
    return pl.pallas_call(
        _compress_finish_kernel,
        out_shape=(out, out),
        grid=(s, g),
        in_specs=[pl.BlockSpec((None, 2, None, n_ch, hid2), lambda si, gi: (si, 0, gi, 0, 0)),
                  pl.BlockSpec((2, 1, kdim), lambda si, gi: (0, 0, 0)),
                  pl.BlockSpec(w1flat_bf16.shape, lambda si, gi: (0, 0, 0)),
                  pl.BlockSpec(w2_bf16.shape, lambda si, gi: (0, 0, 0)),
                  pl.BlockSpec((1, hd), lambda si, gi: (0, 0))],
        out_specs=(pl.BlockSpec((None, n_ch, hd), lambda si, gi: (si, 0, gi)),
                   pl.BlockSpec((None, n_ch, hd), lambda si, gi: (si, 0, gi))),
        compiler_params=_cparams(2, 48),
        name="compress_finish",
    )(pab, cmp_pos.reshape(2, 1, kdim), w1flat_bf16, w2_bf16, k_gain.reshape(1, hd))


def _cmp_to_sel_map(nc, nb):
    c_s = np.arange(nc) * CMP_STRIDE
    c_e = c_s + CMP_BLOCK - 1
    s_s = np.arange(nb) * SEL_BLOCK
    s_e = s_s + SEL_BLOCK - 1
    return ((c_s[:, None] <= s_e[None]) & (c_e[:, None] >= s_s[None])).astype(np.float32)


def _masked_softmax(s, mask, axis):
    s = jnp.where(mask, s, NEG_INF)
    m = jnp.max(s, axis=axis, keepdims=True)
    p = jnp.where(mask, jnp.exp(s - m), 0.0)
    den = jnp.sum(p, axis=axis, keepdims=True)
    return p * (1.0 / jnp.where(den > 0, den, 1.0))


SEL_KEY_TILE = 512


def _nsa_prompt_kernel(q_ref, gt_ref, kc_ref, vc_ref, sel_ref, win_ref, map_ref, exp_ref, o_ref, *, wlen):
    i = pl.program_id(1)
    s0 = i * Q_BLOCK
    hd = LANES
    rr, qb, tk = HEADS_PER_GROUP, Q_BLOCK, SEL_KEY_TILE
    m_rows = rr * qb
    kvw = N_KV_HEADS * hd
    row = lax.broadcasted_iota(jnp.int32, (m_rows, 1), 0)
    pos_q = (s0 + (row & (qb - 1))).astype(F32)
    posq1 = s0 + lax.broadcasted_iota(jnp.int32, (qb, 1), 0)
    n_cp = kc_ref.shape[0]
    nb = map_ref.shape[1]
    nt_dims = (((1,), (1,)), ((), ()))
    n_kt = (s0 + qb + tk - 1) // tk
    ws = pl.multiple_of(jnp.maximum(s0 - WINDOW, 0), Q_BLOCK)

    for g in range(N_KV_HEADS):
        kcols = slice(g * hd, (g + 1) * hd)
        vcols = slice(kvw + g * hd, kvw + (g + 1) * hd)
        q = jnp.concatenate([q_ref[:, (g * rr + r) * hd:(g * rr + r + 1) * hd] for r in range(rr)], axis=0)
        slope = jnp.full((m_rows, 1), _alibi_slope(g * rr + rr - 1), F32)
        for r in reversed(range(rr - 1)):
            slope = jnp.where(row < (r + 1) * qb, _alibi_slope(g * rr + r), slope)

        pos_c = (lax.broadcasted_iota(jnp.int32, (1, n_cp), 1) * CMP_STRIDE + (CMP_BLOCK - 1)).astype(F32)
        dist = pos_q - pos_c
        s = lax.dot_general(q, kc_ref[:, kcols], nt_dims, preferred_element_type=F32) - slope * dist
        p = _masked_softmax(s, dist >= 0, -1)
        o_c = jnp.dot(p.astype(BF16), vc_ref[:, kcols], preferred_element_type=F32)

        p_sum = p[0:qb]
        for r in range(1, rr):
            p_sum = p_sum + p[r * qb:(r + 1) * qb]
        imp = jnp.dot(p_sum, map_ref[...], precision=lax.Precision.HIGHEST, preferred_element_type=F32)
        blk = lax.broadcasted_iota(jnp.int32, (1, nb), 1)
        cur = posq1 // SEL_BLOCK
        forced = (blk == 0) | (blk == cur) | (blk == cur - 1)
        valid = blk * SEL_BLOCK <= posq1
        imp = jnp.where(valid, jnp.where(forced, imp + FORCE_BONUS, imp), NEG_INF)
        cnt = jnp.zeros((qb, nb), F32)
        for c in range(nb):
            col = imp[:, c:c + 1]
            beats = (col > imp) | ((col == imp) & (c < blk))
            cnt = cnt + jnp.where(beats, 1.0, 0.0)
        sel = jnp.where(cnt < min(N_SEL, nb), 1.0, 0.0).astype(BF16)

        def sel_step(kt, carry):
            m_i, l_i, acc = carry
            k0 = pl.multiple_of(kt * tk, tk)
            k = sel_ref[pl.ds(k0, tk), kcols]
            v = sel_ref[pl.ds(k0, tk), vcols]
            pos_k = (k0 + lax.broadcasted_iota(jnp.int32, (1, tk), 1)).astype(F32)
            dist = pos_q - pos_k
            s = lax.dot_general(q, k, nt_dims, preferred_element_type=F32) - slope * dist
            selm = jnp.dot(sel, exp_ref[:, pl.ds(k0, tk)], preferred_element_type=F32)
            selm = jnp.concatenate([selm] * rr, axis=0)
            mask = (dist >= 0) & (selm > 0.5)
            s = jnp.where(mask, s, NEG_INF)
            m_new = jnp.maximum(m_i, jnp.max(s, axis=-1, keepdims=True))
            alpha = jnp.exp(m_i - m_new)
            p = jnp.where(mask, jnp.exp(s - m_new), 0.0)
            l_new = alpha * l_i + jnp.sum(p, axis=-1, keepdims=True)
            acc = alpha * acc + jnp.dot(p.astype(BF16), v, preferred_element_type=F32)
            return m_new, l_new, acc

        init = (jnp.full((m_rows, 1), NEG_INF, F32), jnp.zeros((m_rows, 1), F32), jnp.zeros((m_rows, hd), F32))
        _, l_s, acc_s = lax.fori_loop(0, n_kt, sel_step, init)
        o_s = acc_s * (1.0 / jnp.where(l_s > 0, l_s, 1.0))

        pos_w = (ws + lax.broadcasted_iota(jnp.int32, (1, wlen), 1)).astype(F32)
        dist = pos_q - pos_w
        s = lax.dot_general(q, win_ref[pl.ds(ws, wlen), kcols], nt_dims, preferred_element_type=F32) - slope * dist
        p = _masked_softmax(s, (dist >= 0) & (dist < WINDOW), -1)
        o_w = jnp.dot(p.astype(BF16), win_ref[pl.ds(ws, wlen), vcols], preferred_element_type=F32)

        def gate(branch):
            return jnp.concatenate(
                [gt_ref[:, (g * rr + r) * N_BRANCH + branch:(g * rr + r) * N_BRANCH + branch + 1] for r in range(rr)],
                axis=0)

        o = gate(0) * o_c + gate(1) * o_s + gate(2) * o_w
        for r in range(rr):
            o_ref[:, (g * rr + r) * hd:(g * rr + r + 1) * hd] = o[r * qb:(r + 1) * qb].astype(BF16)


def nsa_prompt(q_bf16, gates_f32, kc, vc, sel_bf16, win_bf16):
    b, t, _ = q_bf16.shape
    qw = N_HEADS * LANES
    n_cp = kc.shape[1]
    nb = t // SEL_BLOCK
    assert t % SEL_KEY_TILE == 0 and t % Q_BLOCK == 0
    wlen = min(WINDOW + Q_BLOCK, t)
    cmap = np.zeros((n_cp, nb), np.float32)
    cmap[:n_cp - 1] = _cmp_to_sel_map(n_cp - 1, nb)
    expand = (np.arange(t)[None, :] // SEL_BLOCK == np.arange(nb)[:, None]).astype(np.float32)
    kern = functools.partial(_nsa_prompt_kernel, wlen=wlen)
    return pl.pallas_call(
        kern,
        out_shape=jax.ShapeDtypeStruct((b, t, qw), BF16),
        grid=(b, t // Q_BLOCK),
        in_specs=[pl.BlockSpec((None, Q_BLOCK, qw), lambda bi, i: (bi, i, 0)),
                  pl.BlockSpec((None, Q_BLOCK, LANES), lambda bi, i: (bi, i, qw // LANES)),
                  pl.BlockSpec((None, n_cp, kc.shape[2]), lambda bi, i: (bi, 0, 0)),
                  pl.BlockSpec((None, n_cp, kc.shape[2]), lambda bi, i: (bi, 0, 0)),
                  pl.BlockSpec((None, t, sel_bf16.shape[2]), lambda bi, i: (bi, 0, 0)),
                  pl.BlockSpec((None, t, win_bf16.shape[2]), lambda bi, i: (bi, 0, 0)),
                  pl.BlockSpec((n_cp, nb), lambda bi, i: (0, 0)),
                  pl.BlockSpec((nb, t), lambda bi, i: (0, 0))],
        out_specs=pl.BlockSpec((None, Q_BLOCK, qw), lambda bi, i: (bi, i, 0)),
        compiler_params=_cparams(2, 56),
        name="nsa_prompt",
    )(q_bf16, gates_f32, kc, vc, sel_bf16, win_bf16, jnp.asarray(cmap), jnp.asarray(expand, dtype=BF16))


SAMPLE_PAGES = 8
TN_DIMS = (((0,), (0,)), ((), ()))


def _lane_to_rows(v):
    return jnp.transpose(jnp.broadcast_to(v, (LANES, LANES)))


def _scale_rows(a, v):
    t = _lane_to_rows(v)
    return a * jnp.concatenate([t] * (a.shape[1] // LANES), axis=1)


def _nsa_sample_kernel(pt_ref, qb_ref, gt_ref, kc_ref, vc_ref, mapt_ref, rsum_ref, slope_ref, lq_ref,
                       selnew_ref, win_ref, *refs, n_pages, n_prev, nb, q_len, win_start):
    del pt_ref
    pages = refs[:n_pages]
    o_ref, imp_scr, sel_scr, m_scr, l_scr, acc_scr, oc_scr = refs[n_pages:]
    j = pl.program_id(1)
    hd = LANES
    kvw = N_KV_HEADS * hd
    kt = n_pages * PAGE_SIZE
    blocks_per_step = kt // SEL_BLOCK
    qb = qb_ref[...]
    slope = slope_ref[...]
    lq = lq_ref[...]
    pos_q = n_prev + lq

    def scores(k_bf16, pos_k):
        dist = pos_q - pos_k
        return jnp.dot(k_bf16, qb, preferred_element_type=F32) - slope * dist, dist

    def online(s, mask, v_bf16):
        s = jnp.where(mask, s, NEG_INF)
        m_old = m_scr[...]
        m_new = jnp.maximum(m_old, jnp.max(s, axis=0, keepdims=True))
        alpha = jnp.exp(m_old - m_new)
        p = jnp.where(mask, jnp.exp(s - m_new), 0.0)
        l_scr[...] = alpha * l_scr[...] + jnp.sum(p, axis=0, keepdims=True)
        m_scr[...] = m_new
        pv = lax.dot_general(p.astype(BF16), v_bf16, TN_DIMS, preferred_element_type=F32)
        acc_scr[...] = _scale_rows(acc_scr[...], alpha) + pv

    @pl.when(j == 0)
    def _():
        n_c = kc_ref.shape[0]
        nbp = imp_scr.shape[0]
        pos_c = (lax.broadcasted_iota(jnp.int32, (n_c, 1), 0) * CMP_STRIDE + (CMP_BLOCK - 1)).astype(F32)
        s, dist = scores(kc_ref[...], pos_c)
        p = _masked_softmax(s, dist >= 0, 0)
        oc_scr[...] = lax.dot_general(p.astype(BF16), vc_ref[...], TN_DIMS, preferred_element_type=F32)
        p_r = jnp.dot(p, rsum_ref[...], precision=lax.Precision.HIGHEST, preferred_element_type=F32)
        imp = jnp.dot(mapt_ref[...], p_r, precision=lax.Precision.HIGHEST, preferred_element_type=F32)
        blk = lax.broadcasted_iota(jnp.int32, (nbp, 1), 0)
        posq_i = n_prev + lq.astype(jnp.int32)
        cur = posq_i // SEL_BLOCK
        forced = (blk == 0) | (blk == cur) | (blk == cur - 1)
        valid = blk * SEL_BLOCK <= posq_i
        imp = jnp.where(valid, jnp.where(forced, imp + FORCE_BONUS, imp), NEG_INF)
        imp_scr[...] = imp

        def rank_step(c, cnt):
            rowc = imp_scr[pl.ds(c, 1), :]
            beats = (rowc > imp) | ((rowc == imp) & (c < blk))
            return cnt + jnp.where(beats, 1.0, 0.0)

        cnt = lax.fori_loop(0, nb, rank_step, jnp.zeros((nbp, LANES), F32))
        sel_scr[...] = jnp.where(cnt < min(N_SEL, nb), 1.0, 0.0)
        m_scr[...] = jnp.full(m_scr.shape, NEG_INF, F32)
        l_scr[...] = jnp.zeros(l_scr.shape, F32)
        acc_scr[...] = jnp.zeros(acc_scr.shape, F32)

    k_all = jnp.concatenate([pg[:, 0:kvw] for pg in pages], axis=0).astype(BF16)
    v_all = jnp.concatenate([pg[:, kvw:2 * kvw] for pg in pages], axis=0).astype(BF16)
    pos_k = (j * kt + lax.broadcasted_iota(jnp.int32, (kt, 1), 0)).astype(F32)
    s, dist = scores(k_all, pos_k)
    selrows = sel_scr[pl.ds(pl.multiple_of(j * blocks_per_step, blocks_per_step), blocks_per_step), :]
    selm = jnp.broadcast_to(selrows[:, None, :], (blocks_per_step, SEL_BLOCK, LANES)).reshape(kt, LANES)
    online(s, (dist >= 0) & (selm > 0.5), v_all)

    @pl.when(j == pl.num_programs(1) - 1)
    def _():
        n_new = selnew_ref.shape[0]
        pos_n = (n_prev + lax.broadcasted_iota(jnp.int32, (n_new, 1), 0)).astype(F32)
        s, dist = scores(selnew_ref[:, 0:kvw].astype(BF16), pos_n)
        selrow = sel_scr[n_prev // SEL_BLOCK:n_prev // SEL_BLOCK + 1, :]
        online(s, (dist >= 0) & (selrow > 0.5), selnew_ref[:, kvw:2 * kvw].astype(BF16))
        l_s = l_scr[...]
        o_s = _scale_rows(acc_scr[...], 1.0 / jnp.where(l_s > 0, l_s, 1.0))

        n_w = win_ref.shape[0]
        pos_w = (win_start + lax.broadcasted_iota(jnp.int32, (n_w, 1), 0)).astype(F32)
        s, dist = scores(win_ref[:, 0:kvw].astype(BF16), pos_w)
        p = _masked_softmax(s, (dist >= 0) & (dist < WINDOW), 0)
        o_w = lax.dot_general(p.astype(BF16), win_ref[:, kvw:2 * kvw].astype(BF16), TN_DIMS,
                              preferred_element_type=F32)
        o_c = oc_scr[...]

        rr = HEADS_PER_GROUP
        for g in range(N_KV_HEADS):
            for r in range(rr):
                h = g * rr + r
                rows = slice((g * rr + r) * q_len, (g * rr + r + 1) * q_len)
                cols = slice(g * hd, (g + 1) * hd)
                gc = gt_ref[:, h * N_BRANCH + 0:h * N_BRANCH + 1]
                gs = gt_ref[:, h * N_BRANCH + 1:h * N_BRANCH + 2]
                gw = gt_ref[:, h * N_BRANCH + 2:h * N_BRANCH + 3]
                o_ref[:, h * hd:(h + 1) * hd] = gc * o_c[rows, cols] + gs * o_s[rows, cols] + gw * o_w[rows, cols]


def nsa_sample(q_f32, gates_f32, kc, vc, sel_pages, page_table, sel_new, win_full, *, n_prev):
    b, q_len, qw = q_f32.shape
    hd = LANES
    rr = HEADS_PER_GROUP
    g_n = N_KV_HEADS
    assert g_n * rr * q_len == LANES and n_prev % SEL_BLOCK == 0 and q_len <= SEL_BLOCK
    npg = page_table.shape[1]
    n_p = min(SAMPLE_PAGES, npg)
    assert npg % n_p == 0 and (n_p * PAGE_SIZE // SEL_BLOCK) % SUBLANES == 0
    n_ch = kc.shape[1]
    nb = -(-(n_prev + q_len) // SEL_BLOCK)
    nbp = -(-nb // SUBLANES) * SUBLANES
    width = sel_pages.shape[2]

    qt = q_f32.reshape(b, q_len, g_n, rr, hd).transpose(0, 2, 4, 3, 1).reshape(b, g_n, hd, rr * q_len)
    qblk = jnp.einsum('bgdl,gh->bgdhl', qt, jnp.eye(g_n, dtype=F32)).reshape(b, g_n * hd, LANES).astype(BF16)

    lane = np.arange(LANES)
    lane_g, lane_r, lane_q = lane // (rr * q_len), (lane // q_len) % rr, lane % q_len
    slope = np.array([_alibi_slope(h) for h in lane_g * rr + lane_r], np.float32).reshape(1, LANES)
    lq = lane_q.astype(np.float32).reshape(1, LANES)
    rsum = ((lane_g[:, None] == lane_g[None, :]) & (lane_q[:, None] == lane_q[None, :])).astype(np.float32)
    mapt = np.zeros((nbp, n_ch), np.float32)
    mapt[:nb, :n_ch - 1] = _cmp_to_sel_map(n_ch - 1, nb).T

    pad16 = lambda a: jnp.pad(a, ((0, 0), (0, (-a.shape[1]) % BF16_ROWS), (0, 0)))
    sel_new_p = pad16(sel_new)
    win_p = pad16(win_full)

    def page_spec(k):
        return pl.BlockSpec((None, PAGE_SIZE, width), lambda bi, j, pt: (pt[bi * npg + j * n_p + k], 0, 0))

    const2 = lambda bi, j, pt: (0, 0)
    per_b = lambda bi, j, pt: (bi, 0, 0)
    grid_spec = pltpu.PrefetchScalarGridSpec(
        num_scalar_prefetch=1,
        grid=(b, npg // n_p),
        in_specs=[pl.BlockSpec((None, g_n * hd, LANES), per_b),
                  pl.BlockSpec((None, q_len, LANES), per_b),
                  pl.BlockSpec((None, n_ch, g_n * hd), per_b),
                  pl.BlockSpec((None, n_ch, g_n * hd), per_b),
                  pl.BlockSpec((nbp, n_ch), const2),
                  pl.BlockSpec((LANES, LANES), const2),
                  pl.BlockSpec((1, LANES), const2),
                  pl.BlockSpec((1, LANES), const2),
                  pl.BlockSpec((None, sel_new_p.shape[1], width), per_b),
                  pl.BlockSpec((None, win_p.shape[1], width), per_b)]
        + [page_spec(k) for k in range(n_p)],
        out_specs=pl.BlockSpec((None, q_len, qw), per_b),
        scratch_shapes=[pltpu.VMEM((nbp, LANES), F32), pltpu.VMEM((nbp, LANES), F32),
                        pltpu.VMEM((1, LANES), F32), pltpu.VMEM((1, LANES), F32),
                        pltpu.VMEM((LANES, g_n * hd), F32), pltpu.VMEM((LANES, g_n * hd), F32)],
    )
    kern = functools.partial(_nsa_sample_kernel, n_pages=n_p, n_prev=n_prev, nb=nb, q_len=q_len,
                             win_start=n_prev + q_len - win_full.shape[1])
    return pl.pallas_call(
        kern,
        out_shape=jax.ShapeDtypeStruct((b, q_len, qw), F32),
        grid_spec=grid_spec,
        compiler_params=_cparams(2, 48),
        name="nsa_sample",
    )(page_table.reshape(-1), qblk, gates_f32, kc, vc, jnp.asarray(mapt), jnp.asarray(rsum), jnp.asarray(slope),
      jnp.asarray(lq), sel_new_p, win_p, *([sel_pages] * n_p))


PROMPT_TILE = 512


def _trunk(x, params, *, n_prev, pool_prefix, conv_prefix, cmp_pages, cmp_table, sel_pages, sel_table, win_past,
           seq_tile, seqs_per_tile):
    p = params
    b, t, d = x.shape
    depth = p['ln_mix'].shape[0]
    n_a = p['w_pool'].shape[0]
    hd = LANES
    kvw = N_KV_HEADS * hd
    row_w = 2 * kvw
    is_prompt = cmp_pages is None
    no_halo = jnp.zeros((b, max(POOL_HALO, FFN_HALO), d), F32)
    pool_rows, conv_rows = [], []
    for l in range(depth):
        if l < n_a:
            x, rows = pool_layer(x, x if is_prompt else no_halo, pool_prefix[l], p['ln_mix'][l], p['w_pool'][l], p['pool_scale'][l],
                                 n_prev=n_prev, tm=seq_tile)
            pool_rows.append(rows)
        else:
            xm = x.reshape(b * t, d)
            if l == n_a:
                cmp_f, _ = norm_linear(xm, p['ln_kv'], p['w_kv'][:, 0:row_w], p['k_norm'][0], n_norm=0)
                sel_f, sel_b = norm_linear(xm, p['ln_kv'], p['w_kv'][:, row_w:2 * row_w], p['k_norm'][1],
                                           n_norm=N_KV_HEADS)
                win_f, win_b = norm_linear(xm, p['ln_kv'], p['w_kv'][:, 2 * row_w:3 * row_w], p['k_norm'][2],
                                           n_norm=N_KV_HEADS)
                if is_prompt:
                    cmp_pages = cmp_f.reshape(b * t // PAGE_SIZE, PAGE_SIZE, row_w)
                    cmp_table = jnp.arange(b * t // PAGE_SIZE, dtype=jnp.int32).reshape(b, t // PAGE_SIZE)
                    win_full = win_f.reshape(b, t, row_w)
                else:
                    win_full = jnp.concatenate([win_past, win_f.reshape(b, t, row_w)], axis=1)
                pab = compress_partial(cmp_pages, cmp_table, p['w1cat'])
                kc, vc = compress_finish(pab, p['cmp_pos'], p['w1flat'], p['cmp_w2'], p['k_norm'][0])
            j = l - n_a
            q_f, q_b = norm_linear(xm, p['ln_mix'][l], p['w_qg'][j], p['q_norm'][j], n_norm=N_HEADS,
                                   norm_scale=hd ** -0.5, sigmoid_from=N_HEADS)
            qw = N_HEADS * hd
            if is_prompt:
                o = nsa_prompt(q_b.reshape(b, t, -1), q_f.reshape(b, t, -1), kc, vc,
                               sel_b.reshape(b, t, row_w), win_b.reshape(b, t, row_w))
            else:
                q3 = q_f.reshape(b, t, -1)
                o = nsa_sample(q3[:, :, :qw], q3[:, :, qw:], kc, vc, sel_pages, sel_table,
                               sel_f.reshape(b, t, row_w), win_full, n_prev=n_prev).astype(BF16)
            x = linear_residual(o.reshape(b * t, qw), p['w_o'][j], xm).reshape(b, t, d)
        x, rows = ffn_layer(x, x if is_prompt else no_halo, conv_prefix[l], p['ln_ffn'][l], p['w_in'][l], p['conv_w'][l], p['conv_b'][l],
                            p['w_out'][l], nb=seqs_per_tile, tm=seq_tile)
        conv_rows.append(rows[:, FFN_HALO - (CONV_WIDTH - 1):])
    row_shape = (2, N_KV_HEADS, hd)
    n_win = min(WINDOW, n_prev + t)
    win_state = win_full[:, win_full.shape[1] - n_win:].reshape((b, n_win) + row_shape)
    return (x, cmp_f.reshape((b, t) + row_shape), sel_f.reshape((b, t) + row_shape), win_state,
            jnp.stack(pool_rows), jnp.stack(conv_rows))


def kernel(x_prompt, x_sample, cache_kv_cmp, cache_kv_sel, page_table, state_kv_win, state_pool, state_conv, ln_mix, ln_ffn, w_pool, pool_scale, ln_kv, w_kv, k_norm, cmp_w1, cmp_pos, cmp_w2, w_qg, q_norm, w_o, w_in, conv_w, conv_b, w_out):
    depth = ln_mix.shape[0]
    n_a = w_pool.shape[0]
    d = x_prompt.shape[2]
    ff2 = w_in.shape[2]
    hd = LANES
    row_w = 2 * N_KV_HEADS * hd
    qg_pad = (-w_qg.shape[2]) % LANES
    half = CMP_BLOCK // 2
    params = dict(
        ln_mix=ln_mix, ln_ffn=ln_ffn, pool_scale=pool_scale, ln_kv=ln_kv, k_norm=k_norm, q_norm=q_norm,
        conv_w=conv_w, conv_b=conv_b, cmp_pos=cmp_pos,
        w_pool=w_pool.astype(BF16), w_kv=w_kv.astype(BF16),
        w_qg=jnp.pad(w_qg, ((0, 0), (0, 0), (0, qg_pad))).astype(BF16),
        w_o=w_o.astype(BF16), w_in=w_in.astype(BF16), w_out=w_out.astype(BF16),
        cmp_w2=cmp_w2.astype(BF16),
        w1cat=jnp.concatenate([cmp_w1[:, :half], cmp_w1[:, half:]], axis=-1).astype(BF16),
        w1flat=cmp_w1.reshape(2, CMP_BLOCK * hd, cmp_w1.shape[3]).astype(BF16),
    )

    b_p, t_p, _ = x_prompt.shape
    prompt = _trunk(
        x_prompt, params, n_prev=0,
        pool_prefix=jnp.zeros((n_a, b_p, POOL_HALO, d), F32),
        conv_prefix=jnp.zeros((depth, b_p, FFN_HALO, ff2), F32),
        cmp_pages=None, cmp_table=None, sel_pages=None, sel_table=None, win_past=None,
        seq_tile=min(PROMPT_TILE, t_p), seqs_per_tile=1)

    b_s, t_s, _ = x_sample.shape
    n_prev = page_table.shape[1] * PAGE_SIZE
    assert t_s < CMP_STRIDE and cache_kv_cmp.shape[1] == PAGE_SIZE
    sample = _trunk(
        x_sample, params, n_prev=n_prev,
        pool_prefix=jnp.pad(state_pool, ((0, 0), (0, 0), (POOL_HALO - POOL_KEEP, 0), (0, 0))),
        conv_prefix=jnp.pad(state_conv, ((0, 0), (0, 0), (FFN_HALO - (CONV_WIDTH - 1), 0), (0, 0))),
        cmp_pages=cache_kv_cmp.reshape(cache_kv_cmp.shape[0], PAGE_SIZE, row_w), cmp_table=page_table,
        sel_pages=cache_kv_sel.reshape(cache_kv_sel.shape[0], PAGE_SIZE, row_w), sel_table=page_table,
        win_past=state_kv_win.reshape(b_s, state_kv_win.shape[1], row_w),
        seq_tile=t_s, seqs_per_tile=b_s)

    return tuple(leaf for pair in zip(prompt, sample) for leaf in pair)
```

```python
import functools

import numpy as np
import jax
import jax.numpy as jnp
from jax import lax
from jax.experimental import pallas as pl
from jax.experimental.pallas import tpu as pltpu

F32 = jnp.float32
BF16 = jnp.bfloat16

POOL_WINDOWS = (2, 4, 8, 16)
POOL_KEEP = max(POOL_WINDOWS) - 1
N_HEADS = 16
N_KV_HEADS = 4
HEADS_PER_GROUP = N_HEADS // N_KV_HEADS
N_BRANCH = 3
CMP_BLOCK = 32
CMP_STRIDE = 16
SEL_BLOCK = 64
N_SEL = 16
WINDOW = 512
FORCE_BONUS = 1e4
CONV_WIDTH = 3
Q_BLOCK = 128
PAGE_SIZE = 128
EPS = 1e-6
NEG_INF = -1e30

LANES = 128
SUBLANES = 8
BF16_ROWS = 16
MIB = 2 ** 20


def _cparams(n_axes, vmem_mib):
    return pltpu.CompilerParams(dimension_semantics=("arbitrary",) * n_axes,
                                vmem_limit_bytes=vmem_mib * MIB)


def _rms(x, g):
    return (x * lax.rsqrt(jnp.mean(x * x, axis=-1, keepdims=True) + EPS)) * g


def _alibi_slope(h):
    return float(np.float32(2.0 ** (-8.0 * (h + 1) / N_HEADS)))


def _norm_linear_kernel(x_ref, g_ref, w_ref, hg_ref, of_ref, ob_ref, *, n_norm, norm_scale, sigmoid_from):
    h = _rms(x_ref[...], g_ref[...]).astype(BF16)
    y = jnp.dot(h, w_ref[...], preferred_element_type=F32)
    for c in range(y.shape[1] // LANES):
        cols = slice(c * LANES, (c + 1) * LANES)
        yc = y[:, cols]
        if c < n_norm:
            yc = _rms(yc, hg_ref[...])
            if norm_scale is not None:
                yc = yc * norm_scale
        elif sigmoid_from is not None and c >= sigmoid_from:
            yc = jax.nn.sigmoid(yc)
        of_ref[:, cols] = yc
        ob_ref[:, cols] = yc.astype(BF16)


def norm_linear(x, gain, w_bf16, head_gain, *, n_norm=0, norm_scale=None, sigmoid_from=None, tm=512):
    m, k = x.shape
    n = w_bf16.shape[1]
    tm = min(tm, m)
    assert m % tm == 0 and n % LANES == 0
    kern = functools.partial(_norm_linear_kernel, n_norm=n_norm, norm_scale=norm_scale, sigmoid_from=sigmoid_from)
    return pl.pallas_call(
        kern,
        out_shape=(jax.ShapeDtypeStruct((m, n), F32), jax.ShapeDtypeStruct((m, n), BF16)),
        grid=(m // tm,),
        in_specs=[pl.BlockSpec((tm, k), lambda i: (i, 0)),
                  pl.BlockSpec((1, k), lambda i: (0, 0)),
                  pl.BlockSpec((k, n), lambda i: (0, 0)),
                  pl.BlockSpec((1, LANES), lambda i: (0, 0))],
        out_specs=(pl.BlockSpec((tm, n), lambda i: (i, 0)), pl.BlockSpec((tm, n), lambda i: (i, 0))),
        compiler_params=_cparams(1, 48),
        name="norm_linear",
    )(x, gain.reshape(1, k), w_bf16, head_gain.reshape(1, LANES))


def _linear_residual_kernel(a_ref, w_ref, r_ref, o_ref):
    o_ref[...] = r_ref[...] + jnp.dot(a_ref[...], w_ref[...], preferred_element_type=F32)


def linear_residual(a_bf16, w_bf16, res, *, tm=512):
    m, k = a_bf16.shape
    n = w_bf16.shape[1]
    tm = min(tm, m)
    assert m % tm == 0
    return pl.pallas_call(
        _linear_residual_kernel,
        out_shape=jax.ShapeDtypeStruct((m, n), F32),
        grid=(m // tm,),
        in_specs=[pl.BlockSpec((tm, k), lambda i: (i, 0)),
                  pl.BlockSpec((k, n), lambda i: (0, 0)),
                  pl.BlockSpec((tm, n), lambda i: (i, 0))],
        out_specs=pl.BlockSpec((tm, n), lambda i: (i, 0)),
        compiler_params=_cparams(1, 48),
        name="linear_residual",
    )(a_bf16, w_bf16, res)


POOL_HALO = 16


def _pool_kernel(x_ref, xh_ref, pre_ref, g_ref, w_ref, sc_ref, o_ref, st_ref, ctx_scr, *, tm, n_prev):
    i = pl.program_id(1)
    x = x_ref[...]
    h = _rms(x, g_ref[...])
    halo = _rms(xh_ref[...], g_ref[...])
    ctx_scr[0:POOL_HALO, :] = jnp.where(i == 0, pre_ref[...], halo)
    ctx_scr[POOL_HALO:POOL_HALO + tm, :] = h
    t = i * tm + lax.broadcasted_iota(jnp.int32, (tm, 1), 0)
    gw = x.shape[1] // len(POOL_WINDOWS)
    for gi, w in enumerate(POOL_WINDOWS):
        cols = slice(gi * gw, (gi + 1) * gw)
        hs = h[:, cols]
        acc = hs
        for k in range(1, w):
            acc = acc + ctx_scr[POOL_HALO - k:POOL_HALO - k + tm, cols]
        cnt = jnp.minimum(w, n_prev + t + 1).astype(F32)
        pooled = acc / cnt - hs
        mixed = jnp.dot(pooled.astype(BF16), w_ref[gi], preferred_element_type=F32)
        o_ref[:, cols] = x[:, cols] + mixed * sc_ref[:, cols]
    st_ref[...] = ctx_scr[tm + POOL_HALO - POOL_KEEP:tm + POOL_HALO, :]


def pool_layer(x, x_halo_src, prefix16, gain, w_pool_bf16, scale, *, n_prev, tm):
    b, t, d = x.shape
    assert t % tm == 0 and tm % SUBLANES == 0
    hb = tm // POOL_HALO
    kern = functools.partial(_pool_kernel, tm=tm, n_prev=n_prev)
    return pl.pallas_call(
        kern,
        out_shape=(jax.ShapeDtypeStruct((b, t, d), F32), jax.ShapeDtypeStruct((b, POOL_KEEP, d), F32)),
        grid=(b, t // tm),
        in_specs=[pl.BlockSpec((None, tm, d), lambda bi, i: (bi, i, 0)),
                  pl.BlockSpec((None, POOL_HALO, d), lambda bi, i: (bi, jnp.maximum(i * hb - 1, 0), 0)),
                  pl.BlockSpec((None, POOL_HALO, d), lambda bi, i: (bi, 0, 0)),
                  pl.BlockSpec((1, d), lambda bi, i: (0, 0)),
                  pl.BlockSpec(w_pool_bf16.shape, lambda bi, i: (0, 0, 0)),
                  pl.BlockSpec((1, d), lambda bi, i: (0, 0))],
        out_specs=(pl.BlockSpec((None, tm, d), lambda bi, i: (bi, i, 0)),
                   pl.BlockSpec((None, POOL_KEEP, d), lambda bi, i: (bi, 0, 0))),
        scratch_shapes=[pltpu.VMEM((tm + POOL_HALO, d), F32)],
        compiler_params=_cparams(2, 48),
        name="pool_layer",
    )(x, x_halo_src, prefix16, gain.reshape(1, d), w_pool_bf16, scale.reshape(1, d))


FFN_HALO = 16


def _ffn_kernel(x_ref, xh_ref, pg_ref, pv_ref, g_ref, wig_ref, wiv_ref, cwg_ref, cwv_ref, cbg_ref, cbv_ref, wo_ref,
                o_ref, sg_ref, sv_ref, u_scr, hg_scr, hv_scr, acc_scr, *, nb, tm, te):
    i = pl.program_id(1)
    f = pl.program_id(2)
    n_ext = nb * te
    n_out = n_ext - FFN_HALO

    @pl.when(f == 0)
    def _():
        for s in range(nb):
            u_scr[s * te:s * te + FFN_HALO, :] = _rms(xh_ref[s], g_ref[...]).astype(BF16)
            u_scr[s * te + FFN_HALO:s * te + FFN_HALO + tm, :] = _rms(x_ref[s], g_ref[...]).astype(BF16)
            if te > FFN_HALO + tm:
                u_scr[s * te + FFN_HALO + tm:(s + 1) * te, :] = jnp.zeros((te - FFN_HALO - tm, u_scr.shape[1]), BF16)
        acc_scr[...] = jnp.zeros_like(acc_scr)

    u = u_scr[...]
    hg_scr[...] = jnp.dot(u, wig_ref[...], preferred_element_type=F32)
    hv_scr[...] = jnp.dot(u, wiv_ref[...], preferred_element_type=F32)

    @pl.when(i == 0)
    def _():
        for s in range(nb):
            hg_scr[s * te:s * te + FFN_HALO, :] = pg_ref[s]
            hv_scr[s * te:s * te + FFN_HALO, :] = pv_ref[s]

    def conv(h_scr, cw_ref, cb_ref):
        c = cb_ref[...] + cw_ref[0:1, :] * h_scr[FFN_HALO - 2:FFN_HALO - 2 + n_out, :]
        c = c + cw_ref[1:2, :] * h_scr[FFN_HALO - 1:FFN_HALO - 1 + n_out, :]
        return c + cw_ref[2:3, :] * h_scr[FFN_HALO:FFN_HALO + n_out, :]

    cg = conv(hg_scr, cwg_ref, cbg_ref)
    cv = conv(hv_scr, cwv_ref, cbv_ref)
    act = (cg * jax.nn.sigmoid(cg)) * cv
    acc_scr[...] += jnp.dot(act.astype(BF16), wo_ref[...], preferred_element_type=F32)

    for s in range(nb):
        sg_ref[s] = hg_scr[s * te + tm:s * te + tm + FFN_HALO, :]
        sv_ref[s] = hv_scr[s * te + tm:s * te + tm + FFN_HALO, :]

    @pl.when(f == pl.num_programs(2) - 1)
    def _():
        for s in range(nb):
            o_ref[s] = x_ref[s] + acc_scr[s * te:s * te + tm, :]


def ffn_layer(x, x_halo_src, prefix16, gain, w_in_bf16, conv_w, conv_b, w_out_bf16, *, nb, tm, tf=512):
    b, t, d = x.shape
    ff = w_out_bf16.shape[0]
    assert t % tm == 0 and b % nb == 0 and ff % tf == 0 and tm % SUBLANES == 0
    nf = ff // tf
    hb = tm // FFN_HALO
    te = FFN_HALO + -(-tm // BF16_ROWS) * BF16_ROWS
    n_ext = nb * te
    kern = functools.partial(_ffn_kernel, nb=nb, tm=tm, te=te)
    gate_col = lambda bi, i, f: (0, f)
    val_col = lambda bi, i, f: (0, nf + f)
    y, sg, sv = pl.pallas_call(
        kern,
        out_shape=(jax.ShapeDtypeStruct((b, t, d), F32),
                   jax.ShapeDtypeStruct((b, t // tm, FFN_HALO, ff), F32),
                   jax.ShapeDtypeStruct((b, t // tm, FFN_HALO, ff), F32)),
        grid=(b // nb, t // tm, nf),
        in_specs=[pl.BlockSpec((nb, tm, d), lambda bi, i, f: (bi, i, 0)),
                  pl.BlockSpec((nb, FFN_HALO, d), lambda bi, i, f: (bi, jnp.maximum(i * hb - 1, 0), 0)),
                  pl.BlockSpec((nb, FFN_HALO, tf), lambda bi, i, f: (bi, 0, f)),
                  pl.BlockSpec((nb, FFN_HALO, tf), lambda bi, i, f: (bi, 0, nf + f)),
                  pl.BlockSpec((1, d), lambda bi, i, f: (0, 0)),
                  pl.BlockSpec((d, tf), gate_col),
                  pl.BlockSpec((d, tf), val_col),
                  pl.BlockSpec((CONV_WIDTH, tf), gate_col),
                  pl.BlockSpec((CONV_WIDTH, tf), val_col),
                  pl.BlockSpec((1, tf), gate_col),
                  pl.BlockSpec((1, tf), val_col),
                  pl.BlockSpec((tf, d), lambda bi, i, f: (f, 0))],
        out_specs=(pl.BlockSpec((nb, tm, d), lambda bi, i, f: (bi, i, 0)),
                   pl.BlockSpec((nb, None, FFN_HALO, tf), lambda bi, i, f: (bi, i, 0, f)),
                   pl.BlockSpec((nb, None, FFN_HALO, tf), lambda bi, i, f: (bi, i, 0, f))),
        scratch_shapes=[pltpu.VMEM((n_ext, d), BF16),
                        pltpu.VMEM((n_ext, tf), F32),
                        pltpu.VMEM((n_ext, tf), F32),
                        pltpu.VMEM((n_ext - FFN_HALO, d), F32)],
        compiler_params=_cparams(3, 56),
        name="ffn_layer",
    )(x, x_halo_src, prefix16, prefix16, gain.reshape(1, d), w_in_bf16, w_in_bf16, conv_w, conv_w,
      conv_b.reshape(1, 2 * ff), conv_b.reshape(1, 2 * ff), w_out_bf16)
    return y, jnp.concatenate([sg[:, -1], sv[:, -1]], axis=-1)


CMP_PAGES = 16
CHUNKS_PER_PAGE = PAGE_SIZE // CMP_STRIDE


def _compress_kernel(pt_ref, *refs, n_pages):
    del pt_ref
    page_refs = refs[:n_pages]
    w_ref, o_ref = refs[n_pages], refs[n_pages + 1]
    rows = n_pages * CHUNKS_PER_PAGE
    heads = 2 * N_KV_HEADS
    for v in range(2):
        acc = None
        for s in range(CMP_STRIDE):
            pieces = []
            for g in range(N_KV_HEADS):
                r0 = s * heads + v * N_KV_HEADS + g
                for k in range(n_pages):
                    pieces.append(page_refs[k][pl.ds(r0, CHUNKS_PER_PAGE, stride=CMP_STRIDE * heads), :])
            lhs = jnp.concatenate(pieces, axis=0).astype(BF16)
            d = jnp.dot(lhs, w_ref[v, s], preferred_element_type=F32)
            acc = d if acc is None else acc + d
        for g in range(N_KV_HEADS):
            o_ref[v, g] = acc[g * rows:(g + 1) * rows, :]


def compress_partial(pages, page_table, w1cat_bf16):
    s, npg = page_table.shape
    page_rows, hd = pages.shape[1:]
    assert page_rows == PAGE_SIZE * 2 * N_KV_HEADS
    hid2 = w1cat_bf16.shape[3]
    n_p = min(CMP_PAGES, npg)
    assert npg % n_p == 0
    n_ch = npg * CHUNKS_PER_PAGE

    def page_spec(k):
        return pl.BlockSpec((None, page_rows, hd), lambda si, j, pt: (pt[si * npg + j * n_p + k], 0, 0))

    grid_spec = pltpu.PrefetchScalarGridSpec(
        num_scalar_prefetch=1,
        grid=(s, npg // n_p),
        in_specs=[page_spec(k) for k in range(n_p)]
        + [pl.BlockSpec(w1cat_bf16.shape, lambda si, j, pt: (0, 0, 0, 0))],
        out_specs=pl.BlockSpec((None, 2, N_KV_HEADS, n_p * CHUNKS_PER_PAGE, hid2), lambda si, j, pt: (si, 0, 0, j, 0)),
    )
    return pl.pallas_call(
        functools.partial(_compress_kernel, n_pages=n_p),
        out_shape=jax.ShapeDtypeStruct((s, 2, N_KV_HEADS, n_ch, hid2), F32),
        grid_spec=grid_spec,
        compiler_params=_cparams(2, 48),
        name="compress_partial",
    )(page_table.reshape(-1), *([pages] * n_p), w1cat_bf16)


def _compress_finish_kernel(pab_ref, pos_ref, w1_ref, w2_ref, kn_ref, kc_ref, vc_ref):
    n = pab_ref.shape[1]
    hid = w2_ref.shape[1]
    for v in range(2):
        blk = pab_ref[v]
        nxt = pltpu.roll(blk[:, hid:], n - 1, 0)
        pos = jnp.broadcast_to(pos_ref[v], (SUBLANES, pos_ref.shape[2])).astype(BF16)
        bias = jnp.dot(pos, w1_ref[v], preferred_element_type=F32)[0:1, :]
        hdn = jax.nn.gelu(blk[:, :hid] + nxt + bias)
        out = jnp.dot(hdn.astype(BF16), w2_ref[v], preferred_element_type=F32)
        if v == 0:
            kc_ref[...] = _rms(out, kn_ref[...]).astype(BF16)
        else:
            vc_ref[...] = out.astype(BF16)


def compress_finish(pab, cmp_pos, w1flat_bf16, w2_bf16, k_gain):
    s, _, g, n_ch, hid2 = pab.shape
    hd = w2_bf16.shape[2]
    kdim = w1flat_bf16.shape[1]
    out = jax.ShapeDtypeStruct((s, n_ch, g * hd), BF16)
    return pl.pallas_call(
        _compress_finish_kernel,
        out_shape=(out, out),
        grid=(s, g),
        in_specs=[pl.BlockSpec((None, 2, None, n_ch, hid2), lambda si, gi: (si, 0, gi, 0, 0)),
                  pl.BlockSpec((2, 1, kdim), lambda si, gi: (0, 0, 0)),
                  pl.BlockSpec(w1flat_bf16.shape, lambda si, gi: (0, 0, 0)),
                  pl.BlockSpec(w2_bf16.shape, lambda si, gi: (0, 0, 0)),
                  pl.BlockSpec((1, hd), lambda si, gi: (0, 0))],
        out_specs=(pl.BlockSpec((None, n_ch, hd), lambda si, gi: (si, 0, gi)),
                   pl.BlockSpec((None, n_ch, hd), lambda si, gi: (si, 0, gi))),
        compiler_params=_cparams(2, 48),
        name="compress_finish",
    )(pab, cmp_pos.reshape(2, 1, kdim), w1flat_bf16, w2_bf16, k_gain.reshape(1, hd))


def _cmp_to_sel_map(nc, nb):
    c_s = np.arange(nc) * CMP_STRIDE
    c_e = c_s + CMP_BLOCK - 1
    s_s = np.arange(nb) * SEL_BLOCK
    s_e = s_s + SEL_BLOCK - 1
    return ((c_s[:, None] <= s_e[None]) & (c_e[:, None] >= s_s[None])).astype(np.float32)


def _masked_softmax(s, mask, axis):
    s = jnp.where(mask, s, NEG_INF)
    m = jnp.max(s, axis=axis, keepdims=True)
    p = jnp.where(mask, jnp.exp(s - m), 0.0)
    den = jnp.sum(p, axis=axis, keepdims=True)
    return p * (1.0 / jnp.where(den > 0, den, 1.0))


SEL_KEY_TILE = 512


POS_SPLIT = 3
BLOCK_LANES = 64
MASK_BIAS = -1e30


def _position_columns(pos, onehot):
    pos = np.asarray(pos)
    assert pos.max() // SEL_BLOCK <= 256
    out = np.zeros((pos.shape[0], LANES), np.float32)
    if onehot:
        assert pos.max() // SEL_BLOCK < BLOCK_LANES
        out[np.arange(pos.shape[0]), pos // SEL_BLOCK] = 1.0
    for j in range(POS_SPLIT):
        out[:, BLOCK_LANES + j] = (pos // SEL_BLOCK) * SEL_BLOCK
        out[:, BLOCK_LANES + POS_SPLIT + j] = pos % SEL_BLOCK
    return out


def _slope_columns():
    out = np.zeros((N_HEADS, LANES), np.float32)
    for h in range(N_HEADS):
        rem = np.float32(_alibi_slope(h))
        for j in range(POS_SPLIT):
            part = np.float32(rem.astype(jnp.bfloat16))
            out[h, BLOCK_LANES + j] = part
            out[h, BLOCK_LANES + POS_SPLIT + j] = part
            rem = np.float32(rem - part)
        assert rem == 0
    return out


def _nsa_prompt_kernel(q_ref, gt_ref, kc_ref, vc_ref, cpos_ref, selk_ref, selv_ref, wink_ref, winv_ref,
                       spos_ref, wpos_ref, mapt_ref, slope_ref, o_ref, *, wlen):
    i = pl.program_id(1)
    s0 = i * Q_BLOCK
    hd = LANES
    rr, qb, tk = HEADS_PER_GROUP, Q_BLOCK, SEL_KEY_TILE
    m_rows = rr * qb
    n_cp = kc_ref.shape[0]
    nb = mapt_ref.shape[0]
    nt_dims = (((1,), (1,)), ((), ()))
    kt_last = s0 // tk
    ws = pl.multiple_of(jnp.maximum(s0 - WINDOW, 0), Q_BLOCK)

    row = lax.broadcasted_iota(jnp.int32, (m_rows, 1), 0)
    pos_q = s0 + (row & (qb - 1))
    posq_l = s0 + lax.broadcasted_iota(jnp.int32, (1, qb), 1)
    blk = lax.broadcasted_iota(jnp.int32, (nb, 1), 0)
    cur = posq_l // SEL_BLOCK
    forced = (blk == 0) | (blk == cur) | (blk == cur - 1)
    valid = blk * SEL_BLOCK <= posq_l
    cmask = lax.broadcasted_iota(jnp.int32, (1, n_cp), 1) * CMP_STRIDE + (CMP_BLOCK - 1) <= pos_q
    dist_w = pos_q - (ws + lax.broadcasted_iota(jnp.int32, (1, wlen), 1))
    wmask = (dist_w >= 0) & (dist_w < WINDOW)

    for g in range(N_KV_HEADS):
        cols = slice(g * hd, (g + 1) * hd)
        q = jnp.concatenate([q_ref[:, (g * rr + r) * hd:(g * rr + r + 1) * hd] for r in range(rr)], axis=0)
        slope_cols = slope_ref[g]
        q_pos = jnp.concatenate([q, slope_cols.astype(BF16)], axis=1)

        kc_aug = jnp.concatenate([kc_ref[:, cols], cpos_ref[...]], axis=1)
        s = lax.dot_general(q_pos, kc_aug, nt_dims, preferred_element_type=F32)
        p = _masked_softmax(s, cmask, -1)
        o_c = jnp.dot(p.astype(BF16), vc_ref[:, cols], preferred_element_type=F32)

        p_sum = p[0:qb]
        for r in range(1, rr):
            p_sum = p_sum + p[r * qb:(r + 1) * qb]
        imp = lax.dot_general(mapt_ref[...], p_sum, nt_dims, precision=lax.Precision.HIGHEST,
                              preferred_element_type=F32)
        imp = jnp.where(valid, jnp.where(forced, imp + FORCE_BONUS, imp), NEG_INF)
        cnt = jnp.zeros((nb, qb), F32)
        for c in range(nb):
            rowc = imp[c:c + 1, :]
            beats = (rowc > imp) | ((rowc == imp) & (blk > c))
            cnt = cnt + jnp.where(beats, 1.0, 0.0)
        bias_t = jnp.where(cnt < min(N_SEL, nb), 0.0, MASK_BIAS)
        bias = jnp.transpose(jnp.concatenate([bias_t, jnp.zeros((hd - nb, qb), F32)], axis=0))
        q_sel = jnp.concatenate([q, (slope_cols + jnp.concatenate([bias] * rr, axis=0)).astype(BF16)], axis=1)

        def sel_tile(kt, carry, causal):
            m_i, l_i, acc = carry
            k0 = pl.multiple_of(kt * tk, tk)
            k_aug = jnp.concatenate([selk_ref[pl.ds(k0, tk), cols], spos_ref[pl.ds(k0, tk), :]], axis=1)
            s = lax.dot_general(q_sel, k_aug, nt_dims, preferred_element_type=F32)
            if causal:
                s = jnp.where(k0 + lax.broadcasted_iota(jnp.int32, (1, tk), 1) <= pos_q, s, NEG_INF)
            m_new = jnp.maximum(m_i, jnp.max(s, axis=-1, keepdims=True))
            alpha = jnp.exp(m_i - m_new)
            p = jnp.exp(s - m_new)
            l_new = alpha * l_i + jnp.sum(p, axis=-1, keepdims=True)
            acc = alpha * acc + jnp.dot(p.astype(BF16), selv_ref[pl.ds(k0, tk), cols], preferred_element_type=F32)
            return m_new, l_new, acc

        init = (jnp.full((m_rows, 1), NEG_INF, F32), jnp.zeros((m_rows, 1), F32), jnp.zeros((m_rows, hd), F32))
        carry = lax.fori_loop(0, kt_last, functools.partial(sel_tile, causal=False), init)
        _, l_s, acc_s = sel_tile(kt_last, carry, True)
        o_s = acc_s * (1.0 / jnp.where(l_s > 0, l_s, 1.0))

        kw_aug = jnp.concatenate([wink_ref[pl.ds(ws, wlen), cols], wpos_ref[pl.ds(ws, wlen), :]], axis=1)
        s = lax.dot_general(q_pos, kw_aug, nt_dims, preferred_element_type=F32)
        p = _masked_softmax(s, wmask, -1)
        o_w = jnp.dot(p.astype(BF16), winv_ref[pl.ds(ws, wlen), cols], preferred_element_type=F32)

        def gate(branch):
            return jnp.concatenate(
                [gt_ref[:, (g * rr + r) * N_BRANCH + branch:(g * rr + r) * N_BRANCH + branch + 1] for r in range(rr)],
                axis=0)

        o = gate(0) * o_c + gate(1) * o_s + gate(2) * o_w
        for r in range(rr):
            o_ref[:, (g * rr + r) * hd:(g * rr + r + 1) * hd] = o[r * qb:(r + 1) * qb].astype(BF16)


def nsa_prompt(q_bf16, gates_f32, kc, vc, sel_bf16, win_bf16):
    b, t, _ = q_bf16.shape
    qw = N_HEADS * LANES
    n_cp = kc.shape[1]
    nb = t // SEL_BLOCK
    assert t % SEL_KEY_TILE == 0 and t % Q_BLOCK == 0
    wlen = min(WINDOW + Q_BLOCK, t)
    kvw = N_KV_HEADS * LANES
    mapt = np.zeros((nb, n_cp), np.float32)
    mapt[:, :n_cp - 1] = _cmp_to_sel_map(n_cp - 1, nb).T
    cpos = _position_columns(np.arange(n_cp) * CMP_STRIDE + (CMP_BLOCK - 1), onehot=False)
    spos = _position_columns(np.arange(t), onehot=True)
    wpos = _position_columns(np.arange(t), onehot=False)
    slope = np.repeat(_slope_columns().reshape(N_KV_HEADS, HEADS_PER_GROUP, 1, LANES), Q_BLOCK, axis=2)
    slope = slope.reshape(N_KV_HEADS, HEADS_PER_GROUP * Q_BLOCK, LANES)
    kern = functools.partial(_nsa_prompt_kernel, wlen=wlen)
    per_b = lambda bi, i: (bi, 0, 0)
    per_b_v = lambda bi, i: (bi, 0, 1)
    const2 = lambda bi, i: (0, 0)
    return pl.pallas_call(
        kern,
        out_shape=jax.ShapeDtypeStruct((b, t, qw), BF16),
        grid=(b, t // Q_BLOCK),
        in_specs=[pl.BlockSpec((None, Q_BLOCK, qw), lambda bi, i: (bi, i, 0)),
                  pl.BlockSpec((None, Q_BLOCK, LANES), lambda bi, i: (bi, i, qw // LANES)),
                  pl.BlockSpec((None, n_cp, kvw), per_b),
                  pl.BlockSpec((None, n_cp, kvw), per_b),
                  pl.BlockSpec((n_cp, LANES), const2),
                  pl.BlockSpec((None, t, kvw), per_b),
                  pl.BlockSpec((None, t, kvw), per_b_v),
                  pl.BlockSpec((None, t, kvw), per_b),
                  pl.BlockSpec((None, t, kvw), per_b_v),
                  pl.BlockSpec((t, LANES), const2),
                  pl.BlockSpec((t, LANES), const2),
                  pl.BlockSpec((nb, n_cp), const2),
                  pl.BlockSpec(slope.shape, lambda bi, i: (0, 0, 0))],
        out_specs=pl.BlockSpec((None, Q_BLOCK, qw), lambda bi, i: (bi, i, 0)),
        compiler_params=_cparams(2, 56),
        name="nsa_prompt",
    )(q_bf16, gates_f32, kc, vc, jnp.asarray(cpos, dtype=BF16), sel_bf16, sel_bf16, win_bf16, win_bf16,
      jnp.asarray(spos, dtype=BF16), jnp.asarray(wpos, dtype=BF16), jnp.asarray(mapt), jnp.asarray(slope))


SAMPLE_PAGES = 8
TN_DIMS = (((0,), (0,)), ((), ()))


def _lane_to_rows(v):
    return jnp.transpose(jnp.broadcast_to(v, (LANES, LANES)))


def _scale_rows(a, v):
    t = _lane_to_rows(v)
    return a * jnp.concatenate([t] * (a.shape[1] // LANES), axis=1)


def _nsa_sample_kernel(pt_ref, qb_ref, gt_ref, kc_ref, vc_ref, mapt_ref, rsum_ref, slope_ref, lq_ref,
                       selnew_ref, win_ref, *refs, n_pages, n_prev, nb, q_len, win_start):
    del pt_ref
    pages = refs[:n_pages]
    o_ref, imp_scr, sel_scr, m_scr, l_scr, acc_scr, oc_scr = refs[n_pages:]
    j = pl.program_id(1)
    hd = LANES
    kvw = N_KV_HEADS * hd
    kt = n_pages * PAGE_SIZE
    blocks_per_step = kt // SEL_BLOCK
    qb = qb_ref[...]
    slope = slope_ref[...]
    lq = lq_ref[...]
    pos_q = n_prev + lq

    def scores(k_bf16, pos_k):
        dist = pos_q - pos_k
        return jnp.dot(k_bf16, qb, preferred_element_type=F32) - slope * dist, dist

    def online(s, mask, v_bf16):
        s = jnp.where(mask, s, NEG_INF)
        m_old = m_scr[...]
        m_new = jnp.maximum(m_old, jnp.max(s, axis=0, keepdims=True))
        alpha = jnp.exp(m_old - m_new)
        p = jnp.where(mask, jnp.exp(s - m_new), 0.0)
        l_scr[...] = alpha * l_scr[...] + jnp.sum(p, axis=0, keepdims=True)
        m_scr[...] = m_new
        pv = lax.dot_general(p.astype(BF16), v_bf16, TN_DIMS, preferred_element_type=F32)
        acc_scr[...] = _scale_rows(acc_scr[...], alpha) + pv

    @pl.when(j == 0)
    def _():
        n_c = kc_ref.shape[0]
        nbp = imp_scr.shape[0]
        pos_c = (lax.broadcasted_iota(jnp.int32, (n_c, 1), 0) * CMP_STRIDE + (CMP_BLOCK - 1)).astype(F32)
        s, dist = scores(kc_ref[...], pos_c)
        p = _masked_softmax(s, dist >= 0, 0)
        oc_scr[...] = lax.dot_general(p.astype(BF16), vc_ref[...], TN_DIMS, preferred_element_type=F32)
        p_r = jnp.dot(p, rsum_ref[...], precision=lax.Precision.HIGHEST, preferred_element_type=F32)
        imp = jnp.dot(mapt_ref[...], p_r, precision=lax.Precision.HIGHEST, preferred_element_type=F32)
        blk = lax.broadcasted_iota(jnp.int32, (nbp, 1), 0)
        posq_i = n_prev + lq.astype(jnp.int32)
        cur = posq_i // SEL_BLOCK
        forced = (blk == 0) | (blk == cur) | (blk == cur - 1)
        valid = blk * SEL_BLOCK <= posq_i
        imp = jnp.where(valid, jnp.where(forced, imp + FORCE_BONUS, imp), NEG_INF)
        imp_scr[...] = imp

        def rank_step(c, cnt):
            rowc = imp_scr[pl.ds(c, 1), :]
            beats = (rowc > imp) | ((rowc == imp) & (c < blk))
            return cnt + jnp.where(beats, 1.0, 0.0)

        cnt = lax.fori_loop(0, nb, rank_step, jnp.zeros((nbp, LANES), F32))
        sel_scr[...] = jnp.where(cnt < min(N_SEL, nb), 1.0, 0.0)
        m_scr[...] = jnp.full(m_scr.shape, NEG_INF, F32)
        l_scr[...] = jnp.zeros(l_scr.shape, F32)
        acc_scr[...] = jnp.zeros(acc_scr.shape, F32)

    heads = 2 * N_KV_HEADS

    def head_rows(h):
        return jnp.concatenate([pg[pl.ds(h, PAGE_SIZE, stride=heads), :] for pg in pages], axis=0)

    k_all = jnp.concatenate([head_rows(g) for g in range(N_KV_HEADS)], axis=1).astype(BF16)
    v_all = jnp.concatenate([head_rows(N_KV_HEADS + g) for g in range(N_KV_HEADS)], axis=1).astype(BF16)
    pos_k = (j * kt + lax.broadcasted_iota(jnp.int32, (kt, 1), 0)).astype(F32)
    s, dist = scores(k_all, pos_k)
    selrows = sel_scr[pl.ds(pl.multiple_of(j * blocks_per_step, blocks_per_step), blocks_per_step), :]
    selm = jnp.broadcast_to(selrows[:, None, :], (blocks_per_step, SEL_BLOCK, LANES)).reshape(kt, LANES)
    online(s, (dist >= 0) & (selm > 0.5), v_all)

    @pl.when(j == pl.num_programs(1) - 1)
    def _():
        n_new = selnew_ref.shape[0]
        pos_n = (n_prev + lax.broadcasted_iota(jnp.int32, (n_new, 1), 0)).astype(F32)
        s, dist = scores(selnew_ref[:, 0:kvw].astype(BF16), pos_n)
        selrow = sel_scr[n_prev // SEL_BLOCK:n_prev // SEL_BLOCK + 1, :]
        online(s, (dist >= 0) & (selrow > 0.5), selnew_ref[:, kvw:2 * kvw].astype(BF16))
        l_s = l_scr[...]
        o_s = _scale_rows(acc_scr[...], 1.0 / jnp.where(l_s > 0, l_s, 1.0))

        n_w = win_ref.shape[0]
        pos_w = (win_start + lax.broadcasted_iota(jnp.int32, (n_w, 1), 0)).astype(F32)
        s, dist = scores(win_ref[:, 0:kvw].astype(BF16), pos_w)
        p = _masked_softmax(s, (dist >= 0) & (dist < WINDOW), 0)
        o_w = lax.dot_general(p.astype(BF16), win_ref[:, kvw:2 * kvw].astype(BF16), TN_DIMS,
                              preferred_element_type=F32)
        o_c = oc_scr[...]

        rr = HEADS_PER_GROUP
        for g in range(N_KV_HEADS):
            for r in range(rr):
                h = g * rr + r
                rows = slice((g * rr + r) * q_len, (g * rr + r + 1) * q_len)
                cols = slice(g * hd, (g + 1) * hd)
                gc = gt_ref[:, h * N_BRANCH + 0:h * N_BRANCH + 1]
                gs = gt_ref[:, h * N_BRANCH + 1:h * N_BRANCH + 2]
                gw = gt_ref[:, h * N_BRANCH + 2:h * N_BRANCH + 3]
                o_ref[:, h * hd:(h + 1) * hd] = gc * o_c[rows, cols] + gs * o_s[rows, cols] + gw * o_w[rows, cols]


def nsa_sample(q_f32, gates_f32, kc, vc, sel_pages, page_table, sel_new, win_full, *, n_prev):
    b, q_len, qw = q_f32.shape
    hd = LANES
    rr = HEADS_PER_GROUP
    g_n = N_KV_HEADS
    assert g_n * rr * q_len == LANES and n_prev % SEL_BLOCK == 0 and q_len <= SEL_BLOCK
    npg = page_table.shape[1]
    n_p = min(SAMPLE_PAGES, npg)
    assert npg % n_p == 0 and (n_p * PAGE_SIZE // SEL_BLOCK) % SUBLANES == 0
    n_ch = kc.shape[1]
    nb = -(-(n_prev + q_len) // SEL_BLOCK)
    nbp = -(-nb // SUBLANES) * SUBLANES
    width = sel_new.shape[2]
    page_rows = sel_pages.shape[1]
    assert page_rows == PAGE_SIZE * 2 * g_n and sel_pages.shape[2] == hd

    qt = q_f32.reshape(b, q_len, g_n, rr, hd).transpose(0, 2, 4, 3, 1).reshape(b, g_n, hd, rr * q_len)
    qblk = jnp.einsum('bgdl,gh->bgdhl', qt, jnp.eye(g_n, dtype=F32)).reshape(b, g_n * hd, LANES).astype(BF16)

    lane = np.arange(LANES)
    lane_g, lane_r, lane_q = lane // (rr * q_len), (lane // q_len) % rr, lane % q_len
    slope = np.array([_alibi_slope(h) for h in lane_g * rr + lane_r], np.float32).reshape(1, LANES)
    lq = lane_q.astype(np.float32).reshape(1, LANES)
    rsum = ((lane_g[:, None] == lane_g[None, :]) & (lane_q[:, None] == lane_q[None, :])).astype(np.float32)
    mapt = np.zeros((nbp, n_ch), np.float32)
    mapt[:nb, :n_ch - 1] = _cmp_to_sel_map(n_ch - 1, nb).T

    pad16 = lambda a: jnp.pad(a, ((0, 0), (0, (-a.shape[1]) % BF16_ROWS), (0, 0)))
    sel_new_p = pad16(sel_new)
    win_p = pad16(win_full)

    def page_spec(k):
        return pl.BlockSpec((None, page_rows, hd), lambda bi, j, pt: (pt[bi * npg + j * n_p + k], 0, 0))

    const2 = lambda bi, j, pt: (0, 0)
    per_b = lambda bi, j, pt: (bi, 0, 0)
    grid_spec = pltpu.PrefetchScalarGridSpec(
        num_scalar_prefetch=1,
        grid=(b, npg // n_p),
        in_specs=[pl.BlockSpec((None, g_n * hd, LANES), per_b),
                  pl.BlockSpec((None, q_len, LANES), per_b),
                  pl.BlockSpec((None, n_ch, g_n * hd), per_b),
                  pl.BlockSpec((None, n_ch, g_n * hd), per_b),
                  pl.BlockSpec((nbp, n_ch), const2),
                  pl.BlockSpec((LANES, LANES), const2),
                  pl.BlockSpec((1, LANES), const2),
                  pl.BlockSpec((1, LANES), const2),
                  pl.BlockSpec((None, sel_new_p.shape[1], width), per_b),
                  pl.BlockSpec((None, win_p.shape[1], width), per_b)]
        + [page_spec(k) for k in range(n_p)],
        out_specs=pl.BlockSpec((None, q_len, qw), per_b),
        scratch_shapes=[pltpu.VMEM((nbp, LANES), F32), pltpu.VMEM((nbp, LANES), F32),
                        pltpu.VMEM((1, LANES), F32), pltpu.VMEM((1, LANES), F32),
                        pltpu.VMEM((LANES, g_n * hd), F32), pltpu.VMEM((LANES, g_n * hd), F32)],
    )
    kern = functools.partial(_nsa_sample_kernel, n_pages=n_p, n_prev=n_prev, nb=nb, q_len=q_len,
                             win_start=n_prev + q_len - win_full.shape[1])
    return pl.pallas_call(
        kern,
        out_shape=jax.ShapeDtypeStruct((b, q_len, qw), F32),
        grid_spec=grid_spec,
        compiler_params=_cparams(2, 48),
        name="nsa_sample",
    )(page_table.reshape(-1), qblk, gates_f32, kc, vc, jnp.asarray(mapt), jnp.asarray(rsum), jnp.asarray(slope),
      jnp.asarray(lq), sel_new_p, win_p, *([sel_pages] * n_p))


PROMPT_TILE = 512


def _trunk(x, params, *, n_prev, pool_prefix, conv_prefix, cmp_pages, cmp_table, sel_pages, sel_table, win_past,
           seq_tile, seqs_per_tile):
    p = params
    b, t, d = x.shape
    depth = p['ln_mix'].shape[0]
    n_a = p['w_pool'].shape[0]
    hd = LANES
    kvw = N_KV_HEADS * hd
    row_w = 2 * kvw
    is_prompt = cmp_pages is None
    no_halo = jnp.zeros((b, max(POOL_HALO, FFN_HALO), d), F32)
    pool_rows, conv_rows = [], []
    for l in range(depth):
        if l < n_a:
            x, rows = pool_layer(x, x if is_prompt else no_halo, pool_prefix[l], p['ln_mix'][l], p['w_pool'][l], p['pool_scale'][l],
                                 n_prev=n_prev, tm=seq_tile)
            pool_rows.append(rows)
        else:
            xm = x.reshape(b * t, d)
            if l == n_a:
                cmp_f, _ = norm_linear(xm, p['ln_kv'], p['w_kv'][:, 0:row_w], p['k_norm'][0], n_norm=0)
                sel_f, sel_b = norm_linear(xm, p['ln_kv'], p['w_kv'][:, row_w:2 * row_w], p['k_norm'][1],
                                           n_norm=N_KV_HEADS)
                win_f, win_b = norm_linear(xm, p['ln_kv'], p['w_kv'][:, 2 * row_w:3 * row_w], p['k_norm'][2],
                                           n_norm=N_KV_HEADS)
                if is_prompt:
                    cmp_pages = cmp_f.reshape(b * t // PAGE_SIZE, PAGE_SIZE * row_w // hd, hd)
                    cmp_table = jnp.arange(b * t // PAGE_SIZE, dtype=jnp.int32).reshape(b, t // PAGE_SIZE)
                    win_full = win_f.reshape(b, t, row_w)
                else:
                    win_full = jnp.concatenate([win_past, win_f.reshape(b, t, row_w)], axis=1)
                pab = compress_partial(cmp_pages, cmp_table, p['w1cat'])
                kc, vc = compress_finish(pab, p['cmp_pos'], p['w1flat'], p['cmp_w2'], p['k_norm'][0])
            j = l - n_a
            q_f, q_b = norm_linear(xm, p['ln_mix'][l], p['w_qg'][j], p['q_norm'][j], n_norm=N_HEADS,
                                   norm_scale=hd ** -0.5, sigmoid_from=N_HEADS)
            qw = N_HEADS * hd
            if is_prompt:
                o = nsa_prompt(q_b.reshape(b, t, -1), q_f.reshape(b, t, -1), kc, vc,
                               sel_b.reshape(b, t, row_w), win_b.reshape(b, t, row_w))
            else:
                q3 = q_f.reshape(b, t, -1)
                o = nsa_sample(q3[:, :, :qw], q3[:, :, qw:], kc, vc, sel_pages, sel_table,
                               sel_f.reshape(b, t, row_w), win_full, n_prev=n_prev).astype(BF16)
            x = linear_residual(o.reshape(b * t, qw), p['w_o'][j], xm).reshape(b, t, d)
        x, rows = ffn_layer(x, x if is_prompt else no_halo, conv_prefix[l], p['ln_ffn'][l], p['w_in'][l], p['conv_w'][l], p['conv_b'][l],
                            p['w_out'][l], nb=seqs_per_tile, tm=seq_tile)
        conv_rows.append(rows[:, FFN_HALO - (CONV_WIDTH - 1):])
    row_shape = (2, N_KV_HEADS, hd)
    n_win = min(WINDOW, n_prev + t)
    win_state = win_full[:, win_full.shape[1] - n_win:].reshape((b, n_win) + row_shape)
    return (x, cmp_f.reshape((b, t) + row_shape), sel_f.reshape((b, t) + row_shape), win_state,
            jnp.stack(pool_rows), jnp.stack(conv_rows))


def kernel(x_prompt, x_sample, cache_kv_cmp, cache_kv_sel, page_table, state_kv_win, state_pool, state_conv, ln_mix, ln_ffn, w_pool, pool_scale, ln_kv, w_kv, k_norm, cmp_w1, cmp_pos, cmp_w2, w_qg, q_norm, w_o, w_in, conv_w, conv_b, w_out):
    depth = ln_mix.shape[0]
    n_a = w_pool.shape[0]
    d = x_prompt.shape[2]
    ff2 = w_in.shape[2]
    hd = LANES
    row_w = 2 * N_KV_HEADS * hd
    qg_pad = (-w_qg.shape[2]) % LANES
    half = CMP_BLOCK // 2
    params = dict(
        ln_mix=ln_mix, ln_ffn=ln_ffn, pool_scale=pool_scale, ln_kv=ln_kv, k_norm=k_norm, q_norm=q_norm,
        conv_w=conv_w, conv_b=conv_b, cmp_pos=cmp_pos,
        w_pool=w_pool.astype(BF16), w_kv=w_kv.astype(BF16),
        w_qg=jnp.pad(w_qg, ((0, 0), (0, 0), (0, qg_pad))).astype(BF16),
        w_o=w_o.astype(BF16), w_in=w_in.astype(BF16), w_out=w_out.astype(BF16),
        cmp_w2=cmp_w2.astype(BF16),
        w1cat=jnp.concatenate([cmp_w1[:, :half], cmp_w1[:, half:]], axis=-1).astype(BF16),
        w1flat=cmp_w1.reshape(2, CMP_BLOCK * hd, cmp_w1.shape[3]).astype(BF16),
    )

    b_p, t_p, _ = x_prompt.shape
    prompt = _trunk(
        x_prompt, params, n_prev=0,
        pool_prefix=jnp.zeros((n_a, b_p, POOL_HALO, d), F32),
        conv_prefix=jnp.zeros((depth, b_p, FFN_HALO, ff2), F32),
        cmp_pages=None, cmp_table=None, sel_pages=None, sel_table=None, win_past=None,
        seq_tile=min(PROMPT_TILE, t_p), seqs_per_tile=1)

    b_s, t_s, _ = x_sample.shape
    n_prev = page_table.shape[1] * PAGE_SIZE
    assert t_s < CMP_STRIDE and cache_kv_cmp.shape[1] == PAGE_SIZE
    sample = _trunk(
        x_sample, params, n_prev=n_prev,
        pool_prefix=jnp.pad(state_pool, ((0, 0), (0, 0), (POOL_HALO - POOL_KEEP, 0), (0, 0))),
        conv_prefix=jnp.pad(state_conv, ((0, 0), (0, 0), (FFN_HALO - (CONV_WIDTH - 1), 0), (0, 0))),
        cmp_pages=cache_kv_cmp.reshape(cache_kv_cmp.shape[0], PAGE_SIZE * row_w // hd, hd), cmp_table=page_table,
        sel_pages=cache_kv_sel.reshape(cache_kv_sel.shape[0], PAGE_SIZE * row_w // hd, hd), sel_table=page_table,
        win_past=state_kv_win.reshape(b_s, state_kv_win.shape[1], row_w),
        seq_tile=t_s, seqs_per_tile=b_s)

    return tuple(leaf for pair in zip(prompt, sample) for leaf in pair)
```

```python
import functools

import numpy as np
import jax
import jax.numpy as jnp
from jax import lax
from jax.experimental import pallas as pl
from jax.experimental.pallas import tpu as pltpu

F32 = jnp.float32
BF16 = jnp.bfloat16

POOL_WINDOWS = (2, 4, 8, 16)
POOL_KEEP = max(POOL_WINDOWS) - 1
N_HEADS = 16
N_KV_HEADS = 4
HEADS_PER_GROUP = N_HEADS // N_KV_HEADS
N_BRANCH = 3
CMP_BLOCK = 32
CMP_STRIDE = 16
SEL_BLOCK = 64
N_SEL = 16
WINDOW = 512
FORCE_BONUS = 1e4
CONV_WIDTH = 3
Q_BLOCK = 128
PAGE_SIZE = 128
EPS = 1e-6
NEG_INF = -1e30

LANES = 128
SUBLANES = 8
BF16_ROWS = 16
MIB = 2 ** 20


def _cparams(n_axes, vmem_mib):
    return pltpu.CompilerParams(dimension_semantics=("arbitrary",) * n_axes,
                                vmem_limit_bytes=vmem_mib * MIB)


def _rms(x, g):
    return (x * lax.rsqrt(jnp.mean(x * x, axis=-1, keepdims=True) + EPS)) * g


def _alibi_slope(h):
    return float(np.float32(2.0 ** (-8.0 * (h + 1) / N_HEADS)))


def _norm_linear_kernel(x_ref, g_ref, w_ref, hg_ref, of_ref, ob_ref, *, n_norm, norm_scale, sigmoid_from):
    h = _rms(x_ref[...], g_ref[...]).astype(BF16)
    y = jnp.dot(h, w_ref[...], preferred_element_type=F32)
    for c in range(y.shape[1] // LANES):
        cols = slice(c * LANES, (c + 1) * LANES)
        yc = y[:, cols]
        if c < n_norm:
            yc = _rms(yc, hg_ref[...])
            if norm_scale is not None:
                yc = yc * norm_scale
        elif sigmoid_from is not None and c >= sigmoid_from:
            yc = jax.nn.sigmoid(yc)
        of_ref[:, cols] = yc
        ob_ref[:, cols] = yc.astype(BF16)


def norm_linear(x, gain, w_bf16, head_gain, *, n_norm=0, norm_scale=None, sigmoid_from=None, tm=512):
    m, k = x.shape
    n = w_bf16.shape[1]
    tm = min(tm, m)
    assert m % tm == 0 and n % LANES == 0
    kern = functools.partial(_norm_linear_kernel, n_norm=n_norm, norm_scale=norm_scale, sigmoid_from=sigmoid_from)
    return pl.pallas_call(
        kern,
        out_shape=(jax.ShapeDtypeStruct((m, n), F32), jax.ShapeDtypeStruct((m, n), BF16)),
        grid=(m // tm,),
        in_specs=[pl.BlockSpec((tm, k), lambda i: (i, 0)),
                  pl.BlockSpec((1, k), lambda i: (0, 0)),
                  pl.BlockSpec((k, n), lambda i: (0, 0)),
                  pl.BlockSpec((1, LANES), lambda i: (0, 0))],
        out_specs=(pl.BlockSpec((tm, n), lambda i: (i, 0)), pl.BlockSpec((tm, n), lambda i: (i, 0))),
        compiler_params=_cparams(1, 48),
        name="norm_linear",
    )(x, gain.reshape(1, k), w_bf16, head_gain.reshape(1, LANES))


def _linear_residual_kernel(a_ref, w_ref, r_ref, o_ref):
    o_ref[...] = r_ref[...] + jnp.dot(a_ref[...], w_ref[...], preferred_element_type=F32)


def linear_residual(a_bf16, w_bf16, res, *, tm=512):
    m, k = a_bf16.shape
    n = w_bf16.shape[1]
    tm = min(tm, m)
    assert m % tm == 0
    return pl.pallas_call(
        _linear_residual_kernel,
        out_shape=jax.ShapeDtypeStruct((m, n), F32),
        grid=(m // tm,),
        in_specs=[pl.BlockSpec((tm, k), lambda i: (i, 0)),
                  pl.BlockSpec((k, n), lambda i: (0, 0)),
                  pl.BlockSpec((tm, n), lambda i: (i, 0))],
        out_specs=pl.BlockSpec((tm, n), lambda i: (i, 0)),
        compiler_params=_cparams(1, 48),
        name="linear_residual",
    )(a_bf16, w_bf16, res)


POOL_HALO = 16


def _pool_kernel(x_ref, xh_ref, pre_ref, g_ref, w_ref, sc_ref, o_ref, st_ref, ctx_scr, *, tm, n_prev):
    i = pl.program_id(1)
    x = x_ref[...]
    h = _rms(x, g_ref[...])
    halo = _rms(xh_ref[...], g_ref[...])
    ctx_scr[0:POOL_HALO, :] = jnp.where(i == 0, pre_ref[...], halo)
    ctx_scr[POOL_HALO:POOL_HALO + tm, :] = h
    t = i * tm + lax.broadcasted_iota(jnp.int32, (tm, 1), 0)
    gw = x.shape[1] // len(POOL_WINDOWS)
    for gi, w in enumerate(POOL_WINDOWS):
        cols = slice(gi * gw, (gi + 1) * gw)
        hs = h[:, cols]
        acc = hs
        for k in range(1, w):
            acc = acc + ctx_scr[POOL_HALO - k:POOL_HALO - k + tm, cols]
        cnt = jnp.minimum(w, n_prev + t + 1).astype(F32)
        pooled = acc / cnt - hs
        mixed = jnp.dot(pooled.astype(BF16), w_ref[gi], preferred_element_type=F32)
        o_ref[:, cols] = x[:, cols] + mixed * sc_ref[:, cols]
    st_ref[...] = ctx_scr[tm + POOL_HALO - POOL_KEEP:tm + POOL_HALO, :]


def pool_layer(x, x_halo_src, prefix16, gain, w_pool_bf16, scale, *, n_prev, tm):
    b, t, d = x.shape
    assert t % tm == 0 and tm % SUBLANES == 0
    hb = tm // POOL_HALO
    kern = functools.partial(_pool_kernel, tm=tm, n_prev=n_prev)
    return pl.pallas_call(
        kern,
        out_shape=(jax.ShapeDtypeStruct((b, t, d), F32), jax.ShapeDtypeStruct((b, POOL_KEEP, d), F32)),
        grid=(b, t // tm),
        in_specs=[pl.BlockSpec((None, tm, d), lambda bi, i: (bi, i, 0)),
                  pl.BlockSpec((None, POOL_HALO, d), lambda bi, i: (bi, jnp.maximum(i * hb - 1, 0), 0)),
                  pl.BlockSpec((None, POOL_HALO, d), lambda bi, i: (bi, 0, 0)),
                  pl.BlockSpec((1, d), lambda bi, i: (0, 0)),
                  pl.BlockSpec(w_pool_bf16.shape, lambda bi, i: (0, 0, 0)),
                  pl.BlockSpec((1, d), lambda bi, i: (0, 0))],
        out_specs=(pl.BlockSpec((None, tm, d), lambda bi, i: (bi, i, 0)),
                   pl.BlockSpec((None, POOL_KEEP, d), lambda bi, i: (bi, 0, 0))),
        scratch_shapes=[pltpu.VMEM((tm + POOL_HALO, d), F32)],
        compiler_params=_cparams(2, 48),
        name="pool_layer",
    )(x, x_halo_src, prefix16, gain.reshape(1, d), w_pool_bf16, scale.reshape(1, d))


FFN_HALO = 16
FFN_SPLIT = 2


def _ffn_kernel(x_ref, xh_ref, pg_ref, pv_ref, g_ref, wig_ref, wiv_ref, cwg_ref, cwv_ref, cbg_ref, cbv_ref, wo_ref,
                o_ref, sg_ref, sv_ref, u_scr, hg_scr, hv_scr, acc_scr, *, nb, tm, te):
    i = pl.program_id(1)
    f = pl.program_id(2)
    n_ext = nb * te
    n_out = n_ext - FFN_HALO

    @pl.when(f == 0)
    def _():
        for s in range(nb):
            u_scr[s * te:s * te + FFN_HALO, :] = _rms(xh_ref[s], g_ref[...]).astype(BF16)
            u_scr[s * te + FFN_HALO:s * te + FFN_HALO + tm, :] = _rms(x_ref[s], g_ref[...]).astype(BF16)
            if te > FFN_HALO + tm:
                u_scr[s * te + FFN_HALO + tm:(s + 1) * te, :] = jnp.zeros((te - FFN_HALO - tm, u_scr.shape[1]), BF16)
        acc_scr[...] = jnp.zeros_like(acc_scr)

    u = u_scr[...]
    n_split = hg_scr.shape[0]
    pw = hg_scr.shape[2]
    first = i == 0

    def up_project(h_scr, w_ref, p_ref, piece):
        cols = slice(piece * pw, (piece + 1) * pw)
        h = jnp.dot(u, w_ref[:, cols], preferred_element_type=F32)
        h_scr[piece] = h
        for s in range(nb):
            halo = slice(s * te, s * te + FFN_HALO)
            h_scr[piece, halo, :] = jnp.where(first, p_ref[s, :, cols], h[halo])

    def conv(h_scr, cw_ref, cb_ref, piece):
        cols = slice(piece * pw, (piece + 1) * pw)
        c = cb_ref[:, cols] + cw_ref[0:1, cols] * h_scr[piece, FFN_HALO - 2:FFN_HALO - 2 + n_out, :]
        c = c + cw_ref[1:2, cols] * h_scr[piece, FFN_HALO - 1:FFN_HALO - 1 + n_out, :]
        return c + cw_ref[2:3, cols] * h_scr[piece, FFN_HALO:FFN_HALO + n_out, :]

    for piece in range(n_split):
        up_project(hg_scr, wig_ref, pg_ref, piece)
        up_project(hv_scr, wiv_ref, pv_ref, piece)
    down = None
    for piece in range(n_split):
        cg = conv(hg_scr, cwg_ref, cbg_ref, piece)
        cv = conv(hv_scr, cwv_ref, cbv_ref, piece)
        act = (cg * jax.nn.sigmoid(cg)) * cv
        d = jnp.dot(act.astype(BF16), wo_ref[piece * pw:(piece + 1) * pw, :], preferred_element_type=F32)
        down = d if down is None else down + d
    acc_scr[...] += down

    for s in range(nb):
        for piece in range(n_split):
            cols = slice(piece * pw, (piece + 1) * pw)
            sg_ref[s, :, cols] = hg_scr[piece, s * te + tm:s * te + tm + FFN_HALO, :]
            sv_ref[s, :, cols] = hv_scr[piece, s * te + tm:s * te + tm + FFN_HALO, :]

    @pl.when(f == pl.num_programs(2) - 1)
    def _():
        for s in range(nb):
            o_ref[s] = x_ref[s] + acc_scr[s * te:s * te + tm, :]


def ffn_layer(x, x_halo_src, prefix16, gain, w_in_bf16, conv_w, conv_b, w_out_bf16, *, nb, tm, tf=512):
    b, t, d = x.shape
    ff = w_out_bf16.shape[0]
    assert t % tm == 0 and b % nb == 0 and ff % tf == 0 and tm % SUBLANES == 0
    nf = ff // tf
    hb = tm // FFN_HALO
    te = FFN_HALO + -(-tm // BF16_ROWS) * BF16_ROWS
    n_ext = nb * te
    kern = functools.partial(_ffn_kernel, nb=nb, tm=tm, te=te)
    gate_col = lambda bi, i, f: (0, f)
    val_col = lambda bi, i, f: (0, nf + f)
    y, sg, sv = pl.pallas_call(
        kern,
        out_shape=(jax.ShapeDtypeStruct((b, t, d), F32),
                   jax.ShapeDtypeStruct((b, t // tm, FFN_HALO, ff), F32),
                   jax.ShapeDtypeStruct((b, t // tm, FFN_HALO, ff), F32)),
        grid=(b // nb, t // tm, nf),
        in_specs=[pl.BlockSpec((nb, tm, d), lambda bi, i, f: (bi, i, 0)),
                  pl.BlockSpec((nb, FFN_HALO, d), lambda bi, i, f: (bi, jnp.maximum(i * hb - 1, 0), 0)),
                  pl.BlockSpec((nb, FFN_HALO, tf), lambda bi, i, f: (bi, 0, f)),
                  pl.BlockSpec((nb, FFN_HALO, tf), lambda bi, i, f: (bi, 0, nf + f)),
                  pl.BlockSpec((1, d), lambda bi, i, f: (0, 0)),
                  pl.BlockSpec((d, tf), gate_col),
                  pl.BlockSpec((d, tf), val_col),
                  pl.BlockSpec((CONV_WIDTH, tf), gate_col),
                  pl.BlockSpec((CONV_WIDTH, tf), val_col),
                  pl.BlockSpec((1, tf), gate_col),
                  pl.BlockSpec((1, tf), val_col),
                  pl.BlockSpec((tf, d), lambda bi, i, f: (f, 0))],
        out_specs=(pl.BlockSpec((nb, tm, d), lambda bi, i, f: (bi, i, 0)),
                   pl.BlockSpec((nb, None, FFN_HALO, tf), lambda bi, i, f: (bi, i, 0, f)),
                   pl.BlockSpec((nb, None, FFN_HALO, tf), lambda bi, i, f: (bi, i, 0, f))),
        scratch_shapes=[pltpu.VMEM((n_ext, d), BF16),
                        pltpu.VMEM((FFN_SPLIT, n_ext, tf // FFN_SPLIT), F32),
                        pltpu.VMEM((FFN_SPLIT, n_ext, tf // FFN_SPLIT), F32),
                        pltpu.VMEM((n_ext - FFN_HALO, d), F32)],
        compiler_params=_cparams(3, 56),
        name="ffn_layer",
    )(x, x_halo_src, prefix16, prefix16, gain.reshape(1, d), w_in_bf16, w_in_bf16, conv_w, conv_w,
      conv_b.reshape(1, 2 * ff), conv_b.reshape(1, 2 * ff), w_out_bf16)
    return y, jnp.concatenate([sg[:, -1], sv[:, -1]], axis=-1)


CMP_PAGES = 16
CHUNKS_PER_PAGE = PAGE_SIZE // CMP_STRIDE


def _compress_kernel(pt_ref, *refs, n_pages):
    del pt_ref
    page_refs = refs[:n_pages]
    w_ref, o_ref = refs[n_pages], refs[n_pages + 1]
    rows = n_pages * CHUNKS_PER_PAGE
    heads = 2 * N_KV_HEADS
    for v in range(2):
        acc = None
        for s in range(CMP_STRIDE):
            pieces = []
            for g in range(N_KV_HEADS):
                r0 = s * heads + v * N_KV_HEADS + g
                for k in range(n_pages):
                    pieces.append(page_refs[k][pl.ds(r0, CHUNKS_PER_PAGE, stride=CMP_STRIDE * heads), :])
            lhs = jnp.concatenate(pieces, axis=0).astype(BF16)
            d = jnp.dot(lhs, w_ref[v, s], preferred_element_type=F32)
            acc = d if acc is None else acc + d
        for g in range(N_KV_HEADS):
            o_ref[v, g] = acc[g * rows:(g + 1) * rows, :]


def compress_partial(pages, page_table, w1cat_bf16):
    s, npg = page_table.shape
    page_rows, hd = pages.shape[1:]
    assert page_rows == PAGE_SIZE * 2 * N_KV_HEADS
    hid2 = w1cat_bf16.shape[3]
    n_p = min(CMP_PAGES, npg)
    assert npg % n_p == 0
    n_ch = npg * CHUNKS_PER_PAGE

    def page_spec(k):
        return pl.BlockSpec((None, page_rows, hd), lambda si, j, pt: (pt[si * npg + j * n_p + k], 0, 0))

    grid_spec = pltpu.PrefetchScalarGridSpec(
        num_scalar_prefetch=1,
        grid=(s, npg // n_p),
        in_specs=[page_spec(k) for k in range(n_p)]
        + [pl.BlockSpec(w1cat_bf16.shape, lambda si, j, pt: (0, 0, 0, 0))],
        out_specs=pl.BlockSpec((None, 2, N_KV_HEADS, n_p * CHUNKS_PER_PAGE, hid2), lambda si, j, pt: (si, 0, 0, j, 0)),
    )
    return pl.pallas_call(
        functools.partial(_compress_kernel, n_pages=n_p),
        out_shape=jax.ShapeDtypeStruct((s, 2, N_KV_HEADS, n_ch, hid2), F32),
        grid_spec=grid_spec,
        compiler_params=_cparams(2, 48),
        name="compress_partial",
    )(page_table.reshape(-1), *([pages] * n_p), w1cat_bf16)


def _compress_finish_kernel(pab_ref, pos_ref, w1_ref, w2_ref, kn_ref, kc_ref, vc_ref):
    n = pab_ref.shape[1]
    hid = w2_ref.shape[1]
    for v in range(2):
        blk = pab_ref[v]
        nxt = pltpu.roll(blk[:, hid:], n - 1, 0)
        pos = jnp.broadcast_to(pos_ref[v], (SUBLANES, pos_ref.shape[2])).astype(BF16)
        bias = jnp.dot(pos, w1_ref[v], preferred_element_type=F32)[0:1, :]
        hdn = jax.nn.gelu(blk[:, :hid] + nxt + bias)
        out = jnp.dot(hdn.astype(BF16), w2_ref[v], preferred_element_type=F32)
        if v == 0:
            kc_ref[...] = _rms(out, kn_ref[...]).astype(BF16)
        else:
            vc_ref[...] = out.astype(BF16)


def compress_finish(pab, cmp_pos, w1flat_bf16, w2_bf16, k_gain):
    s, _, g, n_ch, hid2 = pab.shape
    hd = w2_bf16.shape[2]
    kdim = w1flat_bf16.shape[1]
    out = jax.ShapeDtypeStruct((s, n_ch, g * hd), BF16)
    return pl.pallas_call(
        _compress_finish_kernel,
        out_shape=(out, out),
        grid=(s, g),
        in_specs=[pl.BlockSpec((None, 2, None, n_ch, hid2), lambda si, gi: (si, 0, gi, 0, 0)),
                  pl.BlockSpec((2, 1, kdim), lambda si, gi: (0, 0, 0)),
                  pl.BlockSpec(w1flat_bf16.shape, lambda si, gi: (0, 0, 0)),
                  pl.BlockSpec(w2_bf16.shape, lambda si, gi: (0, 0, 0)),
                  pl.BlockSpec((1, hd), lambda si, gi: (0, 0))],
        out_specs=(pl.BlockSpec((None, n_ch, hd), lambda si, gi: (si, 0, gi)),
                   pl.BlockSpec((None, n_ch, hd), lambda si, gi: (si, 0, gi))),
        compiler_params=_cparams(2, 48),
        name="compress_finish",
    )(pab, cmp_pos.reshape(2, 1, kdim), w1flat_bf16, w2_bf16, k_gain.reshape(1, hd))


def _cmp_to_sel_map(nc, nb):
    c_s = np.arange(nc) * CMP_STRIDE
    c_e = c_s + CMP_BLOCK - 1
    s_s = np.arange(nb) * SEL_BLOCK
    s_e = s_s + SEL_BLOCK - 1
    return ((c_s[:, None] <= s_e[None]) & (c_e[:, None] >= s_s[None])).astype(np.float32)


def _masked_softmax(s, mask, axis):
    s = jnp.where(mask, s, NEG_INF)
    m = jnp.max(s, axis=axis, keepdims=True)
    p = jnp.where(mask, jnp.exp(s - m), 0.0)
    den = jnp.sum(p, axis=axis, keepdims=True)
    return p * (1.0 / jnp.where(den > 0, den, 1.0))


SEL_KEY_TILE = 512


POS_SPLIT = 3
BLOCK_LANES = 64
MASK_BIAS = -1e30


def _position_columns(pos, onehot):
    pos = np.asarray(pos)
    assert pos.max() // SEL_BLOCK <= 256
    out = np.zeros((pos.shape[0], LANES), np.float32)
    if onehot:
        assert pos.max() // SEL_BLOCK < BLOCK_LANES
        out[np.arange(pos.shape[0]), pos // SEL_BLOCK] = 1.0
    for j in range(POS_SPLIT):
        out[:, BLOCK_LANES + j] = (pos // SEL_BLOCK) * SEL_BLOCK
        out[:, BLOCK_LANES + POS_SPLIT + j] = pos % SEL_BLOCK
    return out


def _slope_columns():
    out = np.zeros((N_HEADS, LANES), np.float32)
    for h in range(N_HEADS):
        rem = np.float32(_alibi_slope(h))
        for j in range(POS_SPLIT):
            part = np.float32(rem.astype(jnp.bfloat16))
            out[h, BLOCK_LANES + j] = part
            out[h, BLOCK_LANES + POS_SPLIT + j] = part
            rem = np.float32(rem - part)
        assert rem == 0
    return out


def _nsa_prompt_kernel(q_ref, gt_ref, kc_ref, vc_ref, cpos_ref, selk_ref, selv_ref, wink_ref, winv_ref,
                       spos_ref, wpos_ref, mapt_ref, slope_ref, o_ref, used_ref, *, wlen):
    i = pl.program_id(1)
    s0 = i * Q_BLOCK
    hd = LANES
    rr, qb, tk = HEADS_PER_GROUP, Q_BLOCK, SEL_KEY_TILE
    m_rows = rr * qb
    n_cp = kc_ref.shape[0]
    nb = mapt_ref.shape[0]
    nt_dims = (((1,), (1,)), ((), ()))
    kt_last = s0 // tk
    ws = pl.multiple_of(jnp.maximum(s0 - WINDOW, 0), Q_BLOCK)

    row = lax.broadcasted_iota(jnp.int32, (m_rows, 1), 0)
    pos_q = s0 + (row & (qb - 1))
    posq_l = s0 + lax.broadcasted_iota(jnp.int32, (1, qb), 1)
    blk = lax.broadcasted_iota(jnp.int32, (nb, 1), 0)
    cur = posq_l // SEL_BLOCK
    forced = (blk == 0) | (blk == cur) | (blk == cur - 1)
    valid = blk * SEL_BLOCK <= posq_l
    cmask = lax.broadcasted_iota(jnp.int32, (1, n_cp), 1) * CMP_STRIDE + (CMP_BLOCK - 1) <= pos_q
    dist_w = pos_q - (ws + lax.broadcasted_iota(jnp.int32, (1, wlen), 1))
    wmask = (dist_w >= 0) & (dist_w < WINDOW)

    for g in range(N_KV_HEADS):
        cols = slice(g * hd, (g + 1) * hd)
        q = jnp.concatenate([q_ref[:, (g * rr + r) * hd:(g * rr + r + 1) * hd] for r in range(rr)], axis=0)
        slope_cols = slope_ref[g]
        q_pos = jnp.concatenate([q, slope_cols.astype(BF16)], axis=1)

        kc_aug = jnp.concatenate([kc_ref[:, cols], cpos_ref[...]], axis=1)
        s = lax.dot_general(q_pos, kc_aug, nt_dims, preferred_element_type=F32)
        p = _masked_softmax(s, cmask, -1)
        o_c = jnp.dot(p.astype(BF16), vc_ref[:, cols], preferred_element_type=F32)

        p_sum = p[0:qb]
        for r in range(1, rr):
            p_sum = p_sum + p[r * qb:(r + 1) * qb]
        imp = lax.dot_general(mapt_ref[...], p_sum, nt_dims, precision=lax.Precision.HIGHEST,
                              preferred_element_type=F32)
        imp = jnp.where(valid, jnp.where(forced, imp + FORCE_BONUS, imp), NEG_INF)
        cnt = jnp.zeros((nb, qb), F32)
        for c in range(nb):
            rowc = imp[c:c + 1, :]
            beats = (rowc > imp) | ((rowc == imp) & (blk > c))
            cnt = cnt + jnp.where(beats, 1.0, 0.0)
        chosen = cnt < min(N_SEL, nb)
        bias_t = jnp.where(chosen, 0.0, MASK_BIAS)
        chosen_f = jnp.where(chosen, 1.0, 0.0)
        for t in range(nb // (tk // SEL_BLOCK)):
            used_ref[t] = jnp.max(chosen_f[t * (tk // SEL_BLOCK):(t + 1) * (tk // SEL_BLOCK), :]).astype(jnp.int32)
        bias = jnp.transpose(jnp.concatenate([bias_t, jnp.zeros((hd - nb, qb), F32)], axis=0))
        q_sel = jnp.concatenate([q, (slope_cols + jnp.concatenate([bias] * rr, axis=0)).astype(BF16)], axis=1)

        def sel_tile(kt, carry, causal):
            m_i, l_i, acc = carry
            k0 = pl.multiple_of(kt * tk, tk)
            k_aug = jnp.concatenate([selk_ref[pl.ds(k0, tk), cols], spos_ref[pl.ds(k0, tk), :]], axis=1)
            s = lax.dot_general(q_sel, k_aug, nt_dims, preferred_element_type=F32)
            if causal:
                s = jnp.where(k0 + lax.broadcasted_iota(jnp.int32, (1, tk), 1) <= pos_q, s, NEG_INF)
            m_new = jnp.maximum(m_i, jnp.max(s, axis=-1, keepdims=True))
            alpha = jnp.exp(m_i - m_new)
            p = jnp.exp(s - m_new)
            l_new = alpha * l_i + jnp.sum(p, axis=-1, keepdims=True)
            acc = alpha * acc + jnp.dot(p.astype(BF16), selv_ref[pl.ds(k0, tk), cols], preferred_element_type=F32)
            return m_new, l_new, acc

        init = (jnp.full((m_rows, 1), NEG_INF, F32), jnp.zeros((m_rows, 1), F32), jnp.zeros((m_rows, hd), F32))
        def sel_step(kt, carry):
            return lax.cond(used_ref[kt] > 0, lambda c: sel_tile(kt, c, False), lambda c: c, carry)

        carry = lax.fori_loop(0, kt_last, sel_step, init)
        _, l_s, acc_s = sel_tile(kt_last, carry, True)
        o_s = acc_s * (1.0 / jnp.where(l_s > 0, l_s, 1.0))

        kw_aug = jnp.concatenate([wink_ref[pl.ds(ws, wlen), cols], wpos_ref[pl.ds(ws, wlen), :]], axis=1)
        s = lax.dot_general(q_pos, kw_aug, nt_dims, preferred_element_type=F32)
        p = _masked_softmax(s, wmask, -1)
        o_w = jnp.dot(p.astype(BF16), winv_ref[pl.ds(ws, wlen), cols], preferred_element_type=F32)

        def gate(branch):
            return jnp.concatenate(
                [gt_ref[:, (g * rr + r) * N_BRANCH + branch:(g * rr + r) * N_BRANCH + branch + 1] for r in range(rr)],
                axis=0)

        o = gate(0) * o_c + gate(1) * o_s + gate(2) * o_w
        for r in range(rr):
            o_ref[:, (g * rr + r) * hd:(g * rr + r + 1) * hd] = o[r * qb:(r + 1) * qb].astype(BF16)


def nsa_prompt(q_bf16, gates_f32, kc, vc, sel_bf16, win_bf16):
    b, t, _ = q_bf16.shape
    qw = N_HEADS * LANES
    n_cp = kc.shape[1]
    nb = t // SEL_BLOCK
    assert t % SEL_KEY_TILE == 0 and t % Q_BLOCK == 0
    wlen = min(WINDOW + Q_BLOCK, t)
    kvw = N_KV_HEADS * LANES
    mapt = np.zeros((nb, n_cp), np.float32)
    mapt[:, :n_cp - 1] = _cmp_to_sel_map(n_cp - 1, nb).T
    cpos = _position_columns(np.arange(n_cp) * CMP_STRIDE + (CMP_BLOCK - 1), onehot=False)
    spos = _position_columns(np.arange(t), onehot=True)
    wpos = _position_columns(np.arange(t), onehot=False)
    slope = np.repeat(_slope_columns().reshape(N_KV_HEADS, HEADS_PER_GROUP, 1, LANES), Q_BLOCK, axis=2)
    slope = slope.reshape(N_KV_HEADS, HEADS_PER_GROUP * Q_BLOCK, LANES)
    kern = functools.partial(_nsa_prompt_kernel, wlen=wlen)
    per_b = lambda bi, i: (bi, 0, 0)
    per_b_v = lambda bi, i: (bi, 0, 1)
    const2 = lambda bi, i: (0, 0)
    return pl.pallas_call(
        kern,
        out_shape=jax.ShapeDtypeStruct((b, t, qw), BF16),
        grid=(b, t // Q_BLOCK),
        in_specs=[pl.BlockSpec((None, Q_BLOCK, qw), lambda bi, i: (bi, i, 0)),
                  pl.BlockSpec((None, Q_BLOCK, LANES), lambda bi, i: (bi, i, qw // LANES)),
                  pl.BlockSpec((None, n_cp, kvw), per_b),
                  pl.BlockSpec((None, n_cp, kvw), per_b),
                  pl.BlockSpec((n_cp, LANES), const2),
                  pl.BlockSpec((None, t, kvw), per_b),
                  pl.BlockSpec((None, t, kvw), per_b_v),
                  pl.BlockSpec((None, t, kvw), per_b),
                  pl.BlockSpec((None, t, kvw), per_b_v),
                  pl.BlockSpec((t, LANES), const2),
                  pl.BlockSpec((t, LANES), const2),
                  pl.BlockSpec((nb, n_cp), const2),
                  pl.BlockSpec(slope.shape, lambda bi, i: (0, 0, 0))],
        out_specs=pl.BlockSpec((None, Q_BLOCK, qw), lambda bi, i: (bi, i, 0)),
        scratch_shapes=[pltpu.SMEM((t // SEL_KEY_TILE,), jnp.int32)],
        compiler_params=_cparams(2, 56),
        name="nsa_prompt",
    )(q_bf16, gates_f32, kc, vc, jnp.asarray(cpos, dtype=BF16), sel_bf16, sel_bf16, win_bf16, win_bf16,
      jnp.asarray(spos, dtype=BF16), jnp.asarray(wpos, dtype=BF16), jnp.asarray(mapt), jnp.asarray(slope))


SAMPLE_PAGES = 8
TAKEN = -3e38
TN_DIMS = (((0,), (0,)), ((), ()))


def _lane_to_rows(v):
    return jnp.transpose(jnp.broadcast_to(v, (LANES, LANES)))


def _scale_rows(a, v):
    t = _lane_to_rows(v)
    return a * jnp.concatenate([t] * (a.shape[1] // LANES), axis=1)


def _nsa_sample_kernel(pt_ref, qb_ref, gt_ref, kc_ref, vc_ref, mapt_ref, rsum_ref, slope_ref, lq_ref,
                       selnew_ref, win_ref, *refs, n_pages, n_prev, nb, q_len, win_start):
    del pt_ref
    pages = refs[:n_pages]
    o_ref, sel_scr, m_scr, l_scr, acc_scr, oc_scr = refs[n_pages:]
    j = pl.program_id(1)
    hd = LANES
    kvw = N_KV_HEADS * hd
    kt = n_pages * PAGE_SIZE
    blocks_per_step = kt // SEL_BLOCK
    qb = qb_ref[...]
    slope = slope_ref[...]
    lq = lq_ref[...]
    pos_q = n_prev + lq

    def scores(k_bf16, pos_k):
        dist = pos_q - pos_k
        return jnp.dot(k_bf16, qb, preferred_element_type=F32) - slope * dist, dist

    def online(s, mask, v_bf16):
        s = jnp.where(mask, s, NEG_INF)
        m_old = m_scr[...]
        m_new = jnp.maximum(m_old, jnp.max(s, axis=0, keepdims=True))
        alpha = jnp.exp(m_old - m_new)
        p = jnp.where(mask, jnp.exp(s - m_new), 0.0)
        l_scr[...] = alpha * l_scr[...] + jnp.sum(p, axis=0, keepdims=True)
        m_scr[...] = m_new
        pv = lax.dot_general(p.astype(BF16), v_bf16, TN_DIMS, preferred_element_type=F32)
        acc_scr[...] = _scale_rows(acc_scr[...], alpha) + pv

    @pl.when(j == 0)
    def _():
        n_c = kc_ref.shape[0]
        nbp = sel_scr.shape[0]
        pos_c = (lax.broadcasted_iota(jnp.int32, (n_c, 1), 0) * CMP_STRIDE + (CMP_BLOCK - 1)).astype(F32)
        s, dist = scores(kc_ref[...], pos_c)
        p = _masked_softmax(s, dist >= 0, 0)
        oc_scr[...] = lax.dot_general(p.astype(BF16), vc_ref[...], TN_DIMS, preferred_element_type=F32)
        p_r = jnp.dot(p, rsum_ref[...], precision=lax.Precision.HIGHEST, preferred_element_type=F32)
        imp = jnp.dot(mapt_ref[...], p_r, precision=lax.Precision.HIGHEST, preferred_element_type=F32)
        blk = lax.broadcasted_iota(jnp.int32, (nbp, 1), 0)
        posq_i = n_prev + lq.astype(jnp.int32)
        cur = posq_i // SEL_BLOCK
        forced = (blk == 0) | (blk == cur) | (blk == cur - 1)
        valid = blk * SEL_BLOCK <= posq_i
        imp = jnp.where(valid, jnp.where(forced, imp + FORCE_BONUS, imp), NEG_INF)
        blk_f = blk.astype(F32)

        def pick(_, carry):
            work, chosen = carry
            top = jnp.max(work, axis=0, keepdims=True)
            first = jnp.min(jnp.where(work == top, blk_f, float(nbp)), axis=0, keepdims=True)
            hit = blk_f == first
            return jnp.where(hit, TAKEN, work), jnp.where(hit, 1.0, chosen)

        _, chosen = lax.fori_loop(0, min(N_SEL, nb), pick, (imp, jnp.zeros((nbp, LANES), F32)))
        sel_scr[...] = chosen
        m_scr[...] = jnp.full(m_scr.shape, NEG_INF, F32)
        l_scr[...] = jnp.zeros(l_scr.shape, F32)
        acc_scr[...] = jnp.zeros(acc_scr.shape, F32)

    heads = 2 * N_KV_HEADS

    def head_rows(h):
        return jnp.concatenate([pg[pl.ds(h, PAGE_SIZE, stride=heads), :] for pg in pages], axis=0)

    k_all = jnp.concatenate([head_rows(g) for g in range(N_KV_HEADS)], axis=1).astype(BF16)
    v_all = jnp.concatenate([head_rows(N_KV_HEADS + g) for g in range(N_KV_HEADS)], axis=1).astype(BF16)
    pos_k = (j * kt + lax.broadcasted_iota(jnp.int32, (kt, 1), 0)).astype(F32)
    s, dist = scores(k_all, pos_k)
    selrows = sel_scr[pl.ds(pl.multiple_of(j * blocks_per_step, blocks_per_step), blocks_per_step), :]
    selm = jnp.broadcast_to(selrows[:, None, :], (blocks_per_step, SEL_BLOCK, LANES)).reshape(kt, LANES)
    online(s, (dist >= 0) & (selm > 0.5), v_all)

    @pl.when(j == pl.num_programs(1) - 1)
    def _():
        n_new = selnew_ref.shape[0]
        pos_n = (n_prev + lax.broadcasted_iota(jnp.int32, (n_new, 1), 0)).astype(F32)
        s, dist = scores(selnew_ref[:, 0:kvw].astype(BF16), pos_n)
        selrow = sel_scr[n_prev // SEL_BLOCK:n_prev // SEL_BLOCK + 1, :]
        online(s, (dist >= 0) & (selrow > 0.5), selnew_ref[:, kvw:2 * kvw].astype(BF16))
        l_s = l_scr[...]
        o_s = _scale_rows(acc_scr[...], 1.0 / jnp.where(l_s > 0, l_s, 1.0))

        n_w = win_ref.shape[0]
        pos_w = (win_start + lax.broadcasted_iota(jnp.int32, (n_w, 1), 0)).astype(F32)
        s, dist = scores(win_ref[:, 0:kvw].astype(BF16), pos_w)
        p = _masked_softmax(s, (dist >= 0) & (dist < WINDOW), 0)
        o_w = lax.dot_general(p.astype(BF16), win_ref[:, kvw:2 * kvw].astype(BF16), TN_DIMS,
                              preferred_element_type=F32)
        o_c = oc_scr[...]

        rr = HEADS_PER_GROUP
        for g in range(N_KV_HEADS):
            for r in range(rr):
                h = g * rr + r
                rows = slice((g * rr + r) * q_len, (g * rr + r + 1) * q_len)
                cols = slice(g * hd, (g + 1) * hd)
                gc = gt_ref[:, h * N_BRANCH + 0:h * N_BRANCH + 1]
                gs = gt_ref[:, h * N_BRANCH + 1:h * N_BRANCH + 2]
                gw = gt_ref[:, h * N_BRANCH + 2:h * N_BRANCH + 3]
                o_ref[:, h * hd:(h + 1) * hd] = gc * o_c[rows, cols] + gs * o_s[rows, cols] + gw * o_w[rows, cols]


def nsa_sample(q_f32, gates_f32, kc, vc, sel_pages, page_table, sel_new, win_full, *, n_prev):
    b, q_len, qw = q_f32.shape
    hd = LANES
    rr = HEADS_PER_GROUP
    g_n = N_KV_HEADS
    assert g_n * rr * q_len == LANES and n_prev % SEL_BLOCK == 0 and q_len <= SEL_BLOCK
    npg = page_table.shape[1]
    n_p = min(SAMPLE_PAGES, npg)
    assert npg % n_p == 0 and (n_p * PAGE_SIZE // SEL_BLOCK) % SUBLANES == 0
    n_ch = kc.shape[1]
    nb = -(-(n_prev + q_len) // SEL_BLOCK)
    nbp = -(-nb // SUBLANES) * SUBLANES
    width = sel_new.shape[2]
    page_rows = sel_pages.shape[1]
    assert page_rows == PAGE_SIZE * 2 * g_n and sel_pages.shape[2] == hd

    qt = q_f32.reshape(b, q_len, g_n, rr, hd).transpose(0, 2, 4, 3, 1).reshape(b, g_n, hd, rr * q_len)
    qblk = jnp.einsum('bgdl,gh->bgdhl', qt, jnp.eye(g_n, dtype=F32)).reshape(b, g_n * hd, LANES).astype(BF16)

    lane = np.arange(LANES)
    lane_g, lane_r, lane_q = lane // (rr * q_len), (lane // q_len) % rr, lane % q_len
    slope = np.array([_alibi_slope(h) for h in lane_g * rr + lane_r], np.float32).reshape(1, LANES)
    lq = lane_q.astype(np.float32).reshape(1, LANES)
    rsum = ((lane_g[:, None] == lane_g[None, :]) & (lane_q[:, None] == lane_q[None, :])).astype(np.float32)
    mapt = np.zeros((nbp, n_ch), np.float32)
    mapt[:nb, :n_ch - 1] = _cmp_to_sel_map(n_ch - 1, nb).T

    pad16 = lambda a: jnp.pad(a, ((0, 0), (0, (-a.shape[1]) % BF16_ROWS), (0, 0)))
    sel_new_p = pad16(sel_new)
    win_p = pad16(win_full)

    def page_spec(k):
        return pl.BlockSpec((None, page_rows, hd), lambda bi, j, pt: (pt[bi * npg + j * n_p + k], 0, 0))

    const2 = lambda bi, j, pt: (0, 0)
    per_b = lambda bi, j, pt: (bi, 0, 0)
    grid_spec = pltpu.PrefetchScalarGridSpec(
        num_scalar_prefetch=1,
        grid=(b, npg // n_p),
        in_specs=[pl.BlockSpec((None, g_n * hd, LANES), per_b),
                  pl.BlockSpec((None, q_len, LANES), per_b),
                  pl.BlockSpec((None, n_ch, g_n * hd), per_b),
                  pl.BlockSpec((None, n_ch, g_n * hd), per_b),
                  pl.BlockSpec((nbp, n_ch), const2),
                  pl.BlockSpec((LANES, LANES), const2),
                  pl.BlockSpec((1, LANES), const2),
                  pl.BlockSpec((1, LANES), const2),
                  pl.BlockSpec((None, sel_new_p.shape[1], width), per_b),
                  pl.BlockSpec((None, win_p.shape[1], width), per_b)]
        + [page_spec(k) for k in range(n_p)],
        out_specs=pl.BlockSpec((None, q_len, qw), per_b),
        scratch_shapes=[pltpu.VMEM((nbp, LANES), F32),
                        pltpu.VMEM((1, LANES), F32), pltpu.VMEM((1, LANES), F32),
                        pltpu.VMEM((LANES, g_n * hd), F32), pltpu.VMEM((LANES, g_n * hd), F32)],
    )
    kern = functools.partial(_nsa_sample_kernel, n_pages=n_p, n_prev=n_prev, nb=nb, q_len=q_len,
                             win_start=n_prev + q_len - win_full.shape[1])
    return pl.pallas_call(
        kern,
        out_shape=jax.ShapeDtypeStruct((b, q_len, qw), F32),
        grid_spec=grid_spec,
        compiler_params=_cparams(2, 48),
        name="nsa_sample",
    )(page_table.reshape(-1), qblk, gates_f32, kc, vc, jnp.asarray(mapt), jnp.asarray(rsum), jnp.asarray(slope),
      jnp.asarray(lq), sel_new_p, win_p, *([sel_pages] * n_p))


PROMPT_TILE = 512


def _trunk(x, params, *, n_prev, pool_prefix, conv_prefix, cmp_pages, cmp_table, sel_pages, sel_table, win_past,
           seq_tile, seqs_per_tile):
    p = params
    b, t, d = x.shape
    depth = p['ln_mix'].shape[0]
    n_a = p['w_pool'].shape[0]
    hd = LANES
    kvw = N_KV_HEADS * hd
    row_w = 2 * kvw
    is_prompt = cmp_pages is None
    no_halo = jnp.zeros((b, max(POOL_HALO, FFN_HALO), d), F32)
    pool_rows, conv_rows = [], []
    for l in range(depth):
        if l < n_a:
            x, rows = pool_layer(x, x if is_prompt else no_halo, pool_prefix[l], p['ln_mix'][l], p['w_pool'][l], p['pool_scale'][l],
                                 n_prev=n_prev, tm=seq_tile)
            pool_rows.append(rows)
        else:
            xm = x.reshape(b * t, d)
            if l == n_a:
                cmp_f, _ = norm_linear(xm, p['ln_kv'], p['w_kv'][:, 0:row_w], p['k_norm'][0], n_norm=0)
                sel_f, sel_b = norm_linear(xm, p['ln_kv'], p['w_kv'][:, row_w:2 * row_w], p['k_norm'][1],
                                           n_norm=N_KV_HEADS)
                win_f, win_b = norm_linear(xm, p['ln_kv'], p['w_kv'][:, 2 * row_w:3 * row_w], p['k_norm'][2],
                                           n_norm=N_KV_HEADS)
                if is_prompt:
                    cmp_pages = cmp_f.reshape(b * t // PAGE_SIZE, PAGE_SIZE * row_w // hd, hd)
                    cmp_table = jnp.arange(b * t // PAGE_SIZE, dtype=jnp.int32).reshape(b, t // PAGE_SIZE)
                    win_full = win_f.reshape(b, t, row_w)
                else:
                    win_full = jnp.concatenate([win_past, win_f.reshape(b, t, row_w)], axis=1)
                pab = compress_partial(cmp_pages, cmp_table, p['w1cat'])
                kc, vc = compress_finish(pab, p['cmp_pos'], p['w1flat'], p['cmp_w2'], p['k_norm'][0])
            j = l - n_a
            q_f, q_b = norm_linear(xm, p['ln_mix'][l], p['w_qg'][j], p['q_norm'][j], n_norm=N_HEADS,
                                   norm_scale=hd ** -0.5, sigmoid_from=N_HEADS)
            qw = N_HEADS * hd
            if is_prompt:
                o = nsa_prompt(q_b.reshape(b, t, -1), q_f.reshape(b, t, -1), kc, vc,
                               sel_b.reshape(b, t, row_w), win_b.reshape(b, t, row_w))
            else:
                q3 = q_f.reshape(b, t, -1)
                o = nsa_sample(q3[:, :, :qw], q3[:, :, qw:], kc, vc, sel_pages, sel_table,
                               sel_f.reshape(b, t, row_w), win_full, n_prev=n_prev).astype(BF16)
            x = linear_residual(o.reshape(b * t, qw), p['w_o'][j], xm).reshape(b, t, d)
        x, rows = ffn_layer(x, x if is_prompt else no_halo, conv_prefix[l], p['ln_ffn'][l], p['w_in'][l], p['conv_w'][l], p['conv_b'][l],
                            p['w_out'][l], nb=seqs_per_tile, tm=seq_tile)
        conv_rows.append(rows[:, FFN_HALO - (CONV_WIDTH - 1):])
    row_shape = (2, N_KV_HEADS, hd)
    n_win = min(WINDOW, n_prev + t)
    win_state = win_full[:, win_full.shape[1] - n_win:].reshape((b, n_win) + row_shape)
    return (x, cmp_f.reshape((b, t) + row_shape), sel_f.reshape((b, t) + row_shape), win_state,
            jnp.stack(pool_rows), jnp.stack(conv_rows))


def kernel(x_prompt, x_sample, cache_kv_cmp, cache_kv_sel, page_table, state_kv_win, state_pool, state_conv, ln_mix, ln_ffn, w_pool, pool_scale, ln_kv, w_kv, k_norm, cmp_w1, cmp_pos, cmp_w2, w_qg, q_norm, w_o, w_in, conv_w, conv_b, w_out):
    depth = ln_mix.shape[0]
    n_a = w_pool.shape[0]
    d = x_prompt.shape[2]
    ff2 = w_in.shape[2]
    hd = LANES
    row_w = 2 * N_KV_HEADS * hd
    qg_pad = (-w_qg.shape[2]) % LANES
    half = CMP_BLOCK // 2
    params = dict(
        ln_mix=ln_mix, ln_ffn=ln_ffn, pool_scale=pool_scale, ln_kv=ln_kv, k_norm=k_norm, q_norm=q_norm,
        conv_w=conv_w, conv_b=conv_b, cmp_pos=cmp_pos,
        w_pool=w_pool.astype(BF16), w_kv=w_kv.astype(BF16),
        w_qg=jnp.pad(w_qg, ((0, 0), (0, 0), (0, qg_pad))).astype(BF16),
        w_o=w_o.astype(BF16), w_in=w_in.astype(BF16), w_out=w_out.astype(BF16),
        cmp_w2=cmp_w2.astype(BF16),
        w1cat=jnp.concatenate([cmp_w1[:, :half], cmp_w1[:, half:]], axis=-1).astype(BF16),
        w1flat=cmp_w1.reshape(2, CMP_BLOCK * hd, cmp_w1.shape[3]).astype(BF16),
    )

    b_p, t_p, _ = x_prompt.shape
    prompt = _trunk(
        x_prompt, params, n_prev=0,
        pool_prefix=jnp.zeros((n_a, b_p, POOL_HALO, d), F32),
        conv_prefix=jnp.zeros((depth, b_p, FFN_HALO, ff2), F32),
        cmp_pages=None, cmp_table=None, sel_pages=None, sel_table=None, win_past=None,
        seq_tile=min(PROMPT_TILE, t_p), seqs_per_tile=1)

    b_s, t_s, _ = x_sample.shape
    n_prev = page_table.shape[1] * PAGE_SIZE
    assert t_s < CMP_STRIDE and cache_kv_cmp.shape[1] == PAGE_SIZE
    sample = _trunk(
        x_sample, params, n_prev=n_prev,
        pool_prefix=jnp.pad(state_pool, ((0, 0), (0, 0), (POOL_HALO - POOL_KEEP, 0), (0, 0))),
        conv_prefix=jnp.pad(state_conv, ((0, 0), (0, 0), (FFN_HALO - (CONV_WIDTH - 1), 0), (0, 0))),
        cmp_pages=cache_kv_cmp.reshape(cache_kv_cmp.shape[0], PAGE_SIZE * row_w // hd, hd), cmp_table=page_table,
        sel_pages=cache_kv_sel.reshape(cache_kv_sel.shape[0], PAGE_SIZE * row_w // hd, hd), sel_table=page_table,
        win_past=state_kv_win.reshape(b_s, state_kv_win.shape[1], row_w),
        seq_tile=t_s, seqs_per_tile=b_s)

    return tuple(leaf for pair in zip(prompt, sample) for leaf in pair)
```

```python
import functools

import numpy as np
import jax
import jax.numpy as jnp
from jax import lax
from jax.experimental import pallas as pl
from jax.experimental.pallas import tpu as pltpu

F32 = jnp.float32
BF16 = jnp.bfloat16

POOL_WINDOWS = (2, 4, 8, 16)
POOL_KEEP = max(POOL_WINDOWS) - 1
N_HEADS = 16
N_KV_HEADS = 4
HEADS_PER_GROUP = N_HEADS // N_KV_HEADS
N_BRANCH = 3
CMP_BLOCK = 32
CMP_STRIDE = 16
SEL_BLOCK = 64
N_SEL = 16
WINDOW = 512
FORCE_BONUS = 1e4
CONV_WIDTH = 3
Q_BLOCK = 128
PAGE_SIZE = 128
EPS = 1e-6
NEG_INF = -1e30

LANES = 128
SUBLANES = 8
BF16_ROWS = 16
MIB = 2 ** 20


def _cparams(n_axes, vmem_mib):
    return pltpu.CompilerParams(dimension_semantics=("arbitrary",) * n_axes,
                                vmem_limit_bytes=vmem_mib * MIB)


def _rms(x, g):
    return (x * lax.rsqrt(jnp.mean(x * x, axis=-1, keepdims=True) + EPS)) * g


def _alibi_slope(h):
    return float(np.float32(2.0 ** (-8.0 * (h + 1) / N_HEADS)))


def _norm_linear_kernel(x_ref, g_ref, w_ref, hg_ref, of_ref, ob_ref, *, chunk_gain, norm_scale, sigmoid_from):
    h = _rms(x_ref[...], g_ref[...]).astype(BF16)
    y = jnp.dot(h, w_ref[...], preferred_element_type=F32)
    for c in range(y.shape[1] // LANES):
        cols = slice(c * LANES, (c + 1) * LANES)
        yc = y[:, cols]
        if chunk_gain[c] is not None:
            yc = _rms(yc, hg_ref[chunk_gain[c]:chunk_gain[c] + 1, :])
            if norm_scale is not None:
                yc = yc * norm_scale
        elif sigmoid_from is not None and c >= sigmoid_from:
            yc = jax.nn.sigmoid(yc)
        of_ref[:, cols] = yc
        ob_ref[:, cols] = yc.astype(BF16)


def norm_linear(x, gain, w_bf16, head_gains, chunk_gain, *, norm_scale=None, sigmoid_from=None, tm=512):
    m, k = x.shape
    n = w_bf16.shape[1]
    tm = min(tm, m)
    assert m % tm == 0 and n % LANES == 0 and len(chunk_gain) == n // LANES
    kern = functools.partial(_norm_linear_kernel, chunk_gain=tuple(chunk_gain), norm_scale=norm_scale,
                             sigmoid_from=sigmoid_from)
    return pl.pallas_call(
        kern,
        out_shape=(jax.ShapeDtypeStruct((m, n), F32), jax.ShapeDtypeStruct((m, n), BF16)),
        grid=(m // tm,),
        in_specs=[pl.BlockSpec((tm, k), lambda i: (i, 0)),
                  pl.BlockSpec((1, k), lambda i: (0, 0)),
                  pl.BlockSpec((k, n), lambda i: (0, 0)),
                  pl.BlockSpec(head_gains.shape, lambda i: (0, 0))],
        out_specs=(pl.BlockSpec((tm, n), lambda i: (i, 0)), pl.BlockSpec((tm, n), lambda i: (i, 0))),
        compiler_params=_cparams(1, 56),
        name="norm_linear",
    )(x, gain.reshape(1, k), w_bf16, head_gains)


def _linear_residual_kernel(a_ref, w_ref, r_ref, o_ref):
    o_ref[...] = r_ref[...] + jnp.dot(a_ref[...], w_ref[...], preferred_element_type=F32)


def linear_residual(a_bf16, w_bf16, res, *, tm=512):
    m, k = a_bf16.shape
    n = w_bf16.shape[1]
    tm = min(tm, m)
    assert m % tm == 0
    return pl.pallas_call(
        _linear_residual_kernel,
        out_shape=jax.ShapeDtypeStruct((m, n), F32),
        grid=(m // tm,),
        in_specs=[pl.BlockSpec((tm, k), lambda i: (i, 0)),
                  pl.BlockSpec((k, n), lambda i: (0, 0)),
                  pl.BlockSpec((tm, n), lambda i: (i, 0))],
        out_specs=pl.BlockSpec((tm, n), lambda i: (i, 0)),
        compiler_params=_cparams(1, 48),
        name="linear_residual",
    )(a_bf16, w_bf16, res)


POOL_HALO = 16


def _pool_kernel(x_ref, xh_ref, pre_ref, g_ref, w_ref, sc_ref, o_ref, st_ref, ctx_scr, *, tm, n_prev):
    i = pl.program_id(1)
    x = x_ref[...]
    h = _rms(x, g_ref[...])
    halo = _rms(xh_ref[...], g_ref[...])
    ctx_scr[0:POOL_HALO, :] = jnp.where(i == 0, pre_ref[...], halo)
    ctx_scr[POOL_HALO:POOL_HALO + tm, :] = h
    t = i * tm + lax.broadcasted_iota(jnp.int32, (tm, 1), 0)
    gw = x.shape[1] // len(POOL_WINDOWS)
    for gi, w in enumerate(POOL_WINDOWS):
        cols = slice(gi * gw, (gi + 1) * gw)
        hs = h[:, cols]
        acc = hs
        for k in range(1, w):
            acc = acc + ctx_scr[POOL_HALO - k:POOL_HALO - k + tm, cols]
        cnt = jnp.minimum(w, n_prev + t + 1).astype(F32)
        pooled = acc / cnt - hs
        mixed = jnp.dot(pooled.astype(BF16), w_ref[gi], preferred_element_type=F32)
        o_ref[:, cols] = x[:, cols] + mixed * sc_ref[:, cols]
    st_ref[...] = ctx_scr[tm + POOL_HALO - POOL_KEEP:tm + POOL_HALO, :]


def pool_layer(x, x_halo_src, prefix16, gain, w_pool_bf16, scale, *, n_prev, tm):
    b, t, d = x.shape
    assert t % tm == 0 and tm % SUBLANES == 0
    hb = tm // POOL_HALO
    kern = functools.partial(_pool_kernel, tm=tm, n_prev=n_prev)
    return pl.pallas_call(
        kern,
        out_shape=(jax.ShapeDtypeStruct((b, t, d), F32), jax.ShapeDtypeStruct((b, POOL_KEEP, d), F32)),
        grid=(b, t // tm),
        in_specs=[pl.BlockSpec((None, tm, d), lambda bi, i: (bi, i, 0)),
                  pl.BlockSpec((None, POOL_HALO, d), lambda bi, i: (bi, jnp.maximum(i * hb - 1, 0), 0)),
                  pl.BlockSpec((None, POOL_HALO, d), lambda bi, i: (bi, 0, 0)),
                  pl.BlockSpec((1, d), lambda bi, i: (0, 0)),
                  pl.BlockSpec(w_pool_bf16.shape, lambda bi, i: (0, 0, 0)),
                  pl.BlockSpec((1, d), lambda bi, i: (0, 0))],
        out_specs=(pl.BlockSpec((None, tm, d), lambda bi, i: (bi, i, 0)),
                   pl.BlockSpec((None, POOL_KEEP, d), lambda bi, i: (bi, 0, 0))),
        scratch_shapes=[pltpu.VMEM((tm + POOL_HALO, d), F32)],
        compiler_params=_cparams(2, 48),
        name="pool_layer",
    )(x, x_halo_src, prefix16, gain.reshape(1, d), w_pool_bf16, scale.reshape(1, d))


FFN_HALO = 16
FFN_SPLIT = 2


def _ffn_kernel(x_ref, xh_ref, pg_ref, pv_ref, g_ref, wig_ref, wiv_ref, cwg_ref, cwv_ref, cbg_ref, cbv_ref, wo_ref,
                o_ref, sg_ref, sv_ref, u_scr, acc_scr, *h_scrs, nb, tm, te):
    hg_scr, hv_scr = h_scrs[:FFN_SPLIT], h_scrs[FFN_SPLIT:]
    i = pl.program_id(1)
    f = pl.program_id(2)
    n_ext = nb * te
    n_out = n_ext - FFN_HALO

    @pl.when(f == 0)
    def _():
        for s in range(nb):
            u_scr[s * te:s * te + FFN_HALO, :] = _rms(xh_ref[s], g_ref[...]).astype(BF16)
            u_scr[s * te + FFN_HALO:s * te + FFN_HALO + tm, :] = _rms(x_ref[s], g_ref[...]).astype(BF16)
            if te > FFN_HALO + tm:
                u_scr[s * te + FFN_HALO + tm:(s + 1) * te, :] = jnp.zeros((te - FFN_HALO - tm, u_scr.shape[1]), BF16)
        acc_scr[...] = jnp.zeros_like(acc_scr)

    u = u_scr[...]
    n_split = FFN_SPLIT
    pw = hg_scr[0].shape[1]
    first = i == 0

    def up_project(h_scr, w_ref, p_ref, piece):
        cols = slice(piece * pw, (piece + 1) * pw)
        h = jnp.dot(u, w_ref[:, cols], preferred_element_type=F32)
        h_scr[...] = h
        for s in range(nb):
            halo = slice(s * te, s * te + FFN_HALO)
            h_scr[halo, :] = jnp.where(first, p_ref[s, :, cols], h[halo])

    def conv(h_scr, cw_ref, cb_ref, piece):
        cols = slice(piece * pw, (piece + 1) * pw)
        c = cb_ref[:, cols] + cw_ref[0:1, cols] * h_scr[FFN_HALO - 2:FFN_HALO - 2 + n_out, :]
        c = c + cw_ref[1:2, cols] * h_scr[FFN_HALO - 1:FFN_HALO - 1 + n_out, :]
        return c + cw_ref[2:3, cols] * h_scr[FFN_HALO:FFN_HALO + n_out, :]

    down = None
    for piece in range(n_split):
        up_project(hg_scr[piece], wig_ref, pg_ref, piece)
        up_project(hv_scr[piece], wiv_ref, pv_ref, piece)
        cg = conv(hg_scr[piece], cwg_ref, cbg_ref, piece)
        cv = conv(hv_scr[piece], cwv_ref, cbv_ref, piece)
        act = (cg * jax.nn.sigmoid(cg)) * cv
        d = jnp.dot(act.astype(BF16), wo_ref[piece * pw:(piece + 1) * pw, :], preferred_element_type=F32)
        down = d if down is None else down + d
    acc_scr[...] += down

    for s in range(nb):
        for piece in range(n_split):
            cols = slice(piece * pw, (piece + 1) * pw)
            sg_ref[s, :, cols] = hg_scr[piece][s * te + tm:s * te + tm + FFN_HALO, :]
            sv_ref[s, :, cols] = hv_scr[piece][s * te + tm:s * te + tm + FFN_HALO, :]

    @pl.when(f == pl.num_programs(2) - 1)
    def _():
        for s in range(nb):
            o_ref[s] = x_ref[s] + acc_scr[s * te:s * te + tm, :]


def ffn_layer(x, x_halo_src, prefix16, gain, w_in_bf16, conv_w, conv_b, w_out_bf16, *, layer, nb, tm, tf=512):
    b, t, d = x.shape
    ff = w_out_bf16.shape[1]
    assert t % tm == 0 and b % nb == 0 and ff % tf == 0 and tm % SUBLANES == 0
    nf = ff // tf
    hb = tm // FFN_HALO
    te = FFN_HALO + -(-tm // BF16_ROWS) * BF16_ROWS
    n_ext = nb * te
    kern = functools.partial(_ffn_kernel, nb=nb, tm=tm, te=te)
    gate_col = lambda bi, i, f: (0, f)
    val_col = lambda bi, i, f: (0, nf + f)
    y, sg, sv = pl.pallas_call(
        kern,
        out_shape=(jax.ShapeDtypeStruct((b, t, d), F32),
                   jax.ShapeDtypeStruct((b, t // tm, FFN_HALO, ff), F32),
                   jax.ShapeDtypeStruct((b, t // tm, FFN_HALO, ff), F32)),
        grid=(b // nb, t // tm, nf),
        in_specs=[pl.BlockSpec((nb, tm, d), lambda bi, i, f: (bi, i, 0)),
                  pl.BlockSpec((nb, FFN_HALO, d), lambda bi, i, f: (bi, jnp.maximum(i * hb - 1, 0), 0)),
                  pl.BlockSpec((nb, FFN_HALO, tf), lambda bi, i, f: (bi, 0, f)),
                  pl.BlockSpec((nb, FFN_HALO, tf), lambda bi, i, f: (bi, 0, nf + f)),
                  pl.BlockSpec((1, d), lambda bi, i, f: (0, 0)),
                  pl.BlockSpec((None, d, tf), lambda bi, i, f: (layer, 0, f)),
                  pl.BlockSpec((None, d, tf), lambda bi, i, f: (layer, 0, nf + f)),
                  pl.BlockSpec((CONV_WIDTH, tf), gate_col),
                  pl.BlockSpec((CONV_WIDTH, tf), val_col),
                  pl.BlockSpec((1, tf), gate_col),
                  pl.BlockSpec((1, tf), val_col),
                  pl.BlockSpec((None, tf, d), lambda bi, i, f: (layer, f, 0))],
        out_specs=(pl.BlockSpec((nb, tm, d), lambda bi, i, f: (bi, i, 0)),
                   pl.BlockSpec((nb, None, FFN_HALO, tf), lambda bi, i, f: (bi, i, 0, f)),
                   pl.BlockSpec((nb, None, FFN_HALO, tf), lambda bi, i, f: (bi, i, 0, f))),
        scratch_shapes=[pltpu.VMEM((n_ext, d), BF16), pltpu.VMEM((n_ext - FFN_HALO, d), F32)]
        + [pltpu.VMEM((n_ext, tf // FFN_SPLIT), F32)] * (2 * FFN_SPLIT),
        compiler_params=_cparams(3, 56),
        name="ffn_layer",
    )(x, x_halo_src, prefix16, prefix16, gain.reshape(1, d), w_in_bf16, w_in_bf16, conv_w, conv_w,
      conv_b.reshape(1, 2 * ff), conv_b.reshape(1, 2 * ff), w_out_bf16)
    return y, jnp.concatenate([sg[:, -1], sv[:, -1]], axis=-1)


CMP_PAGES = 16
CHUNKS_PER_PAGE = PAGE_SIZE // CMP_STRIDE


def _compress_kernel(pt_ref, *refs, n_pages):
    del pt_ref
    page_refs = refs[:n_pages]
    w_ref, o_ref = refs[n_pages], refs[n_pages + 1]
    heads = 2 * N_KV_HEADS
    k_rows = lax.broadcasted_iota(jnp.int32, (heads, LANES), 0) < N_KV_HEADS
    acc = [None, None]
    for s in range(CMP_STRIDE):
        lhs = [[], []]
        for k in range(n_pages):
            for n in range(0, CHUNKS_PER_PAGE, 2):
                t_e = page_refs[k][(CMP_STRIDE * n + s) * heads:(CMP_STRIDE * n + s + 1) * heads, :]
                t_o = page_refs[k][(CMP_STRIDE * (n + 1) + s) * heads:(CMP_STRIDE * (n + 1) + s + 1) * heads, :]
                lhs[0].append(jnp.where(k_rows, t_e, pltpu.roll(t_o, N_KV_HEADS, 0)))
                lhs[1].append(jnp.where(k_rows, pltpu.roll(t_e, N_KV_HEADS, 0), t_o))
        for v in range(2):
            a = jnp.concatenate(lhs[v], axis=0).astype(BF16)
            d = jnp.dot(a, w_ref[v, s], preferred_element_type=F32)
            acc[v] = d if acc[v] is None else acc[v] + d
    for v in range(2):
        o_ref[v] = acc[v]


def compress_partial(pages, page_table, w1cat_bf16):
    s, npg = page_table.shape
    page_rows, hd = pages.shape[1:]
    assert page_rows == PAGE_SIZE * 2 * N_KV_HEADS
    hid2 = w1cat_bf16.shape[3]
    n_p = min(CMP_PAGES, npg)
    assert npg % n_p == 0
    n_ch = npg * CHUNKS_PER_PAGE

    def page_spec(k):
        return pl.BlockSpec((None, page_rows, hd), lambda si, j, pt: (pt[si * npg + j * n_p + k], 0, 0))

    grid_spec = pltpu.PrefetchScalarGridSpec(
        num_scalar_prefetch=1,
        grid=(s, npg // n_p),
        in_specs=[page_spec(k) for k in range(n_p)]
        + [pl.BlockSpec(w1cat_bf16.shape, lambda si, j, pt: (0, 0, 0, 0))],
        out_specs=pl.BlockSpec((None, 2, n_p * CHUNKS_PER_PAGE * N_KV_HEADS, hid2), lambda si, j, pt: (si, 0, j, 0)),
    )
    return pl.pallas_call(
        functools.partial(_compress_kernel, n_pages=n_p),
        out_shape=jax.ShapeDtypeStruct((s, 2, n_ch * N_KV_HEADS, hid2), F32),
        grid_spec=grid_spec,
        compiler_params=_cparams(2, 48),
        name="compress_partial",
    )(page_table.reshape(-1), *([pages] * n_p), w1cat_bf16)


def _compress_finish_kernel(pab_ref, pos_ref, w1_ref, w2_ref, kn_ref, kc_ref, vc_ref):
    n = pab_ref.shape[1]
    hid = w2_ref.shape[1]
    for v in range(2):
        blk = pab_ref[v]
        nxt = pltpu.roll(blk[:, hid:], n - N_KV_HEADS, 0)
        pos = jnp.broadcast_to(pos_ref[v], (SUBLANES, pos_ref.shape[2])).astype(BF16)
        bias = jnp.dot(pos, w1_ref[v], preferred_element_type=F32)[0:1, :]
        hdn = jax.nn.gelu(blk[:, :hid] + nxt + bias)
        out = jnp.dot(hdn.astype(BF16), w2_ref[v], preferred_element_type=F32)
        if v == 0:
            kc_ref[...] = _rms(out, kn_ref[...]).astype(BF16)
        else:
            vc_ref[...] = out.astype(BF16)


def compress_finish(pab, cmp_pos, w1flat_bf16, w2_bf16, k_gain):
    s, _, rows, hid2 = pab.shape
    hd = w2_bf16.shape[2]
    kdim = w1flat_bf16.shape[1]
    out = jax.ShapeDtypeStruct((s, rows, hd), BF16)
    kc, vc = pl.pallas_call(
        _compress_finish_kernel,
        out_shape=(out, out),
        grid=(s,),
        in_specs=[pl.BlockSpec((None, 2, rows, hid2), lambda si: (si, 0, 0, 0)),
                  pl.BlockSpec((2, 1, kdim), lambda si: (0, 0, 0)),
                  pl.BlockSpec(w1flat_bf16.shape, lambda si: (0, 0, 0)),
                  pl.BlockSpec(w2_bf16.shape, lambda si: (0, 0, 0)),
                  pl.BlockSpec((1, hd), lambda si: (0, 0))],
        out_specs=(pl.BlockSpec((None, rows, hd), lambda si: (si, 0, 0)),
                   pl.BlockSpec((None, rows, hd), lambda si: (si, 0, 0))),
        compiler_params=_cparams(1, 48),
        name="compress_finish",
    )(pab, cmp_pos.reshape(2, 1, kdim), w1flat_bf16, w2_bf16, k_gain.reshape(1, hd))
    return (kc.reshape(s, rows // N_KV_HEADS, N_KV_HEADS * hd), vc.reshape(s, rows // N_KV_HEADS, N_KV_HEADS * hd))


def _cmp_to_sel_map(nc, nb):
    c_s = np.arange(nc) * CMP_STRIDE
    c_e = c_s + CMP_BLOCK - 1
    s_s = np.arange(nb) * SEL_BLOCK
    s_e = s_s + SEL_BLOCK - 1
    return ((c_s[:, None] <= s_e[None]) & (c_e[:, None] >= s_s[None])).astype(np.float32)


def _masked_softmax(s, mask, axis):
    s = jnp.where(mask, s, NEG_INF)
    m = jnp.max(s, axis=axis, keepdims=True)
    p = jnp.where(mask, jnp.exp(s - m), 0.0)
    den = jnp.sum(p, axis=axis, keepdims=True)
    return p * (1.0 / jnp.where(den > 0, den, 1.0))


SEL_KEY_TILE = 512


POS_SPLIT = 3
BLOCK_LANES = 64
MASK_BIAS = -1e30


def _position_columns(pos, onehot):
    pos = np.asarray(pos)
    assert pos.max() // SEL_BLOCK <= 256
    out = np.zeros((pos.shape[0], LANES), np.float32)
    if onehot:
        assert pos.max() // SEL_BLOCK < BLOCK_LANES
        out[np.arange(pos.shape[0]), pos // SEL_BLOCK] = 1.0
    for j in range(POS_SPLIT):
        out[:, BLOCK_LANES + j] = (pos // SEL_BLOCK) * SEL_BLOCK
        out[:, BLOCK_LANES + POS_SPLIT + j] = pos % SEL_BLOCK
    return out


def _slope_columns():
    out = np.zeros((N_HEADS, LANES), np.float32)
    for h in range(N_HEADS):
        rem = np.float32(_alibi_slope(h))
        for j in range(POS_SPLIT):
            part = np.float32(rem.astype(jnp.bfloat16))
            out[h, BLOCK_LANES + j] = part
            out[h, BLOCK_LANES + POS_SPLIT + j] = part
            rem = np.float32(rem - part)
        assert rem == 0
    return out


def _nsa_prompt_kernel(q_ref, gt_ref, kc_ref, vc_ref, cpos_ref, selk_ref, selv_ref, wink_ref, winv_ref,
                       spos_ref, wpos_ref, mapt_ref, slope_ref, o_ref, used_ref, *, wlen):
    i = pl.program_id(1)
    s0 = i * Q_BLOCK
    hd = LANES
    rr, qb, tk = HEADS_PER_GROUP, Q_BLOCK, SEL_KEY_TILE
    m_rows = rr * qb
    n_cp = kc_ref.shape[0]
    nb = mapt_ref.shape[0]
    nt_dims = (((1,), (1,)), ((), ()))
    kt_last = s0 // tk
    ws = pl.multiple_of(jnp.maximum(s0 - WINDOW, 0), Q_BLOCK)

    row = lax.broadcasted_iota(jnp.int32, (m_rows, 1), 0)
    pos_q = s0 + (row & (qb - 1))
    posq_l = s0 + lax.broadcasted_iota(jnp.int32, (1, qb), 1)
    blk = lax.broadcasted_iota(jnp.int32, (nb, 1), 0)
    cur = posq_l // SEL_BLOCK
    forced = (blk == 0) | (blk == cur) | (blk == cur - 1)
    valid = blk * SEL_BLOCK <= posq_l
    cmask = lax.broadcasted_iota(jnp.int32, (1, n_cp), 1) * CMP_STRIDE + (CMP_BLOCK - 1) <= pos_q
    dist_w = pos_q - (ws + lax.broadcasted_iota(jnp.int32, (1, wlen), 1))
    wmask = (dist_w >= 0) & (dist_w < WINDOW)

    for g in range(N_KV_HEADS):
        cols = slice(g * hd, (g + 1) * hd)
        q = jnp.concatenate([q_ref[:, (g * rr + r) * hd:(g * rr + r + 1) * hd] for r in range(rr)], axis=0)
        slope_cols = slope_ref[g]
        q_pos = jnp.concatenate([q, slope_cols.astype(BF16)], axis=1)

        kc_aug = jnp.concatenate([kc_ref[:, cols], cpos_ref[...]], axis=1)
        s = lax.dot_general(q_pos, kc_aug, nt_dims, preferred_element_type=F32)
        p = _masked_softmax(s, cmask, -1)
        o_c = jnp.dot(p.astype(BF16), vc_ref[:, cols], preferred_element_type=F32)

        p_sum = p[0:qb]
        for r in range(1, rr):
            p_sum = p_sum + p[r * qb:(r + 1) * qb]
        imp = lax.dot_general(mapt_ref[...], p_sum, nt_dims, precision=lax.Precision.HIGHEST,
                              preferred_element_type=F32)
        imp = jnp.where(valid, jnp.where(forced, imp + FORCE_BONUS, imp), NEG_INF)
        cnt = jnp.zeros((nb, qb), F32)
        for c in range(nb):
            rowc = imp[c:c + 1, :]
            beats = (rowc > imp) | ((rowc == imp) & (blk > c))
            cnt = cnt + jnp.where(beats, 1.0, 0.0)
        chosen = cnt < min(N_SEL, nb)
        bias_t = jnp.where(chosen, 0.0, MASK_BIAS)
        chosen_f = jnp.where(chosen, 1.0, 0.0)
        for t in range(nb // (tk // SEL_BLOCK)):
            used_ref[t] = jnp.max(chosen_f[t * (tk // SEL_BLOCK):(t + 1) * (tk // SEL_BLOCK), :]).astype(jnp.int32)
        bias = jnp.transpose(jnp.concatenate([bias_t, jnp.zeros((hd - nb, qb), F32)], axis=0))
        q_sel = jnp.concatenate([q, (slope_cols + jnp.concatenate([bias] * rr, axis=0)).astype(BF16)], axis=1)

        def sel_tile(kt, carry, causal):
            m_i, l_i, acc = carry
            k0 = pl.multiple_of(kt * tk, tk)
            k_aug = jnp.concatenate([selk_ref[pl.ds(k0, tk), cols], spos_ref[pl.ds(k0, tk), :]], axis=1)
            s = lax.dot_general(q_sel, k_aug, nt_dims, preferred_element_type=F32)
            if causal:
                s = jnp.where(k0 + lax.broadcasted_iota(jnp.int32, (1, tk), 1) <= pos_q, s, NEG_INF)
            m_new = jnp.maximum(m_i, jnp.max(s, axis=-1, keepdims=True))
            alpha = jnp.exp(m_i - m_new)
            p = jnp.exp(s - m_new)
            l_new = alpha * l_i + jnp.sum(p, axis=-1, keepdims=True)
            acc = alpha * acc + jnp.dot(p.astype(BF16), selv_ref[pl.ds(k0, tk), cols], preferred_element_type=F32)
            return m_new, l_new, acc

        init = (jnp.full((m_rows, 1), NEG_INF, F32), jnp.zeros((m_rows, 1), F32), jnp.zeros((m_rows, hd), F32))

        def sel_step(kt, carry):
            return lax.cond(used_ref[kt] > 0, lambda c: sel_tile(kt, c, False), lambda c: c, carry)

        carry = lax.fori_loop(0, kt_last, sel_step, init)
        _, l_s, acc_s = sel_tile(kt_last, carry, True)
        o_s = acc_s * (1.0 / jnp.where(l_s > 0, l_s, 1.0))

        kw_aug = jnp.concatenate([wink_ref[pl.ds(ws, wlen), cols], wpos_ref[pl.ds(ws, wlen), :]], axis=1)
        s = lax.dot_general(q_pos, kw_aug, nt_dims, preferred_element_type=F32)
        p = _masked_softmax(s, wmask, -1)
        o_w = jnp.dot(p.astype(BF16), winv_ref[pl.ds(ws, wlen), cols], preferred_element_type=F32)

        def gate(branch):
            return jnp.concatenate(
                [gt_ref[:, (g * rr + r) * N_BRANCH + branch:(g * rr + r) * N_BRANCH + branch + 1] for r in range(rr)],
                axis=0)

        o = gate(0) * o_c + gate(1) * o_s + gate(2) * o_w
        for r in range(rr):
            o_ref[:, (g * rr + r) * hd:(g * rr + r + 1) * hd] = o[r * qb:(r + 1) * qb].astype(BF16)


def nsa_prompt(q_bf16, gates_f32, kc, vc, kv_bf16):
    b, t, _ = q_bf16.shape
    qw = N_HEADS * LANES
    n_cp = kc.shape[1]
    nb = t // SEL_BLOCK
    assert t % SEL_KEY_TILE == 0 and t % Q_BLOCK == 0
    wlen = min(WINDOW + Q_BLOCK, t)
    kvw = N_KV_HEADS * LANES
    mapt = np.zeros((nb, n_cp), np.float32)
    mapt[:, :n_cp - 1] = _cmp_to_sel_map(n_cp - 1, nb).T
    cpos = _position_columns(np.arange(n_cp) * CMP_STRIDE + (CMP_BLOCK - 1), onehot=False)
    spos = _position_columns(np.arange(t), onehot=True)
    wpos = _position_columns(np.arange(t), onehot=False)
    slope = np.repeat(_slope_columns().reshape(N_KV_HEADS, HEADS_PER_GROUP, 1, LANES), Q_BLOCK, axis=2)
    slope = slope.reshape(N_KV_HEADS, HEADS_PER_GROUP * Q_BLOCK, LANES)
    kern = functools.partial(_nsa_prompt_kernel, wlen=wlen)
    per_b = lambda bi, i: (bi, 0, 0)
    const2 = lambda bi, i: (0, 0)

    def kv_cols(branch, v):
        return pl.BlockSpec((None, t, kvw), lambda bi, i: (bi, 0, 2 * branch + v))

    return pl.pallas_call(
        kern,
        out_shape=jax.ShapeDtypeStruct((b, t, qw), BF16),
        grid=(b, t // Q_BLOCK),
        in_specs=[pl.BlockSpec((None, Q_BLOCK, qw), lambda bi, i: (bi, i, 0)),
                  pl.BlockSpec((None, Q_BLOCK, LANES), lambda bi, i: (bi, i, qw // LANES)),
                  pl.BlockSpec((None, n_cp, kvw), per_b),
                  pl.BlockSpec((None, n_cp, kvw), per_b),
                  pl.BlockSpec((n_cp, LANES), const2),
                  kv_cols(1, 0), kv_cols(1, 1), kv_cols(2, 0), kv_cols(2, 1),
                  pl.BlockSpec((t, LANES), const2),
                  pl.BlockSpec((t, LANES), const2),
                  pl.BlockSpec((nb, n_cp), const2),
                  pl.BlockSpec(slope.shape, lambda bi, i: (0, 0, 0))],
        out_specs=pl.BlockSpec((None, Q_BLOCK, qw), lambda bi, i: (bi, i, 0)),
        scratch_shapes=[pltpu.SMEM((t // SEL_KEY_TILE,), jnp.int32)],
        compiler_params=_cparams(2, 56),
        name="nsa_prompt",
    )(q_bf16, gates_f32, kc, vc, jnp.asarray(cpos, dtype=BF16), kv_bf16, kv_bf16, kv_bf16, kv_bf16,
      jnp.asarray(spos, dtype=BF16), jnp.asarray(wpos, dtype=BF16), jnp.asarray(mapt), jnp.asarray(slope))


SAMPLE_PAGES = 8
TAKEN = -3e38
TN_DIMS = (((0,), (0,)), ((), ()))


def _lane_to_rows(v):
    return jnp.transpose(jnp.broadcast_to(v, (LANES, LANES)))


def _scale_rows(a, v):
    t = _lane_to_rows(v)
    return a * jnp.concatenate([t] * (a.shape[1] // LANES), axis=1)


def _nsa_sample_kernel(pt_ref, qb_ref, gt_ref, kc_ref, vc_ref, mapt_ref, rsum_ref, slope_ref, lq_ref,
                       selnew_ref, win_ref, *refs, n_pages, n_prev, nb, q_len, win_start):
    del pt_ref
    pages = refs[:n_pages]
    o_ref, sel_scr, m_scr, l_scr, acc_scr, oc_scr = refs[n_pages:]
    j = pl.program_id(1)
    hd = LANES
    kvw = N_KV_HEADS * hd
    kt = n_pages * PAGE_SIZE
    blocks_per_step = kt // SEL_BLOCK
    qb = qb_ref[...]
    slope = slope_ref[...]
    lq = lq_ref[...]
    pos_q = n_prev + lq

    def scores(k_bf16, pos_k):
        dist = pos_q - pos_k
        return jnp.dot(k_bf16, qb, preferred_element_type=F32) - slope * dist, dist

    def online(s, mask, v_bf16):
        s = jnp.where(mask, s, NEG_INF)
        m_old = m_scr[...]
        m_new = jnp.maximum(m_old, jnp.max(s, axis=0, keepdims=True))
        alpha = jnp.exp(m_old - m_new)
        p = jnp.where(mask, jnp.exp(s - m_new), 0.0)
        l_scr[...] = alpha * l_scr[...] + jnp.sum(p, axis=0, keepdims=True)
        m_scr[...] = m_new
        pv = lax.dot_general(p.astype(BF16), v_bf16, TN_DIMS, preferred_element_type=F32)
        acc_scr[...] = _scale_rows(acc_scr[...], alpha) + pv

    @pl.when(j == 0)
    def _():
        n_c = kc_ref.shape[0]
        nbp = sel_scr.shape[0]
        pos_c = (lax.broadcasted_iota(jnp.int32, (n_c, 1), 0) * CMP_STRIDE + (CMP_BLOCK - 1)).astype(F32)
        s, dist = scores(kc_ref[...], pos_c)
        p = _masked_softmax(s, dist >= 0, 0)
        oc_scr[...] = lax.dot_general(p.astype(BF16), vc_ref[...], TN_DIMS, preferred_element_type=F32)
        p_r = jnp.dot(p, rsum_ref[...], precision=lax.Precision.HIGHEST, preferred_element_type=F32)
        imp = jnp.dot(mapt_ref[...], p_r, precision=lax.Precision.HIGHEST, preferred_element_type=F32)
        blk = lax.broadcasted_iota(jnp.int32, (nbp, 1), 0)
        posq_i = n_prev + lq.astype(jnp.int32)
        cur = posq_i // SEL_BLOCK
        forced = (blk == 0) | (blk == cur) | (blk == cur - 1)
        valid = blk * SEL_BLOCK <= posq_i
        imp = jnp.where(valid, jnp.where(forced, imp + FORCE_BONUS, imp), NEG_INF)
        blk_f = blk.astype(F32)

        def pick(_, carry):
            work, chosen = carry
            top = jnp.max(work, axis=0, keepdims=True)
            first = jnp.min(jnp.where(work == top, blk_f, float(nbp)), axis=0, keepdims=True)
            hit = blk_f == first
            return jnp.where(hit, TAKEN, work), jnp.where(hit, 1.0, chosen)

        _, chosen = lax.fori_loop(0, min(N_SEL, nb), pick, (imp, jnp.zeros((nbp, LANES), F32)))
        sel_scr[...] = chosen
        m_scr[...] = jnp.full(m_scr.shape, NEG_INF, F32)
        l_scr[...] = jnp.zeros(l_scr.shape, F32)
        acc_scr[...] = jnp.zeros(acc_scr.shape, F32)

    heads = 2 * N_KV_HEADS

    def head_rows(h):
        return jnp.concatenate([pg[pl.ds(h, PAGE_SIZE, stride=heads), :] for pg in pages], axis=0)

    k_all = jnp.concatenate([head_rows(g) for g in range(N_KV_HEADS)], axis=1).astype(BF16)
    v_all = jnp.concatenate([head_rows(N_KV_HEADS + g) for g in range(N_KV_HEADS)], axis=1).astype(BF16)
    pos_k = (j * kt + lax.broadcasted_iota(jnp.int32, (kt, 1), 0)).astype(F32)
    s, dist = scores(k_all, pos_k)
    selrows = sel_scr[pl.ds(pl.multiple_of(j * blocks_per_step, blocks_per_step), blocks_per_step), :]
    selm = jnp.broadcast_to(selrows[:, None, :], (blocks_per_step, SEL_BLOCK, LANES)).reshape(kt, LANES)
    online(s, (dist >= 0) & (selm > 0.5), v_all)

    @pl.when(j == pl.num_programs(1) - 1)
    def _():
        n_new = selnew_ref.shape[0]
        pos_n = (n_prev + lax.broadcasted_iota(jnp.int32, (n_new, 1), 0)).astype(F32)
        s, dist = scores(selnew_ref[:, 0:kvw].astype(BF16), pos_n)
        selrow = sel_scr[n_prev // SEL_BLOCK:n_prev // SEL_BLOCK + 1, :]
        online(s, (dist >= 0) & (selrow > 0.5), selnew_ref[:, kvw:2 * kvw].astype(BF16))
        l_s = l_scr[...]
        o_s = _scale_rows(acc_scr[...], 1.0 / jnp.where(l_s > 0, l_s, 1.0))

        n_w = win_ref.shape[0]
        pos_w = (win_start + lax.broadcasted_iota(jnp.int32, (n_w, 1), 0)).astype(F32)
        s, dist = scores(win_ref[:, 0:kvw].astype(BF16), pos_w)
        p = _masked_softmax(s, (dist >= 0) & (dist < WINDOW), 0)
        o_w = lax.dot_general(p.astype(BF16), win_ref[:, kvw:2 * kvw].astype(BF16), TN_DIMS,
                              preferred_element_type=F32)
        o_c = oc_scr[...]

        rr = HEADS_PER_GROUP
        for g in range(N_KV_HEADS):
            for r in range(rr):
                h = g * rr + r
                rows = slice((g * rr + r) * q_len, (g * rr + r + 1) * q_len)
                cols = slice(g * hd, (g + 1) * hd)
                gc = gt_ref[:, h * N_BRANCH + 0:h * N_BRANCH + 1]
                gs = gt_ref[:, h * N_BRANCH + 1:h * N_BRANCH + 2]
                gw = gt_ref[:, h * N_BRANCH + 2:h * N_BRANCH + 3]
                o_ref[:, h * hd:(h + 1) * hd] = gc * o_c[rows, cols] + gs * o_s[rows, cols] + gw * o_w[rows, cols]


def nsa_sample(q_f32, gates_f32, kc, vc, sel_pages, page_table, sel_new, win_full, *, n_prev):
    b, q_len, qw = q_f32.shape
    hd = LANES
    rr = HEADS_PER_GROUP
    g_n = N_KV_HEADS
    assert g_n * rr * q_len == LANES and n_prev % SEL_BLOCK == 0 and q_len <= SEL_BLOCK
    npg = page_table.shape[1]
    n_p = min(SAMPLE_PAGES, npg)
    assert npg % n_p == 0 and (n_p * PAGE_SIZE // SEL_BLOCK) % SUBLANES == 0
    n_ch = kc.shape[1]
    nb = -(-(n_prev + q_len) // SEL_BLOCK)
    nbp = -(-nb // SUBLANES) * SUBLANES
    width = sel_new.shape[2]
    page_rows = sel_pages.shape[1]
    assert page_rows == PAGE_SIZE * 2 * g_n and sel_pages.shape[2] == hd

    qt = q_f32.reshape(b, q_len, g_n, rr, hd).transpose(0, 2, 4, 3, 1).reshape(b, g_n, hd, rr * q_len)
    qblk = jnp.einsum('bgdl,gh->bgdhl', qt, jnp.eye(g_n, dtype=F32)).reshape(b, g_n * hd, LANES).astype(BF16)

    lane = np.arange(LANES)
    lane_g, lane_r, lane_q = lane // (rr * q_len), (lane // q_len) % rr, lane % q_len
    slope = np.array([_alibi_slope(h) for h in lane_g * rr + lane_r], np.float32).reshape(1, LANES)
    lq = lane_q.astype(np.float32).reshape(1, LANES)
    rsum = ((lane_g[:, None] == lane_g[None, :]) & (lane_q[:, None] == lane_q[None, :])).astype(np.float32)
    mapt = np.zeros((nbp, n_ch), np.float32)
    mapt[:nb, :n_ch - 1] = _cmp_to_sel_map(n_ch - 1, nb).T

    pad16 = lambda a: jnp.pad(a, ((0, 0), (0, (-a.shape[1]) % BF16_ROWS), (0, 0)))
    sel_new_p = pad16(sel_new)
    win_p = pad16(win_full)

    def page_spec(k):
        return pl.BlockSpec((None, page_rows, hd), lambda bi, j, pt: (pt[bi * npg + j * n_p + k], 0, 0))

    const2 = lambda bi, j, pt: (0, 0)
    per_b = lambda bi, j, pt: (bi, 0, 0)
    grid_spec = pltpu.PrefetchScalarGridSpec(
        num_scalar_prefetch=1,
        grid=(b, npg // n_p),
        in_specs=[pl.BlockSpec((None, g_n * hd, LANES), per_b),
                  pl.BlockSpec((None, q_len, LANES), per_b),
                  pl.BlockSpec((None, n_ch, g_n * hd), per_b),
                  pl.BlockSpec((None, n_ch, g_n * hd), per_b),
                  pl.BlockSpec((nbp, n_ch), const2),
                  pl.BlockSpec((LANES, LANES), const2),
                  pl.BlockSpec((1, LANES), const2),
                  pl.BlockSpec((1, LANES), const2),
                  pl.BlockSpec((None, sel_new_p.shape[1], width), per_b),
                  pl.BlockSpec((None, win_p.shape[1], width), per_b)]
        + [page_spec(k) for k in range(n_p)],
        out_specs=pl.BlockSpec((None, q_len, qw), per_b),
        scratch_shapes=[pltpu.VMEM((nbp, LANES), F32),
                        pltpu.VMEM((1, LANES), F32), pltpu.VMEM((1, LANES), F32),
                        pltpu.VMEM((LANES, g_n * hd), F32), pltpu.VMEM((LANES, g_n * hd), F32)],
    )
    kern = functools.partial(_nsa_sample_kernel, n_pages=n_p, n_prev=n_prev, nb=nb, q_len=q_len,
                             win_start=n_prev + q_len - win_full.shape[1])
    return pl.pallas_call(
        kern,
        out_shape=jax.ShapeDtypeStruct((b, q_len, qw), F32),
        grid_spec=grid_spec,
        compiler_params=_cparams(2, 48),
        name="nsa_sample",
    )(page_table.reshape(-1), qblk, gates_f32, kc, vc, jnp.asarray(mapt), jnp.asarray(rsum), jnp.asarray(slope),
      jnp.asarray(lq), sel_new_p, win_p, *([sel_pages] * n_p))


PROMPT_TILE = 512
KV_PROJ_TILE = 256


def _trunk(x, params, *, n_prev, pool_prefix, conv_prefix, cmp_pages, cmp_table, sel_pages, sel_table, win_past,
           seq_tile, seqs_per_tile):
    p = params
    b, t, d = x.shape
    depth = p['ln_mix'].shape[0]
    n_a = len(p['w_pool'])
    hd = LANES
    kvw = N_KV_HEADS * hd
    row_w = 2 * kvw
    is_prompt = cmp_pages is None
    no_halo = jnp.zeros((b, max(POOL_HALO, FFN_HALO), d), F32)
    pool_rows, conv_rows = [], []
    for l in range(depth):
        if l < n_a:
            x, rows = pool_layer(x, x if is_prompt else no_halo, pool_prefix[l], p['ln_mix'][l], p['w_pool'][l], p['pool_scale'][l],
                                 n_prev=n_prev, tm=seq_tile)
            pool_rows.append(rows)
        else:
            xm = x.reshape(b * t, d)
            if l == n_a:
                raw = [None] * N_KV_HEADS
                chunk_gain = 2 * raw + [0] * N_KV_HEADS + raw + [1] * N_KV_HEADS + raw
                kv_f, kv_b = norm_linear(xm, p['ln_kv'], p['w_kv'], p['k_norm'][1:N_BRANCH], chunk_gain,
                                         tm=KV_PROJ_TILE)
                cmp_f, sel_f, win_f = (kv_f[:, br * row_w:(br + 1) * row_w] for br in range(N_BRANCH))
                if is_prompt:
                    cmp_pages = cmp_f.reshape(b * t // PAGE_SIZE, PAGE_SIZE * row_w // hd, hd)
                    cmp_table = jnp.arange(b * t // PAGE_SIZE, dtype=jnp.int32).reshape(b, t // PAGE_SIZE)
                    win_full = win_f.reshape(b, t, row_w)
                else:
                    win_full = jnp.concatenate([win_past, win_f.reshape(b, t, row_w)], axis=1)
                pab = compress_partial(cmp_pages, cmp_table, p['w1cat'])
                kc, vc = compress_finish(pab, p['cmp_pos'], p['w1flat'], p['cmp_w2'], p['k_norm'][0])
            j = l - n_a
            q_f, q_b = norm_linear(xm, p['ln_mix'][l], p['w_qg'][j], p['q_norm'][j:j + 1],
                                   [0] * N_HEADS + [None] * (p['w_qg'][j].shape[1] // hd - N_HEADS),
                                   norm_scale=hd ** -0.5, sigmoid_from=N_HEADS)
            qw = N_HEADS * hd
            if is_prompt:
                o = nsa_prompt(q_b.reshape(b, t, -1), q_f.reshape(b, t, -1), kc, vc, kv_b.reshape(b, t, -1))
            else:
                q3 = q_f.reshape(b, t, -1)
                o = nsa_sample(q3[:, :, :qw], q3[:, :, qw:], kc, vc, sel_pages, sel_table,
                               sel_f.reshape(b, t, row_w), win_full, n_prev=n_prev).astype(BF16)
            x = linear_residual(o.reshape(b * t, qw), p['w_o'][j], xm).reshape(b, t, d)
        x, rows = ffn_layer(x, x if is_prompt else no_halo, conv_prefix[l], p['ln_ffn'][l], p['w_in'], p['conv_w'][l],
                            p['conv_b'][l], p['w_out'], layer=l, nb=seqs_per_tile, tm=seq_tile)
        conv_rows.append(rows[:, FFN_HALO - (CONV_WIDTH - 1):])
    row_shape = (2, N_KV_HEADS, hd)
    n_win = min(WINDOW, n_prev + t)
    win_state = win_full[:, win_full.shape[1] - n_win:].reshape((b, n_win) + row_shape)
    return (x, cmp_f.reshape((b, t) + row_shape), sel_f.reshape((b, t) + row_shape), win_state,
            jnp.stack(pool_rows), jnp.stack(conv_rows))


def kernel(x_prompt, x_sample, cache_kv_cmp, cache_kv_sel, page_table, state_kv_win, state_pool, state_conv, ln_mix, ln_ffn, w_pool, pool_scale, ln_kv, w_kv, k_norm, cmp_w1, cmp_pos, cmp_w2, w_qg, q_norm, w_o, w_in, conv_w, conv_b, w_out):
    depth = ln_mix.shape[0]
    n_a = w_pool.shape[0]
    d = x_prompt.shape[2]
    ff2 = w_in.shape[2]
    hd = LANES
    row_w = 2 * N_KV_HEADS * hd
    qg_pad = (-w_qg.shape[2]) % LANES
    half = CMP_BLOCK // 2
    params = dict(
        ln_mix=ln_mix, ln_ffn=ln_ffn, pool_scale=pool_scale, ln_kv=ln_kv, k_norm=k_norm, q_norm=q_norm,
        conv_w=conv_w, conv_b=conv_b, cmp_pos=cmp_pos,
        w_pool=[w_pool[l].astype(BF16) for l in range(n_a)], w_kv=w_kv.astype(BF16),
        w_qg=[jnp.pad(w_qg[j], ((0, 0), (0, qg_pad))).astype(BF16) for j in range(depth - n_a)],
        w_o=[w_o[j].astype(BF16) for j in range(depth - n_a)],
        w_in=w_in.astype(BF16), w_out=w_out.astype(BF16),
        cmp_w2=cmp_w2.astype(BF16),
        w1cat=jnp.concatenate([cmp_w1[:, :half], cmp_w1[:, half:]], axis=-1).astype(BF16),
        w1flat=cmp_w1.reshape(2, CMP_BLOCK * hd, cmp_w1.shape[3]).astype(BF16),
    )

    b_p, t_p, _ = x_prompt.shape
    prompt = _trunk(
        x_prompt, params, n_prev=0,
        pool_prefix=jnp.zeros((n_a, b_p, POOL_HALO, d), F32),
        conv_prefix=jnp.zeros((depth, b_p, FFN_HALO, ff2), F32),
        cmp_pages=None, cmp_table=None, sel_pages=None, sel_table=None, win_past=None,
        seq_tile=min(PROMPT_TILE, t_p), seqs_per_tile=1)

    b_s, t_s, _ = x_sample.shape
    n_prev = page_table.shape[1] * PAGE_SIZE
    assert t_s < CMP_STRIDE and cache_kv_cmp.shape[1] == PAGE_SIZE
    sample = _trunk(
        x_sample, params, n_prev=n_prev,
        pool_prefix=jnp.pad(state_pool, ((0, 0), (0, 0), (POOL_HALO - POOL_KEEP, 0), (0, 0))),
        conv_prefix=jnp.pad(state_conv, ((0, 0), (0, 0), (FFN_HALO - (CONV_WIDTH - 1), 0), (0, 0))),
        cmp_pages=cache_kv_cmp.reshape(cache_kv_cmp.shape[0], PAGE_SIZE * row_w // hd, hd), cmp_table=page_table,
        sel_pages=cache_kv_sel.reshape(cache_kv_sel.shape[0], PAGE_SIZE * row_w // hd, hd), sel_table=page_table,
        win_past=state_kv_win.reshape(b_s, state_kv_win.shape[1], row_w),
        seq_tile=t_s, seqs_per_tile=b_s)

    return tuple(leaf for pair in zip(prompt, sample) for leaf in pair)
```

```python
import functools

import numpy as np
import jax
import jax.numpy as jnp
from jax import lax
from jax.experimental import pallas as pl
from jax.experimental.pallas import tpu as pltpu

F32 = jnp.float32
BF16 = jnp.bfloat16

POOL_WINDOWS = (2, 4, 8, 16)
POOL_KEEP = max(POOL_WINDOWS) - 1
N_HEADS = 16
N_KV_HEADS = 4
HEADS_PER_GROUP = N_HEADS // N_KV_HEADS
N_BRANCH = 3
CMP_BLOCK = 32
CMP_STRIDE = 16
SEL_BLOCK = 64
N_SEL = 16
WINDOW = 512
FORCE_BONUS = 1e4
CONV_WIDTH = 3
Q_BLOCK = 128
PAGE_SIZE = 128
EPS = 1e-6
NEG_INF = -1e30

LANES = 128
SUBLANES = 8
BF16_ROWS = 16
MIB = 2 ** 20


def _cparams(n_axes, vmem_mib):
    return pltpu.CompilerParams(dimension_semantics=("arbitrary",) * n_axes,
                                vmem_limit_bytes=vmem_mib * MIB)


def _rms(x, g):
    return (x * lax.rsqrt(jnp.mean(x * x, axis=-1, keepdims=True) + EPS)) * g


def _alibi_slope(h):
    return float(np.float32(2.0 ** (-8.0 * (h + 1) / N_HEADS)))


def _norm_linear_kernel(x_ref, g_ref, w_ref, hg_ref, of_ref, ob_ref, *, chunk_gain, norm_scale, sigmoid_from):
    h = _rms(x_ref[...], g_ref[...]).astype(BF16)
    y = jnp.dot(h, w_ref[...], preferred_element_type=F32)
    for c in range(y.shape[1] // LANES):
        cols = slice(c * LANES, (c + 1) * LANES)
        yc = y[:, cols]
        if chunk_gain[c] is not None:
            yc = _rms(yc, hg_ref[chunk_gain[c]:chunk_gain[c] + 1, :])
            if norm_scale is not None:
                yc = yc * norm_scale
        elif sigmoid_from is not None and c >= sigmoid_from:
            yc = jax.nn.sigmoid(yc)
        of_ref[:, cols] = yc
        ob_ref[:, cols] = yc.astype(BF16)


def norm_linear(x, gain, w_bf16, head_gains, chunk_gain, *, norm_scale=None, sigmoid_from=None, tm=512):
    m, k = x.shape
    n = w_bf16.shape[1]
    tm = min(tm, m)
    assert m % tm == 0 and n % LANES == 0 and len(chunk_gain) == n // LANES
    kern = functools.partial(_norm_linear_kernel, chunk_gain=tuple(chunk_gain), norm_scale=norm_scale,
                             sigmoid_from=sigmoid_from)
    return pl.pallas_call(
        kern,
        out_shape=(jax.ShapeDtypeStruct((m, n), F32), jax.ShapeDtypeStruct((m, n), BF16)),
        grid=(m // tm,),
        in_specs=[pl.BlockSpec((tm, k), lambda i: (i, 0)),
                  pl.BlockSpec((1, k), lambda i: (0, 0)),
                  pl.BlockSpec((k, n), lambda i: (0, 0)),
                  pl.BlockSpec(head_gains.shape, lambda i: (0, 0))],
        out_specs=(pl.BlockSpec((tm, n), lambda i: (i, 0)), pl.BlockSpec((tm, n), lambda i: (i, 0))),
        compiler_params=_cparams(1, 56),
        name="norm_linear",
    )(x, gain.reshape(1, k), w_bf16, head_gains)


def _linear_residual_kernel(a_ref, w_ref, r_ref, o_ref):
    o_ref[...] = r_ref[...] + jnp.dot(a_ref[...], w_ref[...], preferred_element_type=F32)


def linear_residual(a_bf16, w_bf16, res, *, tm=512):
    m, k = a_bf16.shape
    n = w_bf16.shape[1]
    tm = min(tm, m)
    assert m % tm == 0
    return pl.pallas_call(
        _linear_residual_kernel,
        out_shape=jax.ShapeDtypeStruct((m, n), F32),
        grid=(m // tm,),
        in_specs=[pl.BlockSpec((tm, k), lambda i: (i, 0)),
                  pl.BlockSpec((k, n), lambda i: (0, 0)),
                  pl.BlockSpec((tm, n), lambda i: (i, 0))],
        out_specs=pl.BlockSpec((tm, n), lambda i: (i, 0)),
        compiler_params=_cparams(1, 48),
        name="linear_residual",
    )(a_bf16, w_bf16, res)


POOL_HALO = 16


def _pool_kernel(x_ref, xh_ref, pre_ref, g_ref, w_ref, sc_ref, o_ref, st_ref, ctx_scr, *, tm, n_prev):
    i = pl.program_id(1)
    x = x_ref[...]
    h = _rms(x, g_ref[...])
    halo = _rms(xh_ref[...], g_ref[...])
    ctx_scr[0:POOL_HALO, :] = jnp.where(i == 0, pre_ref[...], halo)
    ctx_scr[POOL_HALO:POOL_HALO + tm, :] = h
    t = i * tm + lax.broadcasted_iota(jnp.int32, (tm, 1), 0)
    gw = x.shape[1] // len(POOL_WINDOWS)
    for gi, w in enumerate(POOL_WINDOWS):
        cols = slice(gi * gw, (gi + 1) * gw)
        hs = h[:, cols]
        acc = hs
        for k in range(1, w):
            acc = acc + ctx_scr[POOL_HALO - k:POOL_HALO - k + tm, cols]
        cnt = jnp.minimum(w, n_prev + t + 1).astype(F32)
        pooled = acc / cnt - hs
        mixed = jnp.dot(pooled.astype(BF16), w_ref[gi], preferred_element_type=F32)
        o_ref[:, cols] = x[:, cols] + mixed * sc_ref[:, cols]
    st_ref[...] = ctx_scr[tm + POOL_HALO - POOL_KEEP:tm + POOL_HALO, :]


def pool_layer(x, x_halo_src, prefix16, gain, w_pool_bf16, scale, *, n_prev, tm):
    b, t, d = x.shape
    assert t % tm == 0 and tm % SUBLANES == 0
    hb = tm // POOL_HALO
    kern = functools.partial(_pool_kernel, tm=tm, n_prev=n_prev)
    return pl.pallas_call(
        kern,
        out_shape=(jax.ShapeDtypeStruct((b, t, d), F32), jax.ShapeDtypeStruct((b, POOL_KEEP, d), F32)),
        grid=(b, t // tm),
        in_specs=[pl.BlockSpec((None, tm, d), lambda bi, i: (bi, i, 0)),
                  pl.BlockSpec((None, POOL_HALO, d), lambda bi, i: (bi, jnp.maximum(i * hb - 1, 0), 0)),
                  pl.BlockSpec((None, POOL_HALO, d), lambda bi, i: (bi, 0, 0)),
                  pl.BlockSpec((1, d), lambda bi, i: (0, 0)),
                  pl.BlockSpec(w_pool_bf16.shape, lambda bi, i: (0, 0, 0)),
                  pl.BlockSpec((1, d), lambda bi, i: (0, 0))],
        out_specs=(pl.BlockSpec((None, tm, d), lambda bi, i: (bi, i, 0)),
                   pl.BlockSpec((None, POOL_KEEP, d), lambda bi, i: (bi, 0, 0))),
        scratch_shapes=[pltpu.VMEM((tm + POOL_HALO, d), F32)],
        compiler_params=_cparams(2, 48),
        name="pool_layer",
    )(x, x_halo_src, prefix16, gain.reshape(1, d), w_pool_bf16, scale.reshape(1, d))


FFN_HALO = 16
FFN_SPLIT = 2


def _ffn_kernel(x_ref, xh_ref, pg_ref, pv_ref, g_ref, wig_ref, wiv_ref, cwg_ref, cwv_ref, cbg_ref, cbv_ref, wo_ref,
                o_ref, sg_ref, sv_ref, u_scr, acc_scr, *h_scrs, nb, tm, te):
    hg_scr, hv_scr = h_scrs[:FFN_SPLIT], h_scrs[FFN_SPLIT:]
    i = pl.program_id(1)
    f = pl.program_id(2)
    n_ext = nb * te
    n_out = n_ext - FFN_HALO

    @pl.when(f == 0)
    def _():
        for s in range(nb):
            u_scr[s * te:s * te + FFN_HALO, :] = _rms(xh_ref[s], g_ref[...]).astype(BF16)
            u_scr[s * te + FFN_HALO:s * te + FFN_HALO + tm, :] = _rms(x_ref[s], g_ref[...]).astype(BF16)
            if te > FFN_HALO + tm:
                u_scr[s * te + FFN_HALO + tm:(s + 1) * te, :] = jnp.zeros((te - FFN_HALO - tm, u_scr.shape[1]), BF16)
        acc_scr[...] = jnp.zeros_like(acc_scr)

    u = u_scr[...]
    n_split = FFN_SPLIT
    pw = hg_scr[0].shape[1]
    first = i == 0

    def up_project(h_scr, w_ref, p_ref, piece):
        cols = slice(piece * pw, (piece + 1) * pw)
        h = jnp.dot(u, w_ref[:, cols], preferred_element_type=F32)
        h_scr[...] = h
        for s in range(nb):
            halo = slice(s * te, s * te + FFN_HALO)
            h_scr[halo, :] = jnp.where(first, p_ref[s, :, cols], h[halo])

    def conv(h_scr, cw_ref, cb_ref, piece):
        cols = slice(piece * pw, (piece + 1) * pw)
        c = cb_ref[:, cols] + cw_ref[0:1, cols] * h_scr[FFN_HALO - 2:FFN_HALO - 2 + n_out, :]
        c = c + cw_ref[1:2, cols] * h_scr[FFN_HALO - 1:FFN_HALO - 1 + n_out, :]
        return c + cw_ref[2:3, cols] * h_scr[FFN_HALO:FFN_HALO + n_out, :]

    down = None
    for piece in range(n_split):
        up_project(hg_scr[piece], wig_ref, pg_ref, piece)
        up_project(hv_scr[piece], wiv_ref, pv_ref, piece)
        cg = conv(hg_scr[piece], cwg_ref, cbg_ref, piece)
        cv = conv(hv_scr[piece], cwv_ref, cbv_ref, piece)
        act = (cg * jax.nn.sigmoid(cg)) * cv
        d = jnp.dot(act.astype(BF16), wo_ref[piece * pw:(piece + 1) * pw, :], preferred_element_type=F32)
        down = d if down is None else down + d
    acc_scr[...] += down

    for s in range(nb):
        for piece in range(n_split):
            cols = slice(piece * pw, (piece + 1) * pw)
            sg_ref[s, :, cols] = hg_scr[piece][s * te + tm:s * te + tm + FFN_HALO, :]
            sv_ref[s, :, cols] = hv_scr[piece][s * te + tm:s * te + tm + FFN_HALO, :]

    @pl.when(f == pl.num_programs(2) - 1)
    def _():
        for s in range(nb):
            o_ref[s] = x_ref[s] + acc_scr[s * te:s * te + tm, :]


def ffn_layer(x, x_halo_src, prefix16, gain, w_in_bf16, conv_w, conv_b, w_out_bf16, *, layer, nb, tm, tf=512):
    b, t, d = x.shape
    ff = w_out_bf16.shape[1]
    assert t % tm == 0 and b % nb == 0 and ff % tf == 0 and tm % SUBLANES == 0
    nf = ff // tf
    hb = tm // FFN_HALO
    te = FFN_HALO + -(-tm // BF16_ROWS) * BF16_ROWS
    n_ext = nb * te
    kern = functools.partial(_ffn_kernel, nb=nb, tm=tm, te=te)
    gate_col = lambda bi, i, f: (0, f)
    val_col = lambda bi, i, f: (0, nf + f)
    y, sg, sv = pl.pallas_call(
        kern,
        out_shape=(jax.ShapeDtypeStruct((b, t, d), F32),
                   jax.ShapeDtypeStruct((b, t // tm, FFN_HALO, ff), F32),
                   jax.ShapeDtypeStruct((b, t // tm, FFN_HALO, ff), F32)),
        grid=(b // nb, t // tm, nf),
        in_specs=[pl.BlockSpec((nb, tm, d), lambda bi, i, f: (bi, i, 0)),
                  pl.BlockSpec((nb, FFN_HALO, d), lambda bi, i, f: (bi, jnp.maximum(i * hb - 1, 0), 0)),
                  pl.BlockSpec((nb, FFN_HALO, tf), lambda bi, i, f: (bi, 0, f)),
                  pl.BlockSpec((nb, FFN_HALO, tf), lambda bi, i, f: (bi, 0, nf + f)),
                  pl.BlockSpec((1, d), lambda bi, i, f: (0, 0)),
                  pl.BlockSpec((None, d, tf), lambda bi, i, f: (layer, 0, f)),
                  pl.BlockSpec((None, d, tf), lambda bi, i, f: (layer, 0, nf + f)),
                  pl.BlockSpec((CONV_WIDTH, tf), gate_col),
                  pl.BlockSpec((CONV_WIDTH, tf), val_col),
                  pl.BlockSpec((1, tf), gate_col),
                  pl.BlockSpec((1, tf), val_col),
                  pl.BlockSpec((None, tf, d), lambda bi, i, f: (layer, f, 0))],
        out_specs=(pl.BlockSpec((nb, tm, d), lambda bi, i, f: (bi, i, 0)),
                   pl.BlockSpec((nb, None, FFN_HALO, tf), lambda bi, i, f: (bi, i, 0, f)),
                   pl.BlockSpec((nb, None, FFN_HALO, tf), lambda bi, i, f: (bi, i, 0, f))),
        scratch_shapes=[pltpu.VMEM((n_ext, d), BF16), pltpu.VMEM((n_ext - FFN_HALO, d), F32)]
        + [pltpu.VMEM((n_ext, tf // FFN_SPLIT), F32)] * (2 * FFN_SPLIT),
        compiler_params=_cparams(3, 56),
        name="ffn_layer",
    )(x, x_halo_src, prefix16, prefix16, gain.reshape(1, d), w_in_bf16, w_in_bf16, conv_w, conv_w,
      conv_b.reshape(1, 2 * ff), conv_b.reshape(1, 2 * ff), w_out_bf16)
    return y, jnp.concatenate([sg[:, -1], sv[:, -1]], axis=-1)


CMP_PAGES = 16
CHUNKS_PER_PAGE = PAGE_SIZE // CMP_STRIDE


def _compress_kernel(pt_ref, *refs, n_pages):
    del pt_ref
    page_refs = refs[:n_pages]
    w_ref, o_ref = refs[n_pages], refs[n_pages + 1]
    heads = 2 * N_KV_HEADS
    k_rows = lax.broadcasted_iota(jnp.int32, (heads, LANES), 0) < N_KV_HEADS
    acc = [None, None]
    for s in range(CMP_STRIDE):
        lhs = [[], []]
        for k in range(n_pages):
            for n in range(0, CHUNKS_PER_PAGE, 2):
                t_e = page_refs[k][(CMP_STRIDE * n + s) * heads:(CMP_STRIDE * n + s + 1) * heads, :]
                t_o = page_refs[k][(CMP_STRIDE * (n + 1) + s) * heads:(CMP_STRIDE * (n + 1) + s + 1) * heads, :]
                lhs[0].append(jnp.where(k_rows, t_e, pltpu.roll(t_o, N_KV_HEADS, 0)))
                lhs[1].append(jnp.where(k_rows, pltpu.roll(t_e, N_KV_HEADS, 0), t_o))
        for v in range(2):
            a = jnp.concatenate(lhs[v], axis=0).astype(BF16)
            d = jnp.dot(a, w_ref[v, s], preferred_element_type=F32)
            acc[v] = d if acc[v] is None else acc[v] + d
    for v in range(2):
        o_ref[v] = acc[v]


def compress_partial(pages, page_table, w1cat_bf16):
    s, npg = page_table.shape
    page_rows, hd = pages.shape[1:]
    assert page_rows == PAGE_SIZE * 2 * N_KV_HEADS
    hid2 = w1cat_bf16.shape[3]
    n_p = min(CMP_PAGES, npg)
    assert npg % n_p == 0
    n_ch = npg * CHUNKS_PER_PAGE

    def page_spec(k):
        return pl.BlockSpec((None, page_rows, hd), lambda si, j, pt: (pt[si * npg + j * n_p + k], 0, 0))

    grid_spec = pltpu.PrefetchScalarGridSpec(
        num_scalar_prefetch=1,
        grid=(s, npg // n_p),
        in_specs=[page_spec(k) for k in range(n_p)]
        + [pl.BlockSpec(w1cat_bf16.shape, lambda si, j, pt: (0, 0, 0, 0))],
        out_specs=pl.BlockSpec((None, 2, n_p * CHUNKS_PER_PAGE * N_KV_HEADS, hid2), lambda si, j, pt: (si, 0, j, 0)),
    )
    return pl.pallas_call(
        functools.partial(_compress_kernel, n_pages=n_p),
        out_shape=jax.ShapeDtypeStruct((s, 2, n_ch * N_KV_HEADS, hid2), F32),
        grid_spec=grid_spec,
        compiler_params=_cparams(2, 48),
        name="compress_partial",
    )(page_table.reshape(-1), *([pages] * n_p), w1cat_bf16)


def _compress_finish_kernel(pab_ref, pos_ref, w1_ref, w2_ref, kn_ref, kc_ref, vc_ref):
    n = pab_ref.shape[1]
    hid = w2_ref.shape[1]
    for v in range(2):
        blk = pab_ref[v]
        nxt = pltpu.roll(blk[:, hid:], n - N_KV_HEADS, 0)
        pos = jnp.broadcast_to(pos_ref[v], (SUBLANES, pos_ref.shape[2])).astype(BF16)
        bias = jnp.dot(pos, w1_ref[v], preferred_element_type=F32)[0:1, :]
        hdn = jax.nn.gelu(blk[:, :hid] + nxt + bias)
        out = jnp.dot(hdn.astype(BF16), w2_ref[v], preferred_element_type=F32)
        if v == 0:
            kc_ref[...] = _rms(out, kn_ref[...]).astype(BF16)
        else:
            vc_ref[...] = out.astype(BF16)


def compress_finish(pab, cmp_pos, w1flat_bf16, w2_bf16, k_gain):
    s, _, rows, hid2 = pab.shape
    hd = w2_bf16.shape[2]
    kdim = w1flat_bf16.shape[1]
    out = jax.ShapeDtypeStruct((s, rows, hd), BF16)
    kc, vc = pl.pallas_call(
        _compress_finish_kernel,
        out_shape=(out, out),
        grid=(s,),
        in_specs=[pl.BlockSpec((None, 2, rows, hid2), lambda si: (si, 0, 0, 0)),
                  pl.BlockSpec((2, 1, kdim), lambda si: (0, 0, 0)),
                  pl.BlockSpec(w1flat_bf16.shape, lambda si: (0, 0, 0)),
                  pl.BlockSpec(w2_bf16.shape, lambda si: (0, 0, 0)),
                  pl.BlockSpec((1, hd), lambda si: (0, 0))],
        out_specs=(pl.BlockSpec((None, rows, hd), lambda si: (si, 0, 0)),
                   pl.BlockSpec((None, rows, hd), lambda si: (si, 0, 0))),
        compiler_params=_cparams(1, 48),
        name="compress_finish",
    )(pab, cmp_pos.reshape(2, 1, kdim), w1flat_bf16, w2_bf16, k_gain.reshape(1, hd))
    return (kc.reshape(s, rows // N_KV_HEADS, N_KV_HEADS * hd), vc.reshape(s, rows // N_KV_HEADS, N_KV_HEADS * hd))


def _cmp_to_sel_map(nc, nb):
    c_s = np.arange(nc) * CMP_STRIDE
    c_e = c_s + CMP_BLOCK - 1
    s_s = np.arange(nb) * SEL_BLOCK
    s_e = s_s + SEL_BLOCK - 1
    return ((c_s[:, None] <= s_e[None]) & (c_e[:, None] >= s_s[None])).astype(np.float32)


def _masked_softmax(s, mask, axis):
    s = jnp.where(mask, s, NEG_INF)
    m = jnp.max(s, axis=axis, keepdims=True)
    p = jnp.where(mask, jnp.exp(s - m), 0.0)
    den = jnp.sum(p, axis=axis, keepdims=True)
    return p * (1.0 / jnp.where(den > 0, den, 1.0))


SEL_KEY_TILE = 512


POS_SPLIT = 3
BLOCK_LANES = 64
MASK_BIAS = -1e30


def _position_columns(pos, onehot):
    pos = np.asarray(pos)
    assert pos.max() // SEL_BLOCK <= 256
    out = np.zeros((pos.shape[0], LANES), np.float32)
    if onehot:
        assert pos.max() // SEL_BLOCK < BLOCK_LANES
        out[np.arange(pos.shape[0]), pos // SEL_BLOCK] = 1.0
    for j in range(POS_SPLIT):
        out[:, BLOCK_LANES + j] = (pos // SEL_BLOCK) * SEL_BLOCK
        out[:, BLOCK_LANES + POS_SPLIT + j] = pos % SEL_BLOCK
    return out


def _slope_columns():
    out = np.zeros((N_HEADS, LANES), np.float32)
    for h in range(N_HEADS):
        rem = np.float32(_alibi_slope(h))
        for j in range(POS_SPLIT):
            part = np.float32(rem.astype(jnp.bfloat16))
            out[h, BLOCK_LANES + j] = part
            out[h, BLOCK_LANES + POS_SPLIT + j] = part
            rem = np.float32(rem - part)
        assert rem == 0
    return out


def _nsa_prompt_kernel(q_ref, gt_ref, kc_ref, vc_ref, cpos_ref, selk_ref, selv_ref, wink_ref, winv_ref,
                       spos_ref, wpos_ref, mapt_ref, slope_ref, o_ref, used_ref, *, wlen):
    i = pl.program_id(1)
    s0 = i * Q_BLOCK
    hd = LANES
    rr, qb, tk = HEADS_PER_GROUP, Q_BLOCK, SEL_KEY_TILE
    m_rows = rr * qb
    n_cp = kc_ref.shape[0]
    nb = mapt_ref.shape[0]
    nt_dims = (((1,), (1,)), ((), ()))
    kt_last = s0 // tk
    ws = pl.multiple_of(jnp.maximum(s0 - WINDOW, 0), Q_BLOCK)

    row = lax.broadcasted_iota(jnp.int32, (m_rows, 1), 0)
    pos_q = s0 + (row & (qb - 1))
    posq_l = s0 + lax.broadcasted_iota(jnp.int32, (1, qb), 1)
    blk = lax.broadcasted_iota(jnp.int32, (nb, 1), 0)
    cur = posq_l // SEL_BLOCK
    forced = (blk == 0) | (blk == cur) | (blk == cur - 1)
    valid = blk * SEL_BLOCK <= posq_l
    cmask = lax.broadcasted_iota(jnp.int32, (1, n_cp), 1) * CMP_STRIDE + (CMP_BLOCK - 1) <= pos_q
    dist_w = pos_q - (ws + lax.broadcasted_iota(jnp.int32, (1, wlen), 1))
    wmask = (dist_w >= 0) & (dist_w < WINDOW)

    for g in range(N_KV_HEADS):
        cols = slice(g * hd, (g + 1) * hd)
        q = jnp.concatenate([q_ref[:, (g * rr + r) * hd:(g * rr + r + 1) * hd] for r in range(rr)], axis=0)
        slope_cols = slope_ref[g]
        q_pos = jnp.concatenate([q, slope_cols.astype(BF16)], axis=1)

        kc_aug = jnp.concatenate([kc_ref[:, cols], cpos_ref[...]], axis=1)
        s = lax.dot_general(q_pos, kc_aug, nt_dims, preferred_element_type=F32)
        p = _masked_softmax(s, cmask, -1)
        o_c = jnp.dot(p.astype(BF16), vc_ref[:, cols], preferred_element_type=F32)

        p_sum = p[0:qb]
        for r in range(1, rr):
            p_sum = p_sum + p[r * qb:(r + 1) * qb]
        imp = lax.dot_general(mapt_ref[...], p_sum, nt_dims, precision=lax.Precision.HIGHEST,
                              preferred_element_type=F32)
        imp = jnp.where(valid, jnp.where(forced, imp + FORCE_BONUS, imp), NEG_INF)
        cnt = jnp.zeros((nb, qb), F32)
        for c in range(nb):
            rowc = imp[c:c + 1, :]
            beats = (rowc > imp) | ((rowc == imp) & (blk > c))
            cnt = cnt + jnp.where(beats, 1.0, 0.0)
        chosen = cnt < min(N_SEL, nb)
        bias_t = jnp.where(chosen, 0.0, MASK_BIAS)
        chosen_f = jnp.where(chosen, 1.0, 0.0)
        n_used = jnp.int32(0)
        for t in range(nb // (tk // SEL_BLOCK)):
            any_sel = jnp.max(chosen_f[t * (tk // SEL_BLOCK):(t + 1) * (tk // SEL_BLOCK), :]) > 0
            used_ref[n_used] = t
            n_used = n_used + jnp.where(any_sel & (t < kt_last), 1, 0)
        bias = jnp.transpose(jnp.concatenate([bias_t, jnp.zeros((hd - nb, qb), F32)], axis=0))
        q_sel = jnp.concatenate([q, (slope_cols + jnp.concatenate([bias] * rr, axis=0)).astype(BF16)], axis=1)

        def sel_tile(kt, carry, causal):
            m_i, l_i, acc = carry
            k0 = pl.multiple_of(kt * tk, tk)
            k_aug = jnp.concatenate([selk_ref[pl.ds(k0, tk), cols], spos_ref[pl.ds(k0, tk), :]], axis=1)
            s = lax.dot_general(q_sel, k_aug, nt_dims, preferred_element_type=F32)
            if causal:
                s = jnp.where(k0 + lax.broadcasted_iota(jnp.int32, (1, tk), 1) <= pos_q, s, NEG_INF)
            m_new = jnp.maximum(m_i, jnp.max(s, axis=-1, keepdims=True))
            alpha = jnp.exp(m_i - m_new)
            p = jnp.exp(s - m_new)
            l_new = alpha * l_i + jnp.sum(p, axis=-1, keepdims=True)
            acc = alpha * acc + jnp.dot(p.astype(BF16), selv_ref[pl.ds(k0, tk), cols], preferred_element_type=F32)
            return m_new, l_new, acc

        init = (jnp.full((m_rows, 1), NEG_INF, F32), jnp.zeros((m_rows, 1), F32), jnp.zeros((m_rows, hd), F32))

        carry = lax.fori_loop(0, n_used, lambda j, c: sel_tile(used_ref[j], c, False), init)
        _, l_s, acc_s = sel_tile(kt_last, carry, True)
        o_s = acc_s * (1.0 / jnp.where(l_s > 0, l_s, 1.0))

        kw_aug = jnp.concatenate([wink_ref[pl.ds(ws, wlen), cols], wpos_ref[pl.ds(ws, wlen), :]], axis=1)
        s = jnp.where(wmask, lax.dot_general(q_pos, kw_aug, nt_dims, preferred_element_type=F32), NEG_INF)
        p = jnp.exp(s - jnp.max(s, axis=-1, keepdims=True))
        l_w = jnp.sum(p, axis=-1, keepdims=True)
        o_w = jnp.dot(p.astype(BF16), winv_ref[pl.ds(ws, wlen), cols], preferred_element_type=F32)
        o_w = o_w * (1.0 / jnp.where(l_w > 0, l_w, 1.0))

        def gate(branch):
            return jnp.concatenate(
                [gt_ref[:, (g * rr + r) * N_BRANCH + branch:(g * rr + r) * N_BRANCH + branch + 1] for r in range(rr)],
                axis=0)

        o = gate(0) * o_c + gate(1) * o_s + gate(2) * o_w
        for r in range(rr):
            o_ref[:, (g * rr + r) * hd:(g * rr + r + 1) * hd] = o[r * qb:(r + 1) * qb].astype(BF16)


def nsa_prompt(q_bf16, gates_f32, kc, vc, kv_bf16):
    b, t, _ = q_bf16.shape
    qw = N_HEADS * LANES
    n_cp = kc.shape[1]
    nb = t // SEL_BLOCK
    assert t % SEL_KEY_TILE == 0 and t % Q_BLOCK == 0
    wlen = min(WINDOW + Q_BLOCK, t)
    kvw = N_KV_HEADS * LANES
    mapt = np.zeros((nb, n_cp), np.float32)
    mapt[:, :n_cp - 1] = _cmp_to_sel_map(n_cp - 1, nb).T
    cpos = _position_columns(np.arange(n_cp) * CMP_STRIDE + (CMP_BLOCK - 1), onehot=False)
    spos = _position_columns(np.arange(t), onehot=True)
    wpos = _position_columns(np.arange(t), onehot=False)
    slope = np.repeat(_slope_columns().reshape(N_KV_HEADS, HEADS_PER_GROUP, 1, LANES), Q_BLOCK, axis=2)
    slope = slope.reshape(N_KV_HEADS, HEADS_PER_GROUP * Q_BLOCK, LANES)
    kern = functools.partial(_nsa_prompt_kernel, wlen=wlen)
    per_b = lambda bi, i: (bi, 0, 0)
    const2 = lambda bi, i: (0, 0)

    def kv_cols(branch, v):
        return pl.BlockSpec((None, t, kvw), lambda bi, i: (bi, 0, 2 * branch + v))

    return pl.pallas_call(
        kern,
        out_shape=jax.ShapeDtypeStruct((b, t, qw), BF16),
        grid=(b, t // Q_BLOCK),
        in_specs=[pl.BlockSpec((None, Q_BLOCK, qw), lambda bi, i: (bi, i, 0)),
                  pl.BlockSpec((None, Q_BLOCK, LANES), lambda bi, i: (bi, i, qw // LANES)),
                  pl.BlockSpec((None, n_cp, kvw), per_b),
                  pl.BlockSpec((None, n_cp, kvw), per_b),
                  pl.BlockSpec((n_cp, LANES), const2),
                  kv_cols(1, 0), kv_cols(1, 1), kv_cols(2, 0), kv_cols(2, 1),
                  pl.BlockSpec((t, LANES), const2),
                  pl.BlockSpec((t, LANES), const2),
                  pl.BlockSpec((nb, n_cp), const2),
                  pl.BlockSpec(slope.shape, lambda bi, i: (0, 0, 0))],
        out_specs=pl.BlockSpec((None, Q_BLOCK, qw), lambda bi, i: (bi, i, 0)),
        scratch_shapes=[pltpu.SMEM((t // SEL_KEY_TILE,), jnp.int32)],
        compiler_params=_cparams(2, 56),
        name="nsa_prompt",
    )(q_bf16, gates_f32, kc, vc, jnp.asarray(cpos, dtype=BF16), kv_bf16, kv_bf16, kv_bf16, kv_bf16,
      jnp.asarray(spos, dtype=BF16), jnp.asarray(wpos, dtype=BF16), jnp.asarray(mapt), jnp.asarray(slope))


SAMPLE_PAGES = 8
TAKEN = -3e38
TN_DIMS = (((0,), (0,)), ((), ()))


def _lane_to_rows(v):
    return jnp.transpose(jnp.broadcast_to(v, (LANES, LANES)))


def _scale_rows(a, v):
    t = _lane_to_rows(v)
    return a * jnp.concatenate([t] * (a.shape[1] // LANES), axis=1)


def _sample_scores(k_bf16, pos_k, qb, slope, pos_q):
    dist = pos_q - pos_k
    return jnp.dot(k_bf16, qb, preferred_element_type=F32) - slope * dist, dist


def _nsa_sample_select_kernel(qb_ref, kc_ref, vc_ref, mapt_ref, rsum_ref, slope_ref, lq_ref, sel_ref, oc_ref,
                              *, n_prev, nb):
    lq = lq_ref[...]
    n_c = kc_ref.shape[0]
    nbp = sel_ref.shape[0]
    pos_c = (lax.broadcasted_iota(jnp.int32, (n_c, 1), 0) * CMP_STRIDE + (CMP_BLOCK - 1)).astype(F32)
    s, dist = _sample_scores(kc_ref[...], pos_c, qb_ref[...], slope_ref[...], n_prev + lq)
    p = _masked_softmax(s, dist >= 0, 0)
    oc_ref[...] = lax.dot_general(p.astype(BF16), vc_ref[...], TN_DIMS, preferred_element_type=F32)
    p_r = jnp.dot(p, rsum_ref[...], precision=lax.Precision.HIGHEST, preferred_element_type=F32)
    imp = jnp.dot(mapt_ref[...], p_r, precision=lax.Precision.HIGHEST, preferred_element_type=F32)
    blk = lax.broadcasted_iota(jnp.int32, (nbp, 1), 0)
    posq_i = n_prev + lq.astype(jnp.int32)
    cur = posq_i // SEL_BLOCK
    forced = (blk == 0) | (blk == cur) | (blk == cur - 1)
    valid = blk * SEL_BLOCK <= posq_i
    imp = jnp.where(valid, jnp.where(forced, imp + FORCE_BONUS, imp), NEG_INF)
    blk_f = blk.astype(F32)

    def pick(_, carry):
        work, chosen = carry
        top = jnp.max(work, axis=0, keepdims=True)
        first = jnp.min(jnp.where(work == top, blk_f, float(nbp)), axis=0, keepdims=True)
        hit = blk_f == first
        return jnp.where(hit, TAKEN, work), jnp.where(hit, 1.0, chosen)

    _, chosen = lax.fori_loop(0, min(N_SEL, nb), pick, (imp, jnp.zeros((nbp, LANES), F32)))
    sel_ref[...] = chosen


def _nsa_sample_kernel(ph_ref, lg_ref, nu_ref, qb_ref, gt_ref, sel3_ref, selcur_ref, oc_ref, slope_ref, lq_ref,
                       selnew_ref, win_ref, *refs, n_pages, n_prev, q_len, win_start, npg):
    del ph_ref
    pages = refs[:n_pages]
    o_ref, m_scr, l_scr, acc_scr = refs[n_pages:]
    bi = pl.program_id(0)
    j = pl.program_id(1)
    hd = LANES
    kvw = N_KV_HEADS * hd
    n_used = nu_ref[bi]
    qb = qb_ref[...]
    slope = slope_ref[...]
    lq = lq_ref[...]
    pos_q = n_prev + lq

    def scores(k_bf16, pos_k):
        return _sample_scores(k_bf16, pos_k, qb, slope, pos_q)

    def online(s, mask, v_bf16):
        s = jnp.where(mask, s, NEG_INF)
        m_old = m_scr[...]
        m_new = jnp.maximum(m_old, jnp.max(s, axis=0, keepdims=True))
        alpha = jnp.exp(m_old - m_new)
        p = jnp.where(mask, jnp.exp(s - m_new), 0.0)
        l_scr[...] = alpha * l_scr[...] + jnp.sum(p, axis=0, keepdims=True)
        m_scr[...] = m_new
        pv = lax.dot_general(p.astype(BF16), v_bf16, TN_DIMS, preferred_element_type=F32)
        acc_scr[...] = _scale_rows(acc_scr[...], alpha) + pv

    @pl.when(j == 0)
    def _():
        m_scr[...] = jnp.full(m_scr.shape, NEG_INF, F32)
        l_scr[...] = jnp.zeros(l_scr.shape, F32)
        acc_scr[...] = jnp.zeros(acc_scr.shape, F32)

    @pl.when(j * n_pages < n_used)
    def _():
        heads = 2 * N_KV_HEADS

        def head_rows(h):
            return jnp.concatenate([pg[pl.ds(h, PAGE_SIZE, stride=heads), :] for pg in pages], axis=0)

        k_all = jnp.concatenate([head_rows(g) for g in range(N_KV_HEADS)], axis=1).astype(BF16)
        v_all = jnp.concatenate([head_rows(N_KV_HEADS + g) for g in range(N_KV_HEADS)], axis=1).astype(BF16)
        pos_parts, sel_parts = [], []
        for k in range(n_pages):
            slot = j * n_pages + k
            page = lg_ref[bi * npg + jnp.minimum(slot, n_used - 1)]
            live = jnp.where(slot < n_used, 1.0, 0.0)
            pos_parts.append((page * PAGE_SIZE + lax.broadcasted_iota(jnp.int32, (PAGE_SIZE, 1), 0)).astype(F32))
            blocks = sel3_ref[page] * live
            sel_parts.append(jnp.broadcast_to(blocks[:, None, :], (blocks.shape[0], SEL_BLOCK, LANES))
                             .reshape(PAGE_SIZE, LANES))
        s, dist = scores(k_all, jnp.concatenate(pos_parts, axis=0))
        online(s, (dist >= 0) & (jnp.concatenate(sel_parts, axis=0) > 0.5), v_all)

    @pl.when(j == pl.num_programs(1) - 1)
    def _():
        n_new = selnew_ref.shape[0]
        pos_n = (n_prev + lax.broadcasted_iota(jnp.int32, (n_new, 1), 0)).astype(F32)
        s, dist = scores(selnew_ref[:, 0:kvw].astype(BF16), pos_n)
        selrow = selcur_ref[...]
        online(s, (dist >= 0) & (selrow > 0.5), selnew_ref[:, kvw:2 * kvw].astype(BF16))
        l_s = l_scr[...]
        o_s = _scale_rows(acc_scr[...], 1.0 / jnp.where(l_s > 0, l_s, 1.0))

        n_w = win_ref.shape[0]
        pos_w = (win_start + lax.broadcasted_iota(jnp.int32, (n_w, 1), 0)).astype(F32)
        s, dist = scores(win_ref[:, 0:kvw].astype(BF16), pos_w)
        p = _masked_softmax(s, (dist >= 0) & (dist < WINDOW), 0)
        o_w = lax.dot_general(p.astype(BF16), win_ref[:, kvw:2 * kvw].astype(BF16), TN_DIMS,
                              preferred_element_type=F32)
        o_c = oc_ref[...]

        rr = HEADS_PER_GROUP
        for g in range(N_KV_HEADS):
            for r in range(rr):
                h = g * rr + r
                rows = slice((g * rr + r) * q_len, (g * rr + r + 1) * q_len)
                cols = slice(g * hd, (g + 1) * hd)
                gc = gt_ref[:, h * N_BRANCH + 0:h * N_BRANCH + 1]
                gs = gt_ref[:, h * N_BRANCH + 1:h * N_BRANCH + 2]
                gw = gt_ref[:, h * N_BRANCH + 2:h * N_BRANCH + 3]
                o_ref[:, h * hd:(h + 1) * hd] = gc * o_c[rows, cols] + gs * o_s[rows, cols] + gw * o_w[rows, cols]


def nsa_sample(q_f32, gates_f32, kc, vc, sel_pages, page_table, sel_new, win_full, *, n_prev):
    b, q_len, qw = q_f32.shape
    hd = LANES
    rr = HEADS_PER_GROUP
    g_n = N_KV_HEADS
    assert g_n * rr * q_len == LANES and n_prev % SEL_BLOCK == 0 and q_len <= SEL_BLOCK
    npg = page_table.shape[1]
    n_p = min(SAMPLE_PAGES, npg)
    assert npg % n_p == 0 and (n_p * PAGE_SIZE // SEL_BLOCK) % SUBLANES == 0
    n_ch = kc.shape[1]
    nb = -(-(n_prev + q_len) // SEL_BLOCK)
    nbp = -(-nb // SUBLANES) * SUBLANES
    width = sel_new.shape[2]
    page_rows = sel_pages.shape[1]
    assert page_rows == PAGE_SIZE * 2 * g_n and sel_pages.shape[2] == hd

    qt = q_f32.reshape(b, q_len, g_n, rr, hd).transpose(0, 2, 4, 3, 1).reshape(b, g_n, hd, rr * q_len)
    qblk = jnp.einsum('bgdl,gh->bgdhl', qt, jnp.eye(g_n, dtype=F32)).reshape(b, g_n * hd, LANES).astype(BF16)

    lane = np.arange(LANES)
    lane_g, lane_r, lane_q = lane // (rr * q_len), (lane // q_len) % rr, lane % q_len
    slope = np.array([_alibi_slope(h) for h in lane_g * rr + lane_r], np.float32).reshape(1, LANES)
    lq = lane_q.astype(np.float32).reshape(1, LANES)
    rsum = ((lane_g[:, None] == lane_g[None, :]) & (lane_q[:, None] == lane_q[None, :])).astype(np.float32)
    mapt = np.zeros((nbp, n_ch), np.float32)
    mapt[:nb, :n_ch - 1] = _cmp_to_sel_map(n_ch - 1, nb).T

    pad16 = lambda a: jnp.pad(a, ((0, 0), (0, (-a.shape[1]) % BF16_ROWS), (0, 0)))
    sel_new_p = pad16(sel_new)
    win_p = pad16(win_full)

    per_b1 = lambda bi: (bi, 0, 0)
    const1 = lambda bi: (0, 0)
    sel, o_c = pl.pallas_call(
        functools.partial(_nsa_sample_select_kernel, n_prev=n_prev, nb=nb),
        out_shape=(jax.ShapeDtypeStruct((b, nbp, LANES), F32), jax.ShapeDtypeStruct((b, LANES, g_n * hd), F32)),
        grid=(b,),
        in_specs=[pl.BlockSpec((None, g_n * hd, LANES), per_b1),
                  pl.BlockSpec((None, n_ch, g_n * hd), per_b1),
                  pl.BlockSpec((None, n_ch, g_n * hd), per_b1),
                  pl.BlockSpec((nbp, n_ch), const1),
                  pl.BlockSpec((LANES, LANES), const1),
                  pl.BlockSpec((1, LANES), const1),
                  pl.BlockSpec((1, LANES), const1)],
        out_specs=(pl.BlockSpec((None, nbp, LANES), per_b1), pl.BlockSpec((None, LANES, g_n * hd), per_b1)),
        compiler_params=_cparams(1, 48),
        name="nsa_sample_select",
    )(qblk, kc, vc, jnp.asarray(mapt), jnp.asarray(rsum), jnp.asarray(slope), jnp.asarray(lq))

    blocks_per_page = PAGE_SIZE // SEL_BLOCK
    sel_pages_mask = sel[:, :npg * blocks_per_page].reshape(b, npg, blocks_per_page, LANES)
    page_used = jnp.max(sel_pages_mask, axis=(2, 3)) > 0
    order = jnp.argsort(jnp.logical_not(page_used), axis=1, stable=True).astype(jnp.int32)
    n_used = jnp.maximum(jnp.sum(page_used, axis=1), 1).astype(jnp.int32)
    phys = jnp.take_along_axis(page_table, order, axis=1)
    cur_blk = n_prev // SEL_BLOCK
    sel_cur = sel[:, cur_blk:cur_blk + 1, :]

    def page_spec(k):
        return pl.BlockSpec(
            (None, page_rows, hd),
            lambda bi, j, ph, lg, nu: (ph[bi * npg + jnp.minimum(j * n_p + k, nu[bi] - 1)], 0, 0))

    const2 = lambda bi, j, ph, lg, nu: (0, 0)
    per_b = lambda bi, j, ph, lg, nu: (bi, 0, 0)
    grid_spec = pltpu.PrefetchScalarGridSpec(
        num_scalar_prefetch=3,
        grid=(b, npg // n_p),
        in_specs=[pl.BlockSpec((None, g_n * hd, LANES), per_b),
                  pl.BlockSpec((None, q_len, LANES), per_b),
                  pl.BlockSpec((None, npg, blocks_per_page, LANES), lambda bi, j, ph, lg, nu: (bi, 0, 0, 0)),
                  pl.BlockSpec((None, 1, LANES), per_b),
                  pl.BlockSpec((None, LANES, g_n * hd), per_b),
                  pl.BlockSpec((1, LANES), const2),
                  pl.BlockSpec((1, LANES), const2),
                  pl.BlockSpec((None, sel_new_p.shape[1], width), per_b),
                  pl.BlockSpec((None, win_p.shape[1], width), per_b)]
        + [page_spec(k) for k in range(n_p)],
        out_specs=pl.BlockSpec((None, q_len, qw), per_b),
        scratch_shapes=[pltpu.VMEM((1, LANES), F32), pltpu.VMEM((1, LANES), F32),
                        pltpu.VMEM((LANES, g_n * hd), F32)],
    )
    kern = functools.partial(_nsa_sample_kernel, n_pages=n_p, n_prev=n_prev, q_len=q_len,
                             win_start=n_prev + q_len - win_full.shape[1], npg=npg)
    return pl.pallas_call(
        kern,
        out_shape=jax.ShapeDtypeStruct((b, q_len, qw), F32),
        grid_spec=grid_spec,
        compiler_params=_cparams(2, 48),
        name="nsa_sample",
    )(phys.reshape(-1), order.reshape(-1), n_used, qblk, gates_f32, sel_pages_mask, sel_cur, o_c,
      jnp.asarray(slope), jnp.asarray(lq), sel_new_p, win_p, *([sel_pages] * n_p))


PROMPT_TILE = 512
KV_PROJ_TILE = 256


def _trunk(x, params, *, n_prev, pool_prefix, conv_prefix, cmp_pages, cmp_table, sel_pages, sel_table, win_past,
           seq_tile, seqs_per_tile):
    p = params
    b, t, d = x.shape
    depth = p['ln_mix'].shape[0]
    n_a = len(p['w_pool'])
    hd = LANES
    kvw = N_KV_HEADS * hd
    row_w = 2 * kvw
    is_prompt = cmp_pages is None
    no_halo = jnp.zeros((b, max(POOL_HALO, FFN_HALO), d), F32)
    pool_rows, conv_rows = [], []
    for l in range(depth):
        if l < n_a:
            x, rows = pool_layer(x, x if is_prompt else no_halo, pool_prefix[l], p['ln_mix'][l], p['w_pool'][l], p['pool_scale'][l],
                                 n_prev=n_prev, tm=seq_tile)
            pool_rows.append(rows)
        else:
            xm = x.reshape(b * t, d)
            if l == n_a:
                raw = [None] * N_KV_HEADS
                chunk_gain = 2 * raw + [0] * N_KV_HEADS + raw + [1] * N_KV_HEADS + raw
                kv_f, kv_b = norm_linear(xm, p['ln_kv'], p['w_kv'], p['k_norm'][1:N_BRANCH], chunk_gain,
                                         tm=KV_PROJ_TILE)
                cmp_f, sel_f, win_f = (kv_f[:, br * row_w:(br + 1) * row_w] for br in range(N_BRANCH))
                if is_prompt:
                    cmp_pages = cmp_f.reshape(b * t // PAGE_SIZE, PAGE_SIZE * row_w // hd, hd)
                    cmp_table = jnp.arange(b * t // PAGE_SIZE, dtype=jnp.int32).reshape(b, t // PAGE_SIZE)
                    win_full = win_f.reshape(b, t, row_w)
                else:
                    win_full = jnp.concatenate([win_past, win_f.reshape(b, t, row_w)], axis=1)
                pab = compress_partial(cmp_pages, cmp_table, p['w1cat'])
                kc, vc = compress_finish(pab, p['cmp_pos'], p['w1flat'], p['cmp_w2'], p['k_norm'][0])
            j = l - n_a
            q_f, q_b = norm_linear(xm, p['ln_mix'][l], p['w_qg'][j], p['q_norm'][j:j + 1],
                                   [0] * N_HEADS + [None] * (p['w_qg'][j].shape[1] // hd - N_HEADS),
                                   norm_scale=hd ** -0.5, sigmoid_from=N_HEADS)
            qw = N_HEADS * hd
            if is_prompt:
                o = nsa_prompt(q_b.reshape(b, t, -1), q_f.reshape(b, t, -1), kc, vc, kv_b.reshape(b, t, -1))
            else:
                q3 = q_f.reshape(b, t, -1)
                o = nsa_sample(q3[:, :, :qw], q3[:, :, qw:], kc, vc, sel_pages, sel_table,
                               sel_f.reshape(b, t, row_w), win_full, n_prev=n_prev).astype(BF16)
            x = linear_residual(o.reshape(b * t, qw), p['w_o'][j], xm).reshape(b, t, d)
        x, rows = ffn_layer(x, x if is_prompt else no_halo, conv_prefix[l], p['ln_ffn'][l], p['w_in'], p['conv_w'][l],
                            p['conv_b'][l], p['w_out'], layer=l, nb=seqs_per_tile, tm=seq_tile)
        conv_rows.append(rows[:, FFN_HALO - (CONV_WIDTH - 1):])
    row_shape = (2, N_KV_HEADS, hd)
    n_win = min(WINDOW, n_prev + t)
    win_state = win_full[:, win_full.shape[1] - n_win:].reshape((b, n_win) + row_shape)
    return (x, cmp_f.reshape((b, t) + row_shape), sel_f.reshape((b, t) + row_shape), win_state,
            jnp.stack(pool_rows), jnp.stack(conv_rows))


def kernel(x_prompt, x_sample, cache_kv_cmp, cache_kv_sel, page_table, state_kv_win, state_pool, state_conv, ln_mix, ln_ffn, w_pool, pool_scale, ln_kv, w_kv, k_norm, cmp_w1, cmp_pos, cmp_w2, w_qg, q_norm, w_o, w_in, conv_w, conv_b, w_out):
    depth = ln_mix.shape[0]
    n_a = w_pool.shape[0]
    d = x_prompt.shape[2]
    ff2 = w_in.shape[2]
    hd = LANES
    row_w = 2 * N_KV_HEADS * hd
    qg_pad = (-w_qg.shape[2]) % LANES
    half = CMP_BLOCK // 2
    params = dict(
        ln_mix=ln_mix, ln_ffn=ln_ffn, pool_scale=pool_scale, ln_kv=ln_kv, k_norm=k_norm, q_norm=q_norm,
        conv_w=conv_w, conv_b=conv_b, cmp_pos=cmp_pos,
        w_pool=[w_pool[l].astype(BF16) for l in range(n_a)], w_kv=w_kv.astype(BF16),
        w_qg=[jnp.pad(w_qg[j], ((0, 0), (0, qg_pad))).astype(BF16) for j in range(depth - n_a)],
        w_o=[w_o[j].astype(BF16) for j in range(depth - n_a)],
        w_in=w_in.astype(BF16), w_out=w_out.astype(BF16),
        cmp_w2=cmp_w2.astype(BF16),
        w1cat=jnp.concatenate([cmp_w1[:, :half], cmp_w1[:, half:]], axis=-1).astype(BF16),
        w1flat=cmp_w1.reshape(2, CMP_BLOCK * hd, cmp_w1.shape[3]).astype(BF16),
    )

    b_p, t_p, _ = x_prompt.shape
    prompt = _trunk(
        x_prompt, params, n_prev=0,
        pool_prefix=jnp.zeros((n_a, b_p, POOL_HALO, d), F32),
        conv_prefix=jnp.zeros((depth, b_p, FFN_HALO, ff2), F32),
        cmp_pages=None, cmp_table=None, sel_pages=None, sel_table=None, win_past=None,
        seq_tile=min(PROMPT_TILE, t_p), seqs_per_tile=1)

    b_s, t_s, _ = x_sample.shape
    n_prev = page_table.shape[1] * PAGE_SIZE
    assert t_s < CMP_STRIDE and cache_kv_cmp.shape[1] == PAGE_SIZE
    sample = _trunk(
        x_sample, params, n_prev=n_prev,
        pool_prefix=jnp.pad(state_pool, ((0, 0), (0, 0), (POOL_HALO - POOL_KEEP, 0), (0, 0))),
        conv_prefix=jnp.pad(state_conv, ((0, 0), (0, 0), (FFN_HALO - (CONV_WIDTH - 1), 0), (0, 0))),
        cmp_pages=cache_kv_cmp.reshape(cache_kv_cmp.shape[0], PAGE_SIZE * row_w // hd, hd), cmp_table=page_table,
        sel_pages=cache_kv_sel.reshape(cache_kv_sel.shape[0], PAGE_SIZE * row_w // hd, hd), sel_table=page_table,
        win_past=state_kv_win.reshape(b_s, state_kv_win.shape[1], row_w),
        seq_tile=t_s, seqs_per_tile=b_s)

    return tuple(leaf for pair in zip(prompt, sample) for leaf in pair)
```

```python
import functools

import numpy as np
import jax
import jax.numpy as jnp
from jax import lax
from jax.experimental import pallas as pl
from jax.experimental.pallas import tpu as pltpu

F32 = jnp.float32
BF16 = jnp.bfloat16

POOL_WINDOWS = (2, 4, 8, 16)
POOL_KEEP = max(POOL_WINDOWS) - 1
N_HEADS = 16
N_KV_HEADS = 4
HEADS_PER_GROUP = N_HEADS // N_KV_HEADS
N_BRANCH = 3
CMP_BLOCK = 32
CMP_STRIDE = 16
SEL_BLOCK = 64
N_SEL = 16
WINDOW = 512
FORCE_BONUS = 1e4
CONV_WIDTH = 3
Q_BLOCK = 128
PAGE_SIZE = 128
EPS = 1e-6
NEG_INF = -1e30

LANES = 128
SUBLANES = 8
BF16_ROWS = 16
MIB = 2 ** 20


def _cparams(n_axes, vmem_mib):
    return pltpu.CompilerParams(dimension_semantics=("arbitrary",) * n_axes,
                                vmem_limit_bytes=vmem_mib * MIB)


def _rms(x, g):
    return (x * lax.rsqrt(jnp.mean(x * x, axis=-1, keepdims=True) + EPS)) * g


def _alibi_slope(h):
    return float(np.float32(2.0 ** (-8.0 * (h + 1) / N_HEADS)))


def _norm_linear_kernel(x_ref, g_ref, w_ref, hg_ref, *out_refs, chunk_gain, norm_scale, sigmoid_from):
    of_refs, ob_ref = out_refs[:-1], out_refs[-1]
    per_out = of_refs[0].shape[1] // LANES
    h = _rms(x_ref[...], g_ref[...]).astype(BF16)
    y = jnp.dot(h, w_ref[...], preferred_element_type=F32)
    for c in range(y.shape[1] // LANES):
        cols = slice(c * LANES, (c + 1) * LANES)
        yc = y[:, cols]
        if chunk_gain[c] is not None:
            yc = _rms(yc, hg_ref[chunk_gain[c]:chunk_gain[c] + 1, :])
            if norm_scale is not None:
                yc = yc * norm_scale
        elif sigmoid_from is not None and c >= sigmoid_from:
            yc = jax.nn.sigmoid(yc)
        of_refs[c // per_out][:, (c % per_out) * LANES:(c % per_out + 1) * LANES] = yc
        ob_ref[:, cols] = yc.astype(BF16)


def norm_linear(x, gain, w_bf16, head_gains, chunk_gain, *, norm_scale=None, sigmoid_from=None, n_f32=1, tm=512):
    m, k = x.shape
    n = w_bf16.shape[1]
    tm = min(tm, m)
    assert m % tm == 0 and n % (n_f32 * LANES) == 0 and len(chunk_gain) == n // LANES
    kern = functools.partial(_norm_linear_kernel, chunk_gain=tuple(chunk_gain), norm_scale=norm_scale,
                             sigmoid_from=sigmoid_from)
    row_tile = lambda width: pl.BlockSpec((tm, width), lambda i: (i, 0))
    return pl.pallas_call(
        kern,
        out_shape=(jax.ShapeDtypeStruct((m, n // n_f32), F32),) * n_f32 + (jax.ShapeDtypeStruct((m, n), BF16),),
        grid=(m // tm,),
        in_specs=[row_tile(k),
                  pl.BlockSpec((1, k), lambda i: (0, 0)),
                  pl.BlockSpec((k, n), lambda i: (0, 0)),
                  pl.BlockSpec(head_gains.shape, lambda i: (0, 0))],
        out_specs=(row_tile(n // n_f32),) * n_f32 + (row_tile(n),),
        compiler_params=_cparams(1, 56),
        name="norm_linear",
    )(x, gain.reshape(1, k), w_bf16, head_gains)


def _linear_residual_kernel(a_ref, w_ref, r_ref, o_ref):
    o_ref[...] = r_ref[...] + jnp.dot(a_ref[...], w_ref[...], preferred_element_type=F32)


def linear_residual(a_bf16, w_bf16, res, *, tm=512):
    m, k = a_bf16.shape
    n = w_bf16.shape[1]
    tm = min(tm, m)
    assert m % tm == 0
    return pl.pallas_call(
        _linear_residual_kernel,
        out_shape=jax.ShapeDtypeStruct((m, n), F32),
        grid=(m // tm,),
        in_specs=[pl.BlockSpec((tm, k), lambda i: (i, 0)),
                  pl.BlockSpec((k, n), lambda i: (0, 0)),
                  pl.BlockSpec((tm, n), lambda i: (i, 0))],
        out_specs=pl.BlockSpec((tm, n), lambda i: (i, 0)),
        compiler_params=_cparams(1, 48),
        name="linear_residual",
    )(a_bf16, w_bf16, res)


POOL_HALO = 16


def _pool_kernel(x_ref, xh_ref, pre_ref, g_ref, w_ref, sc_ref, o_ref, st_ref, ctx_scr, *, tm, n_prev):
    i = pl.program_id(1)
    x = x_ref[...]
    h = _rms(x, g_ref[...])
    halo = _rms(xh_ref[...], g_ref[...])
    ctx_scr[0:POOL_HALO, :] = jnp.where(i == 0, pre_ref[...], halo)
    ctx_scr[POOL_HALO:POOL_HALO + tm, :] = h
    t = i * tm + lax.broadcasted_iota(jnp.int32, (tm, 1), 0)
    gw = x.shape[1] // len(POOL_WINDOWS)
    for gi, w in enumerate(POOL_WINDOWS):
        cols = slice(gi * gw, (gi + 1) * gw)
        hs = h[:, cols]
        acc = hs
        for k in range(1, w):
            acc = acc + ctx_scr[POOL_HALO - k:POOL_HALO - k + tm, cols]
        cnt = jnp.minimum(w, n_prev + t + 1).astype(F32)
        pooled = acc / cnt - hs
        mixed = jnp.dot(pooled.astype(BF16), w_ref[gi], preferred_element_type=F32)
        o_ref[:, cols] = x[:, cols] + mixed * sc_ref[:, cols]
    st_ref[...] = ctx_scr[tm + POOL_HALO - POOL_KEEP:tm + POOL_HALO, :]


def pool_layer(x, x_halo_src, prefix16, gain, w_pool_bf16, scale, *, n_prev, tm):
    b, t, d = x.shape
    assert t % tm == 0 and tm % SUBLANES == 0
    hb = tm // POOL_HALO
    kern = functools.partial(_pool_kernel, tm=tm, n_prev=n_prev)
    return pl.pallas_call(
        kern,
        out_shape=(jax.ShapeDtypeStruct((b, t, d), F32), jax.ShapeDtypeStruct((b, POOL_KEEP, d), F32)),
        grid=(b, t // tm),
        in_specs=[pl.BlockSpec((None, tm, d), lambda bi, i: (bi, i, 0)),
                  pl.BlockSpec((None, POOL_HALO, d), lambda bi, i: (bi, jnp.maximum(i * hb - 1, 0), 0)),
                  pl.BlockSpec((None, POOL_HALO, d), lambda bi, i: (bi, 0, 0)),
                  pl.BlockSpec((1, d), lambda bi, i: (0, 0)),
                  pl.BlockSpec(w_pool_bf16.shape, lambda bi, i: (0, 0, 0)),
                  pl.BlockSpec((1, d), lambda bi, i: (0, 0))],
        out_specs=(pl.BlockSpec((None, tm, d), lambda bi, i: (bi, i, 0)),
                   pl.BlockSpec((None, POOL_KEEP, d), lambda bi, i: (bi, 0, 0))),
        scratch_shapes=[pltpu.VMEM((tm + POOL_HALO, d), F32)],
        compiler_params=_cparams(2, 48),
        name="pool_layer",
    )(x, x_halo_src, prefix16, gain.reshape(1, d), w_pool_bf16, scale.reshape(1, d))


FFN_HALO = 16
FFN_SPLIT = 2


def _ffn_kernel(x_ref, xh_ref, pg_ref, pv_ref, g_ref, wig_ref, wiv_ref, cwg_ref, cwv_ref, cbg_ref, cbv_ref, wo_ref,
                o_ref, sg_ref, sv_ref, u_scr, acc_scr, *h_scrs, nb, tm, te):
    hg_scr, hv_scr = h_scrs[:FFN_SPLIT], h_scrs[FFN_SPLIT:]
    i = pl.program_id(1)
    f = pl.program_id(2)
    n_ext = nb * te
    n_out = n_ext - FFN_HALO

    @pl.when(f == 0)
    def _():
        for s in range(nb):
            u_scr[s * te:s * te + FFN_HALO, :] = _rms(xh_ref[s], g_ref[...]).astype(BF16)
            u_scr[s * te + FFN_HALO:s * te + FFN_HALO + tm, :] = _rms(x_ref[s], g_ref[...]).astype(BF16)
            if te > FFN_HALO + tm:
                u_scr[s * te + FFN_HALO + tm:(s + 1) * te, :] = jnp.zeros((te - FFN_HALO - tm, u_scr.shape[1]), BF16)
        acc_scr[...] = jnp.zeros_like(acc_scr)

    u = u_scr[...]
    n_split = FFN_SPLIT
    pw = hg_scr[0].shape[1]
    first = i == 0

    def up_project(h_scr, w_ref, p_ref, piece):
        cols = slice(piece * pw, (piece + 1) * pw)
        h = jnp.dot(u, w_ref[:, cols], preferred_element_type=F32)
        h_scr[...] = h
        for s in range(nb):
            halo = slice(s * te, s * te + FFN_HALO)
            h_scr[halo, :] = jnp.where(first, p_ref[s, :, cols], h[halo])

    def conv(h_scr, cw_ref, cb_ref, piece):
        cols = slice(piece * pw, (piece + 1) * pw)
        c = cb_ref[:, cols] + cw_ref[0:1, cols] * h_scr[FFN_HALO - 2:FFN_HALO - 2 + n_out, :]
        c = c + cw_ref[1:2, cols] * h_scr[FFN_HALO - 1:FFN_HALO - 1 + n_out, :]
        return c + cw_ref[2:3, cols] * h_scr[FFN_HALO:FFN_HALO + n_out, :]

    acts = []
    for piece in range(n_split):
        up_project(hg_scr[piece], wig_ref, pg_ref, piece)
        up_project(hv_scr[piece], wiv_ref, pv_ref, piece)
        cg = conv(hg_scr[piece], cwg_ref, cbg_ref, piece)
        cv = conv(hv_scr[piece], cwv_ref, cbv_ref, piece)
        acts.append(((cg * jax.nn.sigmoid(cg)) * cv).astype(BF16))
    acc_scr[...] += jnp.dot(jnp.concatenate(acts, axis=1), wo_ref[...], preferred_element_type=F32)

    for s in range(nb):
        for piece in range(n_split):
            cols = slice(piece * pw, (piece + 1) * pw)
            sg_ref[s, :, cols] = hg_scr[piece][s * te + tm:s * te + tm + FFN_HALO, :]
            sv_ref[s, :, cols] = hv_scr[piece][s * te + tm:s * te + tm + FFN_HALO, :]

    @pl.when(f == pl.num_programs(2) - 1)
    def _():
        for s in range(nb):
            o_ref[s] = x_ref[s] + acc_scr[s * te:s * te + tm, :]


def ffn_layer(x, x_halo_src, prefix16, gain, w_in_bf16, conv_w, conv_b, w_out_bf16, *, layer, nb, tm, tf=512):
    b, t, d = x.shape
    ff = w_out_bf16.shape[1]
    assert t % tm == 0 and b % nb == 0 and ff % tf == 0 and tm % SUBLANES == 0
    nf = ff // tf
    hb = tm // FFN_HALO
    te = FFN_HALO + -(-tm // BF16_ROWS) * BF16_ROWS
    n_ext = nb * te
    kern = functools.partial(_ffn_kernel, nb=nb, tm=tm, te=te)
    gate_col = lambda bi, i, f: (0, f)
    val_col = lambda bi, i, f: (0, nf + f)
    y, sg, sv = pl.pallas_call(
        kern,
        out_shape=(jax.ShapeDtypeStruct((b, t, d), F32),
                   jax.ShapeDtypeStruct((b, t // tm, FFN_HALO, ff), F32),
                   jax.ShapeDtypeStruct((b, t // tm, FFN_HALO, ff), F32)),
        grid=(b // nb, t // tm, nf),
        in_specs=[pl.BlockSpec((nb, tm, d), lambda bi, i, f: (bi, i, 0)),
                  pl.BlockSpec((nb, FFN_HALO, d), lambda bi, i, f: (bi, jnp.maximum(i * hb - 1, 0), 0)),
                  pl.BlockSpec((nb, FFN_HALO, tf), lambda bi, i, f: (bi, 0, f)),
                  pl.BlockSpec((nb, FFN_HALO, tf), lambda bi, i, f: (bi, 0, nf + f)),
                  pl.BlockSpec((1, d), lambda bi, i, f: (0, 0)),
                  pl.BlockSpec((None, d, tf), lambda bi, i, f: (layer, 0, f)),
                  pl.BlockSpec((None, d, tf), lambda bi, i, f: (layer, 0, nf + f)),
                  pl.BlockSpec((CONV_WIDTH, tf), gate_col),
                  pl.BlockSpec((CONV_WIDTH, tf), val_col),
                  pl.BlockSpec((1, tf), gate_col),
                  pl.BlockSpec((1, tf), val_col),
                  pl.BlockSpec((None, tf, d), lambda bi, i, f: (layer, f, 0))],
        out_specs=(pl.BlockSpec((nb, tm, d), lambda bi, i, f: (bi, i, 0)),
                   pl.BlockSpec((nb, None, FFN_HALO, tf), lambda bi, i, f: (bi, i, 0, f)),
                   pl.BlockSpec((nb, None, FFN_HALO, tf), lambda bi, i, f: (bi, i, 0, f))),
        scratch_shapes=[pltpu.VMEM((n_ext, d), BF16), pltpu.VMEM((n_ext - FFN_HALO, d), F32)]
        + [pltpu.VMEM((n_ext, tf // FFN_SPLIT), F32)] * (2 * FFN_SPLIT),
        compiler_params=_cparams(3, 56),
        name="ffn_layer",
    )(x, x_halo_src, prefix16, prefix16, gain.reshape(1, d), w_in_bf16, w_in_bf16, conv_w, conv_w,
      conv_b.reshape(1, 2 * ff), conv_b.reshape(1, 2 * ff), w_out_bf16)
    return y, jnp.concatenate([sg[:, -1], sv[:, -1]], axis=-1)


CMP_PAGES = 16
CHUNKS_PER_PAGE = PAGE_SIZE // CMP_STRIDE
CMP_PAIR = 2


def _compress_kernel(pt_ref, *refs, n_pages):
    del pt_ref
    page_refs = refs[:n_pages]
    w_ref, o_ref = refs[n_pages], refs[n_pages + 1]
    heads = 2 * N_KV_HEADS
    k_rows = lax.broadcasted_iota(jnp.int32, (heads, LANES), 0) < N_KV_HEADS
    def head_tiles(s):
        lhs = [[], []]
        for k in range(n_pages):
            for n in range(0, CHUNKS_PER_PAGE, 2):
                t_e = page_refs[k][(CMP_STRIDE * n + s) * heads:(CMP_STRIDE * n + s + 1) * heads, :]
                t_o = page_refs[k][(CMP_STRIDE * (n + 1) + s) * heads:(CMP_STRIDE * (n + 1) + s + 1) * heads, :]
                lhs[0].append(jnp.where(k_rows, t_e, pltpu.roll(t_o, N_KV_HEADS, 0)))
                lhs[1].append(jnp.where(k_rows, pltpu.roll(t_e, N_KV_HEADS, 0), t_o))
        return [jnp.concatenate(rows, axis=0).astype(BF16) for rows in lhs]

    acc = [None, None]
    for s2 in range(CMP_STRIDE // CMP_PAIR):
        tiles = [head_tiles(CMP_PAIR * s2 + i) for i in range(CMP_PAIR)]
        for v in range(2):
            a = jnp.concatenate([t[v] for t in tiles], axis=1)
            d = jnp.dot(a, w_ref[v, s2], preferred_element_type=F32)
            acc[v] = d if acc[v] is None else acc[v] + d
    for v in range(2):
        o_ref[v] = acc[v]


def compress_partial(pages, page_table, w1cat_bf16):
    s, npg = page_table.shape
    page_rows, hd = pages.shape[1:]
    assert page_rows == PAGE_SIZE * 2 * N_KV_HEADS
    hid2 = w1cat_bf16.shape[3]
    n_p = min(CMP_PAGES, npg)
    assert npg % n_p == 0
    n_ch = npg * CHUNKS_PER_PAGE

    def page_spec(k):
        return pl.BlockSpec((None, page_rows, hd), lambda si, j, pt: (pt[si * npg + j * n_p + k], 0, 0))

    grid_spec = pltpu.PrefetchScalarGridSpec(
        num_scalar_prefetch=1,
        grid=(s, npg // n_p),
        in_specs=[page_spec(k) for k in range(n_p)]
        + [pl.BlockSpec(w1cat_bf16.shape, lambda si, j, pt: (0, 0, 0, 0))],
        out_specs=pl.BlockSpec((None, 2, n_p * CHUNKS_PER_PAGE * N_KV_HEADS, hid2), lambda si, j, pt: (si, 0, j, 0)),
    )
    return pl.pallas_call(
        functools.partial(_compress_kernel, n_pages=n_p),
        out_shape=jax.ShapeDtypeStruct((s, 2, n_ch * N_KV_HEADS, hid2), F32),
        grid_spec=grid_spec,
        compiler_params=_cparams(2, 48),
        name="compress_partial",
    )(page_table.reshape(-1), *([pages] * n_p), w1cat_bf16)


def _compress_finish_kernel(pab_ref, pos_ref, w1_ref, w2_ref, kn_ref, kc_ref, vc_ref):
    n = pab_ref.shape[1]
    hid = w2_ref.shape[1]
    for v in range(2):
        blk = pab_ref[v]
        nxt = pltpu.roll(blk[:, hid:], n - N_KV_HEADS, 0)
        pos = jnp.broadcast_to(pos_ref[v], (SUBLANES, pos_ref.shape[2])).astype(BF16)
        bias = jnp.dot(pos, w1_ref[v], preferred_element_type=F32)[0:1, :]
        hdn = jax.nn.gelu(blk[:, :hid] + nxt + bias)
        out = jnp.dot(hdn.astype(BF16), w2_ref[v], preferred_element_type=F32)
        if v == 0:
            kc_ref[...] = _rms(out, kn_ref[...]).astype(BF16)
        else:
            vc_ref[...] = out.astype(BF16)


def compress_finish(pab, cmp_pos, w1flat_bf16, w2_bf16, k_gain):
    s, _, rows, hid2 = pab.shape
    hd = w2_bf16.shape[2]
    kdim = w1flat_bf16.shape[1]
    out = jax.ShapeDtypeStruct((s, rows, hd), BF16)
    kc, vc = pl.pallas_call(
        _compress_finish_kernel,
        out_shape=(out, out),
        grid=(s,),
        in_specs=[pl.BlockSpec((None, 2, rows, hid2), lambda si: (si, 0, 0, 0)),
                  pl.BlockSpec((2, 1, kdim), lambda si: (0, 0, 0)),
                  pl.BlockSpec(w1flat_bf16.shape, lambda si: (0, 0, 0)),
                  pl.BlockSpec(w2_bf16.shape, lambda si: (0, 0, 0)),
                  pl.BlockSpec((1, hd), lambda si: (0, 0))],
        out_specs=(pl.BlockSpec((None, rows, hd), lambda si: (si, 0, 0)),
                   pl.BlockSpec((None, rows, hd), lambda si: (si, 0, 0))),
        compiler_params=_cparams(1, 48),
        name="compress_finish",
    )(pab, cmp_pos.reshape(2, 1, kdim), w1flat_bf16, w2_bf16, k_gain.reshape(1, hd))
    return (kc.reshape(s, rows // N_KV_HEADS, N_KV_HEADS * hd), vc.reshape(s, rows // N_KV_HEADS, N_KV_HEADS * hd))


def _cmp_to_sel_map(nc, nb):
    c_s = np.arange(nc) * CMP_STRIDE
    c_e = c_s + CMP_BLOCK - 1
    s_s = np.arange(nb) * SEL_BLOCK
    s_e = s_s + SEL_BLOCK - 1
    return ((c_s[:, None] <= s_e[None]) & (c_e[:, None] >= s_s[None])).astype(np.float32)


def _masked_softmax(s, mask, axis):
    s = jnp.where(mask, s, NEG_INF)
    m = jnp.max(s, axis=axis, keepdims=True)
    p = jnp.where(mask, jnp.exp(s - m), 0.0)
    den = jnp.sum(p, axis=axis, keepdims=True)
    return p * (1.0 / jnp.where(den > 0, den, 1.0))


SEL_KEY_TILE = 512


POS_SPLIT = 3
BLOCK_LANES = 64
MASK_BIAS = -1e30


def _position_columns(pos, onehot):
    pos = np.asarray(pos)
    assert pos.max() // SEL_BLOCK <= 256
    out = np.zeros((pos.shape[0], LANES), np.float32)
    if onehot:
        assert pos.max() // SEL_BLOCK < BLOCK_LANES
        out[np.arange(pos.shape[0]), pos // SEL_BLOCK] = 1.0
    for j in range(POS_SPLIT):
        out[:, BLOCK_LANES + j] = (pos // SEL_BLOCK) * SEL_BLOCK
        out[:, BLOCK_LANES + POS_SPLIT + j] = pos % SEL_BLOCK
    return out


def _slope_columns():
    out = np.zeros((N_HEADS, LANES), np.float32)
    for h in range(N_HEADS):
        rem = np.float32(_alibi_slope(h))
        for j in range(POS_SPLIT):
            part = np.float32(rem.astype(jnp.bfloat16))
            out[h, BLOCK_LANES + j] = part
            out[h, BLOCK_LANES + POS_SPLIT + j] = part
            rem = np.float32(rem - part)
        assert rem == 0
    return out


def _nsa_prompt_kernel(q_ref, gt_ref, kc_ref, vc_ref, cpos_ref, selk_ref, selv_ref, wink_ref, winv_ref,
                       spos_ref, wpos_ref, mapt_ref, slope_ref, o_ref, used_ref, *, wlen):
    i = pl.program_id(1)
    s0 = i * Q_BLOCK
    hd = LANES
    rr, qb, tk = HEADS_PER_GROUP, Q_BLOCK, SEL_KEY_TILE
    m_rows = rr * qb
    n_cp = kc_ref.shape[0]
    nb = mapt_ref.shape[0]
    nt_dims = (((1,), (1,)), ((), ()))
    kt_last = s0 // tk
    ws = pl.multiple_of(jnp.maximum(s0 - WINDOW, 0), Q_BLOCK)

    row = lax.broadcasted_iota(jnp.int32, (m_rows, 1), 0)
    pos_q = s0 + (row & (qb - 1))
    posq_l = s0 + lax.broadcasted_iota(jnp.int32, (1, qb), 1)
    blk = lax.broadcasted_iota(jnp.int32, (nb, 1), 0)
    cur = posq_l // SEL_BLOCK
    forced = (blk == 0) | (blk == cur) | (blk == cur - 1)
    valid = blk * SEL_BLOCK <= posq_l
    cmask = lax.broadcasted_iota(jnp.int32, (1, n_cp), 1) * CMP_STRIDE + (CMP_BLOCK - 1) <= pos_q
    dist_w = pos_q - (ws + lax.broadcasted_iota(jnp.int32, (1, wlen), 1))
    wmask = (dist_w >= 0) & (dist_w < WINDOW)

    for g in range(N_KV_HEADS):
        cols = slice(g * hd, (g + 1) * hd)
        q = jnp.concatenate([q_ref[:, (g * rr + r) * hd:(g * rr + r + 1) * hd] for r in range(rr)], axis=0)
        slope_cols = slope_ref[g]
        q_pos = jnp.concatenate([q, slope_cols.astype(BF16)], axis=1)

        kc_aug = jnp.concatenate([kc_ref[:, cols], cpos_ref[...]], axis=1)
        s = lax.dot_general(q_pos, kc_aug, nt_dims, preferred_element_type=F32)
        p = _masked_softmax(s, cmask, -1)
        o_c = jnp.dot(p.astype(BF16), vc_ref[:, cols], preferred_element_type=F32)

        p_sum = p[0:qb]
        for r in range(1, rr):
            p_sum = p_sum + p[r * qb:(r + 1) * qb]
        imp = lax.dot_general(mapt_ref[...], p_sum, nt_dims, precision=lax.Precision.HIGHEST,
                              preferred_element_type=F32)
        imp = jnp.where(valid, jnp.where(forced, imp + FORCE_BONUS, imp), NEG_INF)
        cnt = jnp.zeros((nb, qb), F32)
        for c in range(nb):
            rowc = imp[c:c + 1, :]
            beats = (rowc > imp) | ((rowc == imp) & (blk > c))
            cnt = cnt + jnp.where(beats, 1.0, 0.0)
        chosen = cnt < min(N_SEL, nb)
        bias_t = jnp.where(chosen, 0.0, MASK_BIAS)
        chosen_f = jnp.where(chosen, 1.0, 0.0)
        n_used = jnp.int32(0)
        for t in range(nb // (tk // SEL_BLOCK)):
            any_sel = jnp.max(chosen_f[t * (tk // SEL_BLOCK):(t + 1) * (tk // SEL_BLOCK), :]) > 0
            used_ref[n_used] = t
            n_used = n_used + jnp.where(any_sel & (t < kt_last), 1, 0)
        bias = jnp.transpose(jnp.concatenate([bias_t, jnp.zeros((hd - nb, qb), F32)], axis=0))
        q_sel = jnp.concatenate([q, (slope_cols + jnp.concatenate([bias] * rr, axis=0)).astype(BF16)], axis=1)

        def sel_tile(kt, carry, causal):
            m_i, l_i, acc = carry
            k0 = pl.multiple_of(kt * tk, tk)
            k_aug = jnp.concatenate([selk_ref[pl.ds(k0, tk), cols], spos_ref[pl.ds(k0, tk), :]], axis=1)
            s = lax.dot_general(q_sel, k_aug, nt_dims, preferred_element_type=F32)
            if causal:
                s = jnp.where(k0 + lax.broadcasted_iota(jnp.int32, (1, tk), 1) <= pos_q, s, NEG_INF)
            m_new = jnp.maximum(m_i, jnp.max(s, axis=-1, keepdims=True))
            alpha = jnp.exp(m_i - m_new)
            p = jnp.exp(s - m_new)
            l_new = alpha * l_i + jnp.sum(p, axis=-1, keepdims=True)
            acc = alpha * acc + jnp.dot(p.astype(BF16), selv_ref[pl.ds(k0, tk), cols], preferred_element_type=F32)
            return m_new, l_new, acc

        init = (jnp.full((m_rows, 1), NEG_INF, F32), jnp.zeros((m_rows, 1), F32), jnp.zeros((m_rows, hd), F32))

        carry = lax.fori_loop(0, n_used, lambda j, c: sel_tile(used_ref[j], c, False), init)
        _, l_s, acc_s = sel_tile(kt_last, carry, True)
        o_s = acc_s * (1.0 / jnp.where(l_s > 0, l_s, 1.0))

        kw_aug = jnp.concatenate([wink_ref[pl.ds(ws, wlen), cols], wpos_ref[pl.ds(ws, wlen), :]], axis=1)
        s = jnp.where(wmask, lax.dot_general(q_pos, kw_aug, nt_dims, preferred_element_type=F32), NEG_INF)
        p = jnp.exp(s - jnp.max(s, axis=-1, keepdims=True))
        l_w = jnp.sum(p, axis=-1, keepdims=True)
        o_w = jnp.dot(p.astype(BF16), winv_ref[pl.ds(ws, wlen), cols], preferred_element_type=F32)
        o_w = o_w * (1.0 / jnp.where(l_w > 0, l_w, 1.0))

        def gate(branch):
            return jnp.concatenate(
                [gt_ref[:, (g * rr + r) * N_BRANCH + branch:(g * rr + r) * N_BRANCH + branch + 1] for r in range(rr)],
                axis=0)

        o = gate(0) * o_c + gate(1) * o_s + gate(2) * o_w
        for r in range(rr):
            o_ref[:, (g * rr + r) * hd:(g * rr + r + 1) * hd] = o[r * qb:(r + 1) * qb].astype(BF16)


def nsa_prompt(q_bf16, gates_f32, kc, vc, kv_bf16):
    b, t, _ = q_bf16.shape
    qw = N_HEADS * LANES
    n_cp = kc.shape[1]
    nb = t // SEL_BLOCK
    assert t % SEL_KEY_TILE == 0 and t % Q_BLOCK == 0
    wlen = min(WINDOW + Q_BLOCK, t)
    kvw = N_KV_HEADS * LANES
    mapt = np.zeros((nb, n_cp), np.float32)
    mapt[:, :n_cp - 1] = _cmp_to_sel_map(n_cp - 1, nb).T
    cpos = _position_columns(np.arange(n_cp) * CMP_STRIDE + (CMP_BLOCK - 1), onehot=False)
    spos = _position_columns(np.arange(t), onehot=True)
    wpos = _position_columns(np.arange(t), onehot=False)
    slope = np.repeat(_slope_columns().reshape(N_KV_HEADS, HEADS_PER_GROUP, 1, LANES), Q_BLOCK, axis=2)
    slope = slope.reshape(N_KV_HEADS, HEADS_PER_GROUP * Q_BLOCK, LANES)
    kern = functools.partial(_nsa_prompt_kernel, wlen=wlen)
    per_b = lambda bi, i: (bi, 0, 0)
    const2 = lambda bi, i: (0, 0)

    def kv_cols(branch, v):
        return pl.BlockSpec((None, t, kvw), lambda bi, i: (bi, 0, 2 * branch + v))

    return pl.pallas_call(
        kern,
        out_shape=jax.ShapeDtypeStruct((b, t, qw), BF16),
        grid=(b, t // Q_BLOCK),
        in_specs=[pl.BlockSpec((None, Q_BLOCK, qw), lambda bi, i: (bi, i, 0)),
                  pl.BlockSpec((None, Q_BLOCK, LANES), lambda bi, i: (bi, i, qw // LANES)),
                  pl.BlockSpec((None, n_cp, kvw), per_b),
                  pl.BlockSpec((None, n_cp, kvw), per_b),
                  pl.BlockSpec((n_cp, LANES), const2),
                  kv_cols(1, 0), kv_cols(1, 1), kv_cols(2, 0), kv_cols(2, 1),
                  pl.BlockSpec((t, LANES), const2),
                  pl.BlockSpec((t, LANES), const2),
                  pl.BlockSpec((nb, n_cp), const2),
                  pl.BlockSpec(slope.shape, lambda bi, i: (0, 0, 0))],
        out_specs=pl.BlockSpec((None, Q_BLOCK, qw), lambda bi, i: (bi, i, 0)),
        scratch_shapes=[pltpu.SMEM((t // SEL_KEY_TILE,), jnp.int32)],
        compiler_params=_cparams(2, 56),
        name="nsa_prompt",
    )(q_bf16, gates_f32, kc, vc, jnp.asarray(cpos, dtype=BF16), kv_bf16, kv_bf16, kv_bf16, kv_bf16,
      jnp.asarray(spos, dtype=BF16), jnp.asarray(wpos, dtype=BF16), jnp.asarray(mapt), jnp.asarray(slope))


SAMPLE_PAGES = 8
TAKEN = -3e38
TN_DIMS = (((0,), (0,)), ((), ()))


def _lane_to_rows(v):
    return jnp.transpose(jnp.broadcast_to(v, (LANES, LANES)))


def _scale_rows(a, v):
    t = _lane_to_rows(v)
    return a * jnp.concatenate([t] * (a.shape[1] // LANES), axis=1)


def _sample_scores(k_bf16, pos_k, qb, slope, pos_q):
    dist = pos_q - pos_k
    return jnp.dot(k_bf16, qb, preferred_element_type=F32) - slope * dist, dist


def _nsa_sample_select_kernel(qb_ref, kc_ref, vc_ref, mapt_ref, rsum_ref, slope_ref, lq_ref, sel_ref, oc_ref,
                              *, n_prev, nb):
    lq = lq_ref[...]
    n_c = kc_ref.shape[0]
    nbp = sel_ref.shape[0]
    pos_c = (lax.broadcasted_iota(jnp.int32, (n_c, 1), 0) * CMP_STRIDE + (CMP_BLOCK - 1)).astype(F32)
    s, dist = _sample_scores(kc_ref[...], pos_c, qb_ref[...], slope_ref[...], n_prev + lq)
    p = _masked_softmax(s, dist >= 0, 0)
    oc_ref[...] = lax.dot_general(p.astype(BF16), vc_ref[...], TN_DIMS, preferred_element_type=F32)
    p_r = jnp.dot(p, rsum_ref[...], precision=lax.Precision.HIGHEST, preferred_element_type=F32)
    imp = jnp.dot(mapt_ref[...], p_r, precision=lax.Precision.HIGHEST, preferred_element_type=F32)
    blk = lax.broadcasted_iota(jnp.int32, (nbp, 1), 0)
    posq_i = n_prev + lq.astype(jnp.int32)
    cur = posq_i // SEL_BLOCK
    forced = (blk == 0) | (blk == cur) | (blk == cur - 1)
    valid = blk * SEL_BLOCK <= posq_i
    imp = jnp.where(valid, jnp.where(forced, imp + FORCE_BONUS, imp), NEG_INF)
    blk_f = blk.astype(F32)

    def pick(_, carry):
        work, chosen = carry
        top = jnp.max(work, axis=0, keepdims=True)
        first = jnp.min(jnp.where(work == top, blk_f, float(nbp)), axis=0, keepdims=True)
        hit = blk_f == first
        return jnp.where(hit, TAKEN, work), jnp.where(hit, 1.0, chosen)

    _, chosen = lax.fori_loop(0, min(N_SEL, nb), pick, (imp, jnp.zeros((nbp, LANES), F32)))
    sel_ref[...] = chosen


def _nsa_sample_kernel(ph_ref, lg_ref, nu_ref, qb_ref, gt_ref, sel3_ref, selcur_ref, oc_ref, slope_ref, lq_ref,
                       selnew_ref, win_ref, *refs, n_pages, n_prev, q_len, win_start, npg):
    del ph_ref
    pages = refs[:n_pages]
    o_ref, m_scr, l_scr, acc_scr = refs[n_pages:]
    bi = pl.program_id(0)
    j = pl.program_id(1)
    hd = LANES
    kvw = N_KV_HEADS * hd
    n_used = nu_ref[bi]
    qb = qb_ref[...]
    slope = slope_ref[...]
    lq = lq_ref[...]
    pos_q = n_prev + lq

    def scores(k_bf16, pos_k):
        return _sample_scores(k_bf16, pos_k, qb, slope, pos_q)

    def online(s, mask, v_bf16):
        s = jnp.where(mask, s, NEG_INF)
        m_old = m_scr[...]
        m_new = jnp.maximum(m_old, jnp.max(s, axis=0, keepdims=True))
        alpha = jnp.exp(m_old - m_new)
        p = jnp.where(mask, jnp.exp(s - m_new), 0.0)
        l_scr[...] = alpha * l_scr[...] + jnp.sum(p, axis=0, keepdims=True)
        m_scr[...] = m_new
        pv = lax.dot_general(p.astype(BF16), v_bf16, TN_DIMS, preferred_element_type=F32)
        acc_scr[...] = _scale_rows(acc_scr[...], alpha) + pv

    @pl.when(j == 0)
    def _():
        m_scr[...] = jnp.full(m_scr.shape, NEG_INF, F32)
        l_scr[...] = jnp.zeros(l_scr.shape, F32)
        acc_scr[...] = jnp.zeros(acc_scr.shape, F32)

    @pl.when(j * n_pages < n_used)
    def _():
        heads = 2 * N_KV_HEADS

        def head_rows(h):
            return jnp.concatenate([pg[pl.ds(h, PAGE_SIZE, stride=heads), :] for pg in pages], axis=0)

        k_all = jnp.concatenate([head_rows(g) for g in range(N_KV_HEADS)], axis=1).astype(BF16)
        v_all = jnp.concatenate([head_rows(N_KV_HEADS + g) for g in range(N_KV_HEADS)], axis=1).astype(BF16)
        pos_parts, sel_parts = [], []
        for k in range(n_pages):
            slot = j * n_pages + k
            page = lg_ref[bi * npg + jnp.minimum(slot, n_used - 1)]
            live = jnp.where(slot < n_used, 1.0, 0.0)
            pos_parts.append((page * PAGE_SIZE + lax.broadcasted_iota(jnp.int32, (PAGE_SIZE, 1), 0)).astype(F32))
            blocks = sel3_ref[page] * live
            sel_parts.append(jnp.broadcast_to(blocks[:, None, :], (blocks.shape[0], SEL_BLOCK, LANES))
                             .reshape(PAGE_SIZE, LANES))
        s, dist = scores(k_all, jnp.concatenate(pos_parts, axis=0))
        online(s, (dist >= 0) & (jnp.concatenate(sel_parts, axis=0) > 0.5), v_all)

    @pl.when(j == pl.num_programs(1) - 1)
    def _():
        n_new = selnew_ref.shape[0]
        pos_n = (n_prev + lax.broadcasted_iota(jnp.int32, (n_new, 1), 0)).astype(F32)
        s, dist = scores(selnew_ref[:, 0:kvw].astype(BF16), pos_n)
        selrow = selcur_ref[...]
        online(s, (dist >= 0) & (selrow > 0.5), selnew_ref[:, kvw:2 * kvw].astype(BF16))
        l_s = l_scr[...]
        o_s = _scale_rows(acc_scr[...], 1.0 / jnp.where(l_s > 0, l_s, 1.0))

        n_w = win_ref.shape[0]
        pos_w = (win_start + lax.broadcasted_iota(jnp.int32, (n_w, 1), 0)).astype(F32)
        s, dist = scores(win_ref[:, 0:kvw].astype(BF16), pos_w)
        p = _masked_softmax(s, (dist >= 0) & (dist < WINDOW), 0)
        o_w = lax.dot_general(p.astype(BF16), win_ref[:, kvw:2 * kvw].astype(BF16), TN_DIMS,
                              preferred_element_type=F32)
        o_c = oc_ref[...]

        rr = HEADS_PER_GROUP
        for g in range(N_KV_HEADS):
            for r in range(rr):
                h = g * rr + r
                rows = slice((g * rr + r) * q_len, (g * rr + r + 1) * q_len)
                cols = slice(g * hd, (g + 1) * hd)
                gc = gt_ref[:, h * N_BRANCH + 0:h * N_BRANCH + 1]
                gs = gt_ref[:, h * N_BRANCH + 1:h * N_BRANCH + 2]
                gw = gt_ref[:, h * N_BRANCH + 2:h * N_BRANCH + 3]
                o_ref[:, h * hd:(h + 1) * hd] = gc * o_c[rows, cols] + gs * o_s[rows, cols] + gw * o_w[rows, cols]


def nsa_sample(q_f32, gates_f32, kc, vc, sel_pages, page_table, sel_new, win_full, *, n_prev):
    b, q_len, qw = q_f32.shape
    hd = LANES
    rr = HEADS_PER_GROUP
    g_n = N_KV_HEADS
    assert g_n * rr * q_len == LANES and n_prev % SEL_BLOCK == 0 and q_len <= SEL_BLOCK
    npg = page_table.shape[1]
    n_p = min(SAMPLE_PAGES, npg)
    assert npg % n_p == 0 and (n_p * PAGE_SIZE // SEL_BLOCK) % SUBLANES == 0
    n_ch = kc.shape[1]
    nb = -(-(n_prev + q_len) // SEL_BLOCK)
    nbp = -(-nb // SUBLANES) * SUBLANES
    width = sel_new.shape[2]
    page_rows = sel_pages.shape[1]
    assert page_rows == PAGE_SIZE * 2 * g_n and sel_pages.shape[2] == hd

    qt = q_f32.reshape(b, q_len, g_n, rr, hd).transpose(0, 2, 4, 3, 1).reshape(b, g_n, hd, rr * q_len)
    qblk = jnp.einsum('bgdl,gh->bgdhl', qt, jnp.eye(g_n, dtype=F32)).reshape(b, g_n * hd, LANES).astype(BF16)

    lane = np.arange(LANES)
    lane_g, lane_r, lane_q = lane // (rr * q_len), (lane // q_len) % rr, lane % q_len
    slope = np.array([_alibi_slope(h) for h in lane_g * rr + lane_r], np.float32).reshape(1, LANES)
    lq = lane_q.astype(np.float32).reshape(1, LANES)
    rsum = ((lane_g[:, None] == lane_g[None, :]) & (lane_q[:, None] == lane_q[None, :])).astype(np.float32)
    mapt = np.zeros((nbp, n_ch), np.float32)
    mapt[:nb, :n_ch - 1] = _cmp_to_sel_map(n_ch - 1, nb).T

    pad16 = lambda a: jnp.pad(a, ((0, 0), (0, (-a.shape[1]) % BF16_ROWS), (0, 0)))
    sel_new_p = pad16(sel_new)
    win_p = pad16(win_full)

    per_b1 = lambda bi: (bi, 0, 0)
    const1 = lambda bi: (0, 0)
    sel, o_c = pl.pallas_call(
        functools.partial(_nsa_sample_select_kernel, n_prev=n_prev, nb=nb),
        out_shape=(jax.ShapeDtypeStruct((b, nbp, LANES), F32), jax.ShapeDtypeStruct((b, LANES, g_n * hd), F32)),
        grid=(b,),
        in_specs=[pl.BlockSpec((None, g_n * hd, LANES), per_b1),
                  pl.BlockSpec((None, n_ch, g_n * hd), per_b1),
                  pl.BlockSpec((None, n_ch, g_n * hd), per_b1),
                  pl.BlockSpec((nbp, n_ch), const1),
                  pl.BlockSpec((LANES, LANES), const1),
                  pl.BlockSpec((1, LANES), const1),
                  pl.BlockSpec((1, LANES), const1)],
        out_specs=(pl.BlockSpec((None, nbp, LANES), per_b1), pl.BlockSpec((None, LANES, g_n * hd), per_b1)),
        compiler_params=_cparams(1, 48),
        name="nsa_sample_select",
    )(qblk, kc, vc, jnp.asarray(mapt), jnp.asarray(rsum), jnp.asarray(slope), jnp.asarray(lq))

    blocks_per_page = PAGE_SIZE // SEL_BLOCK
    sel_pages_mask = sel[:, :npg * blocks_per_page].reshape(b, npg, blocks_per_page, LANES)
    page_used = jnp.max(sel_pages_mask, axis=(2, 3)) > 0
    order = jnp.argsort(jnp.logical_not(page_used), axis=1, stable=True).astype(jnp.int32)
    n_used = jnp.maximum(jnp.sum(page_used, axis=1), 1).astype(jnp.int32)
    phys = jnp.take_along_axis(page_table, order, axis=1)
    cur_blk = n_prev // SEL_BLOCK
    sel_cur = sel[:, cur_blk:cur_blk + 1, :]

    def page_spec(k):
        return pl.BlockSpec(
            (None, page_rows, hd),
            lambda bi, j, ph, lg, nu: (ph[bi * npg + jnp.minimum(j * n_p + k, nu[bi] - 1)], 0, 0))

    const2 = lambda bi, j, ph, lg, nu: (0, 0)
    per_b = lambda bi, j, ph, lg, nu: (bi, 0, 0)
    grid_spec = pltpu.PrefetchScalarGridSpec(
        num_scalar_prefetch=3,
        grid=(b, npg // n_p),
        in_specs=[pl.BlockSpec((None, g_n * hd, LANES), per_b),
                  pl.BlockSpec((None, q_len, LANES), per_b),
                  pl.BlockSpec((None, npg, blocks_per_page, LANES), lambda bi, j, ph, lg, nu: (bi, 0, 0, 0)),
                  pl.BlockSpec((None, 1, LANES), per_b),
                  pl.BlockSpec((None, LANES, g_n * hd), per_b),
                  pl.BlockSpec((1, LANES), const2),
                  pl.BlockSpec((1, LANES), const2),
                  pl.BlockSpec((None, sel_new_p.shape[1], width), per_b),
                  pl.BlockSpec((None, win_p.shape[1], width), per_b)]
        + [page_spec(k) for k in range(n_p)],
        out_specs=pl.BlockSpec((None, q_len, qw), per_b),
        scratch_shapes=[pltpu.VMEM((1, LANES), F32), pltpu.VMEM((1, LANES), F32),
                        pltpu.VMEM((LANES, g_n * hd), F32)],
    )
    kern = functools.partial(_nsa_sample_kernel, n_pages=n_p, n_prev=n_prev, q_len=q_len,
                             win_start=n_prev + q_len - win_full.shape[1], npg=npg)
    return pl.pallas_call(
        kern,
        out_shape=jax.ShapeDtypeStruct((b, q_len, qw), F32),
        grid_spec=grid_spec,
        compiler_params=_cparams(2, 48),
        name="nsa_sample",
    )(phys.reshape(-1), order.reshape(-1), n_used, qblk, gates_f32, sel_pages_mask, sel_cur, o_c,
      jnp.asarray(slope), jnp.asarray(lq), sel_new_p, win_p, *([sel_pages] * n_p))


PROMPT_TILE = 512
KV_PROJ_TILE = 256


def _trunk(x, params, *, n_prev, pool_prefix, conv_prefix, cmp_pages, cmp_table, sel_pages, sel_table, win_past,
           seq_tile, seqs_per_tile):
    p = params
    b, t, d = x.shape
    depth = p['ln_mix'].shape[0]
    n_a = len(p['w_pool'])
    hd = LANES
    kvw = N_KV_HEADS * hd
    row_w = 2 * kvw
    is_prompt = cmp_pages is None
    no_halo = jnp.zeros((b, max(POOL_HALO, FFN_HALO), d), F32)
    pool_rows, conv_rows = [], []
    for l in range(depth):
        if l < n_a:
            x, rows = pool_layer(x, x if is_prompt else no_halo, pool_prefix[l], p['ln_mix'][l], p['w_pool'][l], p['pool_scale'][l],
                                 n_prev=n_prev, tm=seq_tile)
            pool_rows.append(rows)
        else:
            xm = x.reshape(b * t, d)
            if l == n_a:
                raw = [None] * N_KV_HEADS
                chunk_gain = 2 * raw + [0] * N_KV_HEADS + raw + [1] * N_KV_HEADS + raw
                cmp_f, sel_f, win_f, kv_b = norm_linear(xm, p['ln_kv'], p['w_kv'], p['k_norm'][1:N_BRANCH], chunk_gain,
                                                        n_f32=N_BRANCH, tm=KV_PROJ_TILE)
                cmp_rows = cmp_f.reshape((b, t, 2, N_KV_HEADS, hd))
                if is_prompt:
                    cmp_pages = cmp_rows.reshape(b * t // PAGE_SIZE, PAGE_SIZE * row_w // hd, hd)
                    cmp_table = jnp.arange(b * t // PAGE_SIZE, dtype=jnp.int32).reshape(b, t // PAGE_SIZE)
                    win_full = win_f.reshape(b, t, row_w)
                else:
                    win_full = jnp.concatenate([win_past, win_f.reshape(b, t, row_w)], axis=1)
                pab = compress_partial(cmp_pages, cmp_table, p['w1cat'])
                kc, vc = compress_finish(pab, p['cmp_pos'], p['w1flat'], p['cmp_w2'], p['k_norm'][0])
            j = l - n_a
            q_f, q_b = norm_linear(xm, p['ln_mix'][l], p['w_qg'][j], p['q_norm'][j:j + 1],
                                   [0] * N_HEADS + [None] * (p['w_qg'][j].shape[1] // hd - N_HEADS),
                                   norm_scale=hd ** -0.5, sigmoid_from=N_HEADS)
            qw = N_HEADS * hd
            if is_prompt:
                o = nsa_prompt(q_b.reshape(b, t, -1), q_f.reshape(b, t, -1), kc, vc, kv_b.reshape(b, t, -1))
            else:
                q3 = q_f.reshape(b, t, -1)
                o = nsa_sample(q3[:, :, :qw], q3[:, :, qw:], kc, vc, sel_pages, sel_table,
                               sel_f.reshape(b, t, row_w), win_full, n_prev=n_prev).astype(BF16)
            x = linear_residual(o.reshape(b * t, qw), p['w_o'][j], xm).reshape(b, t, d)
        x, rows = ffn_layer(x, x if is_prompt else no_halo, conv_prefix[l], p['ln_ffn'][l], p['w_in'], p['conv_w'][l],
                            p['conv_b'][l], p['w_out'], layer=l, nb=seqs_per_tile, tm=seq_tile)
        conv_rows.append(rows[:, FFN_HALO - (CONV_WIDTH - 1):])
    row_shape = (2, N_KV_HEADS, hd)
    n_win = min(WINDOW, n_prev + t)
    win_state = win_full[:, win_full.shape[1] - n_win:].reshape((b, n_win) + row_shape)
    return (x, cmp_rows, sel_f.reshape((b, t) + row_shape), win_state,
            jnp.stack(pool_rows), jnp.stack(conv_rows))


def kernel(x_prompt, x_sample, cache_kv_cmp, cache_kv_sel, page_table, state_kv_win, state_pool, state_conv, ln_mix, ln_ffn, w_pool, pool_scale, ln_kv, w_kv, k_norm, cmp_w1, cmp_pos, cmp_w2, w_qg, q_norm, w_o, w_in, conv_w, conv_b, w_out):
    depth = ln_mix.shape[0]
    n_a = w_pool.shape[0]
    d = x_prompt.shape[2]
    ff2 = w_in.shape[2]
    hd = LANES
    row_w = 2 * N_KV_HEADS * hd
    qg_pad = (-w_qg.shape[2]) % LANES
    half = CMP_BLOCK // 2
    params = dict(
        ln_mix=ln_mix, ln_ffn=ln_ffn, pool_scale=pool_scale, ln_kv=ln_kv, k_norm=k_norm, q_norm=q_norm,
        conv_w=conv_w, conv_b=conv_b, cmp_pos=cmp_pos,
        w_pool=[w_pool[l].astype(BF16) for l in range(n_a)], w_kv=w_kv.astype(BF16),
        w_qg=[jnp.pad(w_qg[j], ((0, 0), (0, qg_pad))).astype(BF16) for j in range(depth - n_a)],
        w_o=[w_o[j].astype(BF16) for j in range(depth - n_a)],
        w_in=w_in.astype(BF16), w_out=w_out.astype(BF16),
        cmp_w2=cmp_w2.astype(BF16),
        w1cat=jnp.concatenate([cmp_w1[:, :half], cmp_w1[:, half:]], axis=-1).astype(BF16).reshape(
            2, half // CMP_PAIR, CMP_PAIR * hd, 2 * cmp_w1.shape[3]),
        w1flat=cmp_w1.reshape(2, CMP_BLOCK * hd, cmp_w1.shape[3]).astype(BF16),
    )

    b_p, t_p, _ = x_prompt.shape
    prompt = _trunk(
        x_prompt, params, n_prev=0,
        pool_prefix=jnp.zeros((n_a, b_p, POOL_HALO, d), F32),
        conv_prefix=jnp.zeros((depth, b_p, FFN_HALO, ff2), F32),
        cmp_pages=None, cmp_table=None, sel_pages=None, sel_table=None, win_past=None,
        seq_tile=min(PROMPT_TILE, t_p), seqs_per_tile=1)

    b_s, t_s, _ = x_sample.shape
    n_prev = page_table.shape[1] * PAGE_SIZE
    assert t_s < CMP_STRIDE and cache_kv_cmp.shape[1] == PAGE_SIZE
    sample = _trunk(
        x_sample, params, n_prev=n_prev,
        pool_prefix=jnp.pad(state_pool, ((0, 0), (0, 0), (POOL_HALO - POOL_KEEP, 0), (0, 0))),
        conv_prefix=jnp.pad(state_conv, ((0, 0), (0, 0), (FFN_HALO - (CONV_WIDTH - 1), 0), (0, 0))),
        cmp_pages=cache_kv_cmp.reshape(cache_kv_cmp.shape[0], PAGE_SIZE * row_w // hd, hd), cmp_table=page_table,
        sel_pages=cache_kv_sel.reshape(cache_kv_sel.shape[0], PAGE_SIZE * row_w // hd, hd), sel_table=page_table,
        win_past=state_kv_win.reshape(b_s, state_kv_win.shape[1], row_w),
        seq_tile=t_s, seqs_per_tile=b_s)

    return tuple(leaf for pair in zip(prompt, sample) for leaf in pair)
```

```python
import functools

import numpy as np
import jax
import jax.numpy as jnp
from jax import lax
from jax.experimental import pallas as pl
from jax.experimental.pallas import tpu as pltpu

F32 = jnp.float32
BF16 = jnp.bfloat16

POOL_WINDOWS = (2, 4, 8, 16)
POOL_KEEP = max(POOL_WINDOWS) - 1
N_HEADS = 16
N_KV_HEADS = 4
HEADS_PER_GROUP = N_HEADS // N_KV_HEADS
N_BRANCH = 3
CMP_BLOCK = 32
CMP_STRIDE = 16
SEL_BLOCK = 64
N_SEL = 16
WINDOW = 512
FORCE_BONUS = 1e4
CONV_WIDTH = 3
Q_BLOCK = 128
PAGE_SIZE = 128
EPS = 1e-6
NEG_INF = -1e30

LANES = 128
SUBLANES = 8
BF16_ROWS = 16
MIB = 2 ** 20


def _cparams(n_axes, vmem_mib):
    return pltpu.CompilerParams(dimension_semantics=("arbitrary",) * n_axes,
                                vmem_limit_bytes=vmem_mib * MIB)


def _rms(x, g):
    return (x * lax.rsqrt(jnp.mean(x * x, axis=-1, keepdims=True) + EPS)) * g


def _alibi_slope(h):
    return float(np.float32(2.0 ** (-8.0 * (h + 1) / N_HEADS)))


def _norm_linear_kernel(x_ref, g_ref, w_ref, hg_ref, *out_refs, chunk_gain, norm_scale, sigmoid_from):
    of_refs, ob_ref = out_refs[:-1], out_refs[-1]
    per_out = of_refs[0].shape[1] // LANES
    h = _rms(x_ref[...], g_ref[...]).astype(BF16)
    y = jnp.dot(h, w_ref[...], preferred_element_type=F32)
    for c in range(y.shape[1] // LANES):
        cols = slice(c * LANES, (c + 1) * LANES)
        yc = y[:, cols]
        if chunk_gain[c] is not None:
            yc = _rms(yc, hg_ref[chunk_gain[c]:chunk_gain[c] + 1, :])
            if norm_scale is not None:
                yc = yc * norm_scale
        elif sigmoid_from is not None and c >= sigmoid_from:
            yc = jax.nn.sigmoid(yc)
        of_refs[c // per_out][:, (c % per_out) * LANES:(c % per_out + 1) * LANES] = yc
        ob_ref[:, cols] = yc.astype(BF16)


def norm_linear(x, gain, w_bf16, head_gains, chunk_gain, *, norm_scale=None, sigmoid_from=None, n_f32=1, tm=512):
    m, k = x.shape
    n = w_bf16.shape[1]
    tm = min(tm, m)
    assert m % tm == 0 and n % (n_f32 * LANES) == 0 and len(chunk_gain) == n // LANES
    kern = functools.partial(_norm_linear_kernel, chunk_gain=tuple(chunk_gain), norm_scale=norm_scale,
                             sigmoid_from=sigmoid_from)
    row_tile = lambda width: pl.BlockSpec((tm, width), lambda i: (i, 0))
    return pl.pallas_call(
        kern,
        out_shape=(jax.ShapeDtypeStruct((m, n // n_f32), F32),) * n_f32 + (jax.ShapeDtypeStruct((m, n), BF16),),
        grid=(m // tm,),
        in_specs=[row_tile(k),
                  pl.BlockSpec((1, k), lambda i: (0, 0)),
                  pl.BlockSpec((k, n), lambda i: (0, 0)),
                  pl.BlockSpec(head_gains.shape, lambda i: (0, 0))],
        out_specs=(row_tile(n // n_f32),) * n_f32 + (row_tile(n),),
        compiler_params=_cparams(1, 56),
        name="norm_linear",
    )(x, gain.reshape(1, k), w_bf16, head_gains)


def _linear_residual_kernel(a_ref, w_ref, r_ref, o_ref):
    o_ref[...] = r_ref[...] + jnp.dot(a_ref[...], w_ref[...], preferred_element_type=F32)


def linear_residual(a_bf16, w_bf16, res, *, tm=512):
    m, k = a_bf16.shape
    n = w_bf16.shape[1]
    tm = min(tm, m)
    assert m % tm == 0
    return pl.pallas_call(
        _linear_residual_kernel,
        out_shape=jax.ShapeDtypeStruct((m, n), F32),
        grid=(m // tm,),
        in_specs=[pl.BlockSpec((tm, k), lambda i: (i, 0)),
                  pl.BlockSpec((k, n), lambda i: (0, 0)),
                  pl.BlockSpec((tm, n), lambda i: (i, 0))],
        out_specs=pl.BlockSpec((tm, n), lambda i: (i, 0)),
        compiler_params=_cparams(1, 48),
        name="linear_residual",
    )(a_bf16, w_bf16, res)


POOL_HALO = 16


def _pool_kernel(x_ref, xh_ref, pre_ref, g_ref, w_ref, sc_ref, o_ref, st_ref, ctx_scr, *, tm, n_prev):
    i = pl.program_id(1)
    x = x_ref[...]
    h = _rms(x, g_ref[...])
    halo = _rms(xh_ref[...], g_ref[...])
    ctx_scr[0:POOL_HALO, :] = jnp.where(i == 0, pre_ref[...], halo)
    ctx_scr[POOL_HALO:POOL_HALO + tm, :] = h
    t = i * tm + lax.broadcasted_iota(jnp.int32, (tm, 1), 0)
    gw = x.shape[1] // len(POOL_WINDOWS)
    for gi, w in enumerate(POOL_WINDOWS):
        cols = slice(gi * gw, (gi + 1) * gw)
        hs = h[:, cols]
        acc = hs
        for k in range(1, w):
            acc = acc + ctx_scr[POOL_HALO - k:POOL_HALO - k + tm, cols]
        cnt = jnp.minimum(w, n_prev + t + 1).astype(F32)
        pooled = acc / cnt - hs
        mixed = jnp.dot(pooled.astype(BF16), w_ref[gi], preferred_element_type=F32)
        o_ref[:, cols] = x[:, cols] + mixed * sc_ref[:, cols]
    st_ref[...] = ctx_scr[tm + POOL_HALO - POOL_KEEP:tm + POOL_HALO, :]


def pool_layer(x, x_halo_src, prefix16, gain, w_pool_bf16, scale, *, n_prev, tm):
    b, t, d = x.shape
    assert t % tm == 0 and tm % SUBLANES == 0
    hb = tm // POOL_HALO
    kern = functools.partial(_pool_kernel, tm=tm, n_prev=n_prev)
    return pl.pallas_call(
        kern,
        out_shape=(jax.ShapeDtypeStruct((b, t, d), F32), jax.ShapeDtypeStruct((b, POOL_KEEP, d), F32)),
        grid=(b, t // tm),
        in_specs=[pl.BlockSpec((None, tm, d), lambda bi, i: (bi, i, 0)),
                  pl.BlockSpec((None, POOL_HALO, d), lambda bi, i: (bi, jnp.maximum(i * hb - 1, 0), 0)),
                  pl.BlockSpec((None, POOL_HALO, d), lambda bi, i: (bi, 0, 0)),
                  pl.BlockSpec((1, d), lambda bi, i: (0, 0)),
                  pl.BlockSpec(w_pool_bf16.shape, lambda bi, i: (0, 0, 0)),
                  pl.BlockSpec((1, d), lambda bi, i: (0, 0))],
        out_specs=(pl.BlockSpec((None, tm, d), lambda bi, i: (bi, i, 0)),
                   pl.BlockSpec((None, POOL_KEEP, d), lambda bi, i: (bi, 0, 0))),
        scratch_shapes=[pltpu.VMEM((tm + POOL_HALO, d), F32)],
        compiler_params=_cparams(2, 48),
        name="pool_layer",
    )(x, x_halo_src, prefix16, gain.reshape(1, d), w_pool_bf16, scale.reshape(1, d))


FFN_HALO = 16
FFN_SPLIT = 2


def _ffn_kernel(x_ref, xh_ref, pg_ref, pv_ref, g_ref, wig_ref, wiv_ref, cwg_ref, cwv_ref, cbg_ref, cbv_ref, wo_ref,
                o_ref, sg_ref, sv_ref, u_scr, acc_scr, *h_scrs, nb, tm, te):
    hg_scr, hv_scr = h_scrs[:FFN_SPLIT], h_scrs[FFN_SPLIT:]
    i = pl.program_id(1)
    f = pl.program_id(2)
    n_ext = nb * te
    n_out = n_ext - FFN_HALO

    @pl.when(f == 0)
    def _():
        for s in range(nb):
            u_scr[s * te:s * te + FFN_HALO, :] = _rms(xh_ref[s], g_ref[...]).astype(BF16)
            u_scr[s * te + FFN_HALO:s * te + FFN_HALO + tm, :] = _rms(x_ref[s], g_ref[...]).astype(BF16)
            if te > FFN_HALO + tm:
                u_scr[s * te + FFN_HALO + tm:(s + 1) * te, :] = jnp.zeros((te - FFN_HALO - tm, u_scr.shape[1]), BF16)
        acc_scr[...] = jnp.zeros_like(acc_scr)

    u = u_scr[...]
    n_split = FFN_SPLIT
    pw = hg_scr[0].shape[1]
    first = i == 0

    def up_project(h_scr, w_ref, p_ref, piece):
        cols = slice(piece * pw, (piece + 1) * pw)
        h = jnp.dot(u, w_ref[:, cols], preferred_element_type=F32)
        h_scr[...] = h
        for s in range(nb):
            halo = slice(s * te, s * te + FFN_HALO)
            h_scr[halo, :] = jnp.where(first, p_ref[s, :, cols], h[halo])

    def conv(h_scr, cw_ref, cb_ref, piece):
        cols = slice(piece * pw, (piece + 1) * pw)
        c = cb_ref[:, cols] + cw_ref[0:1, cols] * h_scr[FFN_HALO - 2:FFN_HALO - 2 + n_out, :]
        c = c + cw_ref[1:2, cols] * h_scr[FFN_HALO - 1:FFN_HALO - 1 + n_out, :]
        return c + cw_ref[2:3, cols] * h_scr[FFN_HALO:FFN_HALO + n_out, :]

    acts = []
    for piece in range(n_split):
        up_project(hg_scr[piece], wig_ref, pg_ref, piece)
        up_project(hv_scr[piece], wiv_ref, pv_ref, piece)
        cg = conv(hg_scr[piece], cwg_ref, cbg_ref, piece)
        cv = conv(hv_scr[piece], cwv_ref, cbv_ref, piece)
        acts.append(((cg * jax.nn.sigmoid(cg)) * cv).astype(BF16))
    acc_scr[...] += jnp.dot(jnp.concatenate(acts, axis=1), wo_ref[...], preferred_element_type=F32)

    for s in range(nb):
        for piece in range(n_split):
            cols = slice(piece * pw, (piece + 1) * pw)
            sg_ref[s, :, cols] = hg_scr[piece][s * te + tm:s * te + tm + FFN_HALO, :]
            sv_ref[s, :, cols] = hv_scr[piece][s * te + tm:s * te + tm + FFN_HALO, :]

    @pl.when(f == pl.num_programs(2) - 1)
    def _():
        for s in range(nb):
            o_ref[s] = x_ref[s] + acc_scr[s * te:s * te + tm, :]


def ffn_layer(x, x_halo_src, prefix16, gain, w_in_bf16, conv_w, conv_b, w_out_bf16, *, layer, nb, tm, tf=512):
    b, t, d = x.shape
    ff = w_out_bf16.shape[1]
    assert t % tm == 0 and b % nb == 0 and ff % tf == 0 and tm % SUBLANES == 0
    nf = ff // tf
    hb = tm // FFN_HALO
    te = FFN_HALO + -(-tm // BF16_ROWS) * BF16_ROWS
    n_ext = nb * te
    kern = functools.partial(_ffn_kernel, nb=nb, tm=tm, te=te)
    gate_col = lambda bi, i, f: (0, f)
    val_col = lambda bi, i, f: (0, nf + f)
    y, sg, sv = pl.pallas_call(
        kern,
        out_shape=(jax.ShapeDtypeStruct((b, t, d), F32),
                   jax.ShapeDtypeStruct((b, t // tm, FFN_HALO, ff), F32),
                   jax.ShapeDtypeStruct((b, t // tm, FFN_HALO, ff), F32)),
        grid=(b // nb, t // tm, nf),
        in_specs=[pl.BlockSpec((nb, tm, d), lambda bi, i, f: (bi, i, 0)),
                  pl.BlockSpec((nb, FFN_HALO, d), lambda bi, i, f: (bi, jnp.maximum(i * hb - 1, 0), 0)),
                  pl.BlockSpec((nb, FFN_HALO, tf), lambda bi, i, f: (bi, 0, f)),
                  pl.BlockSpec((nb, FFN_HALO, tf), lambda bi, i, f: (bi, 0, nf + f)),
                  pl.BlockSpec((1, d), lambda bi, i, f: (0, 0)),
                  pl.BlockSpec((None, d, tf), lambda bi, i, f: (layer, 0, f)),
                  pl.BlockSpec((None, d, tf), lambda bi, i, f: (layer, 0, nf + f)),
                  pl.BlockSpec((CONV_WIDTH, tf), gate_col),
                  pl.BlockSpec((CONV_WIDTH, tf), val_col),
                  pl.BlockSpec((1, tf), gate_col),
                  pl.BlockSpec((1, tf), val_col),
                  pl.BlockSpec((None, tf, d), lambda bi, i, f: (layer, f, 0))],
        out_specs=(pl.BlockSpec((nb, tm, d), lambda bi, i, f: (bi, i, 0)),
                   pl.BlockSpec((nb, None, FFN_HALO, tf), lambda bi, i, f: (bi, i, 0, f)),
                   pl.BlockSpec((nb, None, FFN_HALO, tf), lambda bi, i, f: (bi, i, 0, f))),
        scratch_shapes=[pltpu.VMEM((n_ext, d), BF16), pltpu.VMEM((n_ext - FFN_HALO, d), F32)]
        + [pltpu.VMEM((n_ext, tf // FFN_SPLIT), F32)] * (2 * FFN_SPLIT),
        compiler_params=_cparams(3, 56),
        name="ffn_layer",
    )(x, x_halo_src, prefix16, prefix16, gain.reshape(1, d), w_in_bf16, w_in_bf16, conv_w, conv_w,
      conv_b.reshape(1, 2 * ff), conv_b.reshape(1, 2 * ff), w_out_bf16)
    return y, jnp.concatenate([sg[:, -1], sv[:, -1]], axis=-1)


CMP_PAGES = 16
CHUNKS_PER_PAGE = PAGE_SIZE // CMP_STRIDE
CMP_PAIR = 2


def _compress_kernel(pt_ref, *refs, n_pages):
    del pt_ref
    page_refs = refs[:n_pages]
    w_ref, o_ref = refs[n_pages], refs[n_pages + 1]
    heads = 2 * N_KV_HEADS
    k_rows = lax.broadcasted_iota(jnp.int32, (heads, LANES), 0) < N_KV_HEADS
    def head_tiles(s):
        lhs = [[], []]
        for k in range(n_pages):
            for n in range(0, CHUNKS_PER_PAGE, 2):
                t_e = page_refs[k][(CMP_STRIDE * n + s) * heads:(CMP_STRIDE * n + s + 1) * heads, :]
                t_o = page_refs[k][(CMP_STRIDE * (n + 1) + s) * heads:(CMP_STRIDE * (n + 1) + s + 1) * heads, :]
                lhs[0].append(jnp.where(k_rows, t_e, pltpu.roll(t_o, N_KV_HEADS, 0)))
                lhs[1].append(jnp.where(k_rows, pltpu.roll(t_e, N_KV_HEADS, 0), t_o))
        return [jnp.concatenate(rows, axis=0).astype(BF16) for rows in lhs]

    acc = [None, None]
    for s2 in range(CMP_STRIDE // CMP_PAIR):
        tiles = [head_tiles(CMP_PAIR * s2 + i) for i in range(CMP_PAIR)]
        for v in range(2):
            a = jnp.concatenate([t[v] for t in tiles], axis=1)
            d = jnp.dot(a, w_ref[v, s2], preferred_element_type=F32)
            acc[v] = d if acc[v] is None else acc[v] + d
    for v in range(2):
        o_ref[v] = acc[v]


def compress_partial(pages, page_table, w1cat_bf16):
    s, npg = page_table.shape
    page_rows, hd = pages.shape[1:]
    assert page_rows == PAGE_SIZE * 2 * N_KV_HEADS
    hid2 = w1cat_bf16.shape[3]
    n_p = min(CMP_PAGES, npg)
    assert npg % n_p == 0
    n_ch = npg * CHUNKS_PER_PAGE

    def page_spec(k):
        return pl.BlockSpec((None, page_rows, hd), lambda si, j, pt: (pt[si * npg + j * n_p + k], 0, 0))

    grid_spec = pltpu.PrefetchScalarGridSpec(
        num_scalar_prefetch=1,
        grid=(s, npg // n_p),
        in_specs=[page_spec(k) for k in range(n_p)]
        + [pl.BlockSpec(w1cat_bf16.shape, lambda si, j, pt: (0, 0, 0, 0))],
        out_specs=pl.BlockSpec((None, 2, n_p * CHUNKS_PER_PAGE * N_KV_HEADS, hid2), lambda si, j, pt: (si, 0, j, 0)),
    )
    return pl.pallas_call(
        functools.partial(_compress_kernel, n_pages=n_p),
        out_shape=jax.ShapeDtypeStruct((s, 2, n_ch * N_KV_HEADS, hid2), F32),
        grid_spec=grid_spec,
        compiler_params=_cparams(2, 48),
        name="compress_partial",
    )(page_table.reshape(-1), *([pages] * n_p), w1cat_bf16)


def _compress_finish_kernel(pab_ref, pos_ref, w1_ref, w2_ref, kn_ref, kc_ref, vc_ref):
    n = pab_ref.shape[1]
    hid = w2_ref.shape[1]
    for v in range(2):
        blk = pab_ref[v]
        nxt = pltpu.roll(blk[:, hid:], n - N_KV_HEADS, 0)
        pos = jnp.broadcast_to(pos_ref[v], (SUBLANES, pos_ref.shape[2])).astype(BF16)
        bias = jnp.dot(pos, w1_ref[v], preferred_element_type=F32)[0:1, :]
        hdn = jax.nn.gelu(blk[:, :hid] + nxt + bias)
        out = jnp.dot(hdn.astype(BF16), w2_ref[v], preferred_element_type=F32)
        if v == 0:
            kc_ref[...] = _rms(out, kn_ref[...]).astype(BF16)
        else:
            vc_ref[...] = out.astype(BF16)


def compress_finish(pab, cmp_pos, w1flat_bf16, w2_bf16, k_gain):
    s, _, rows, hid2 = pab.shape
    hd = w2_bf16.shape[2]
    kdim = w1flat_bf16.shape[1]
    out = jax.ShapeDtypeStruct((s, rows, hd), BF16)
    kc, vc = pl.pallas_call(
        _compress_finish_kernel,
        out_shape=(out, out),
        grid=(s,),
        in_specs=[pl.BlockSpec((None, 2, rows, hid2), lambda si: (si, 0, 0, 0)),
                  pl.BlockSpec((2, 1, kdim), lambda si: (0, 0, 0)),
                  pl.BlockSpec(w1flat_bf16.shape, lambda si: (0, 0, 0)),
                  pl.BlockSpec(w2_bf16.shape, lambda si: (0, 0, 0)),
                  pl.BlockSpec((1, hd), lambda si: (0, 0))],
        out_specs=(pl.BlockSpec((None, rows, hd), lambda si: (si, 0, 0)),
                   pl.BlockSpec((None, rows, hd), lambda si: (si, 0, 0))),
        compiler_params=_cparams(1, 48),
        name="compress_finish",
    )(pab, cmp_pos.reshape(2, 1, kdim), w1flat_bf16, w2_bf16, k_gain.reshape(1, hd))
    return (kc.reshape(s, rows // N_KV_HEADS, N_KV_HEADS * hd), vc.reshape(s, rows // N_KV_HEADS, N_KV_HEADS * hd))


def _cmp_to_sel_map(nc, nb):
    c_s = np.arange(nc) * CMP_STRIDE
    c_e = c_s + CMP_BLOCK - 1
    s_s = np.arange(nb) * SEL_BLOCK
    s_e = s_s + SEL_BLOCK - 1
    return ((c_s[:, None] <= s_e[None]) & (c_e[:, None] >= s_s[None])).astype(np.float32)


def _masked_softmax(s, mask, axis):
    s = jnp.where(mask, s, NEG_INF)
    m = jnp.max(s, axis=axis, keepdims=True)
    p = jnp.where(mask, jnp.exp(s - m), 0.0)
    den = jnp.sum(p, axis=axis, keepdims=True)
    return p * (1.0 / jnp.where(den > 0, den, 1.0))


SEL_KEY_TILE = 512


POS_SPLIT = 3
BLOCK_LANES = 64
MASK_BIAS = -1e30


def _position_columns(pos, onehot):
    pos = np.asarray(pos)
    assert pos.max() // SEL_BLOCK <= 256
    out = np.zeros((pos.shape[0], LANES), np.float32)
    if onehot:
        assert pos.max() // SEL_BLOCK < BLOCK_LANES
        out[np.arange(pos.shape[0]), pos // SEL_BLOCK] = 1.0
    for j in range(POS_SPLIT):
        out[:, BLOCK_LANES + j] = (pos // SEL_BLOCK) * SEL_BLOCK
        out[:, BLOCK_LANES + POS_SPLIT + j] = pos % SEL_BLOCK
    return out


def _slope_columns():
    out = np.zeros((N_HEADS, LANES), np.float32)
    for h in range(N_HEADS):
        rem = np.float32(_alibi_slope(h))
        for j in range(POS_SPLIT):
            part = np.float32(rem.astype(jnp.bfloat16))
            out[h, BLOCK_LANES + j] = part
            out[h, BLOCK_LANES + POS_SPLIT + j] = part
            rem = np.float32(rem - part)
        assert rem == 0
    return out


def _nsa_prompt_kernel(q_ref, gt_ref, kc_ref, vc_ref, cpos_ref, selk_ref, selv_ref, wink_ref, winv_ref,
                       spos_ref, wpos_ref, mapt_ref, slope_ref, o_ref, used_ref, *, wlen):
    i = pl.program_id(1)
    s0 = i * Q_BLOCK
    hd = LANES
    rr, qb, tk = HEADS_PER_GROUP, Q_BLOCK, SEL_KEY_TILE
    m_rows = rr * qb
    n_cp = kc_ref.shape[0]
    nb = mapt_ref.shape[0]
    nt_dims = (((1,), (1,)), ((), ()))
    kt_last = s0 // tk
    ws = pl.multiple_of(jnp.maximum(s0 - WINDOW, 0), Q_BLOCK)

    row = lax.broadcasted_iota(jnp.int32, (m_rows, 1), 0)
    pos_q = s0 + (row & (qb - 1))
    posq_l = s0 + lax.broadcasted_iota(jnp.int32, (1, qb), 1)
    blk = lax.broadcasted_iota(jnp.int32, (nb, 1), 0)
    cur = posq_l // SEL_BLOCK
    forced = (blk == 0) | (blk == cur) | (blk == cur - 1)
    valid = blk * SEL_BLOCK <= posq_l
    cmask = lax.broadcasted_iota(jnp.int32, (1, n_cp), 1) * CMP_STRIDE + (CMP_BLOCK - 1) <= pos_q
    dist_w = pos_q - (ws + lax.broadcasted_iota(jnp.int32, (1, wlen), 1))
    wmask = (dist_w >= 0) & (dist_w < WINDOW)

    per_group, chosen_any = [], None
    for g in range(N_KV_HEADS):
        cols = slice(g * hd, (g + 1) * hd)
        q = jnp.concatenate([q_ref[:, (g * rr + r) * hd:(g * rr + r + 1) * hd] for r in range(rr)], axis=0)
        slope_cols = slope_ref[g]
        q_pos = jnp.concatenate([q, slope_cols.astype(BF16)], axis=1)

        kc_aug = jnp.concatenate([kc_ref[:, cols], cpos_ref[...]], axis=1)
        s = lax.dot_general(q_pos, kc_aug, nt_dims, preferred_element_type=F32)
        p = _masked_softmax(s, cmask, -1)
        o_c = jnp.dot(p.astype(BF16), vc_ref[:, cols], preferred_element_type=F32)

        p_sum = p[0:qb]
        for r in range(1, rr):
            p_sum = p_sum + p[r * qb:(r + 1) * qb]
        imp = lax.dot_general(mapt_ref[...], p_sum, nt_dims, precision=lax.Precision.HIGHEST,
                              preferred_element_type=F32)
        imp = jnp.where(valid, jnp.where(forced, imp + FORCE_BONUS, imp), NEG_INF)
        cnt = jnp.zeros((nb, qb), F32)
        for c in range(nb):
            rowc = imp[c:c + 1, :]
            beats = (rowc > imp) | ((rowc == imp) & (blk > c))
            cnt = cnt + jnp.where(beats, 1.0, 0.0)
        chosen = cnt < min(N_SEL, nb)
        bias_t = jnp.where(chosen, 0.0, MASK_BIAS)
        chosen_f = jnp.where(chosen, 1.0, 0.0)
        chosen_any = chosen_f if chosen_any is None else jnp.maximum(chosen_any, chosen_f)
        bias = jnp.transpose(jnp.concatenate([bias_t, jnp.zeros((hd - nb, qb), F32)], axis=0))
        q_sel = jnp.concatenate([q, (slope_cols + jnp.concatenate([bias] * rr, axis=0)).astype(BF16)], axis=1)
        per_group.append((q_pos, q_sel, o_c))

    n_used = jnp.int32(0)
    for t in range(nb // (tk // SEL_BLOCK)):
        any_sel = jnp.max(chosen_any[t * (tk // SEL_BLOCK):(t + 1) * (tk // SEL_BLOCK), :]) > 0
        used_ref[n_used] = t
        n_used = n_used + jnp.where(any_sel & (t < kt_last), 1, 0)

    def sel_tile(kt, carry, causal):
        k0 = pl.multiple_of(kt * tk, tk)
        pos_cols = spos_ref[pl.ds(k0, tk), :]
        if causal:
            visible = k0 + lax.broadcasted_iota(jnp.int32, (1, tk), 1) <= pos_q
        out = []
        for g in range(N_KV_HEADS):
            cols = slice(g * hd, (g + 1) * hd)
            m_i, l_i, acc = carry[g]
            k_aug = jnp.concatenate([selk_ref[pl.ds(k0, tk), cols], pos_cols], axis=1)
            s = lax.dot_general(per_group[g][1], k_aug, nt_dims, preferred_element_type=F32)
            if causal:
                s = jnp.where(visible, s, NEG_INF)
            m_new = jnp.maximum(m_i, jnp.max(s, axis=-1, keepdims=True))
            alpha = jnp.exp(m_i - m_new)
            p = jnp.exp(s - m_new)
            l_new = alpha * l_i + jnp.sum(p, axis=-1, keepdims=True)
            acc = alpha * acc + jnp.dot(p.astype(BF16), selv_ref[pl.ds(k0, tk), cols], preferred_element_type=F32)
            out.append((m_new, l_new, acc))
        return tuple(out)

    init = tuple((jnp.full((m_rows, 1), NEG_INF, F32), jnp.zeros((m_rows, 1), F32), jnp.zeros((m_rows, hd), F32))
                 for _ in range(N_KV_HEADS))
    carry = lax.fori_loop(0, n_used, lambda j, c: sel_tile(used_ref[j], c, False), init)
    final = sel_tile(kt_last, carry, True)

    for g in range(N_KV_HEADS):
        cols = slice(g * hd, (g + 1) * hd)
        q_pos, _, o_c = per_group[g]
        _, l_s, acc_s = final[g]
        o_s = acc_s * (1.0 / jnp.where(l_s > 0, l_s, 1.0))

        kw_aug = jnp.concatenate([wink_ref[pl.ds(ws, wlen), cols], wpos_ref[pl.ds(ws, wlen), :]], axis=1)
        s = jnp.where(wmask, lax.dot_general(q_pos, kw_aug, nt_dims, preferred_element_type=F32), NEG_INF)
        p = jnp.exp(s - jnp.max(s, axis=-1, keepdims=True))
        l_w = jnp.sum(p, axis=-1, keepdims=True)
        o_w = jnp.dot(p.astype(BF16), winv_ref[pl.ds(ws, wlen), cols], preferred_element_type=F32)
        o_w = o_w * (1.0 / jnp.where(l_w > 0, l_w, 1.0))

        def gate(branch):
            return jnp.concatenate(
                [gt_ref[:, (g * rr + r) * N_BRANCH + branch:(g * rr + r) * N_BRANCH + branch + 1] for r in range(rr)],
                axis=0)

        o = gate(0) * o_c + gate(1) * o_s + gate(2) * o_w
        for r in range(rr):
            o_ref[:, (g * rr + r) * hd:(g * rr + r + 1) * hd] = o[r * qb:(r + 1) * qb].astype(BF16)


def nsa_prompt(q_bf16, gates_f32, kc, vc, kv_bf16):
    b, t, _ = q_bf16.shape
    qw = N_HEADS * LANES
    n_cp = kc.shape[1]
    nb = t // SEL_BLOCK
    assert t % SEL_KEY_TILE == 0 and t % Q_BLOCK == 0
    wlen = min(WINDOW + Q_BLOCK, t)
    kvw = N_KV_HEADS * LANES
    mapt = np.zeros((nb, n_cp), np.float32)
    mapt[:, :n_cp - 1] = _cmp_to_sel_map(n_cp - 1, nb).T
    cpos = _position_columns(np.arange(n_cp) * CMP_STRIDE + (CMP_BLOCK - 1), onehot=False)
    spos = _position_columns(np.arange(t), onehot=True)
    wpos = _position_columns(np.arange(t), onehot=False)
    slope = np.repeat(_slope_columns().reshape(N_KV_HEADS, HEADS_PER_GROUP, 1, LANES), Q_BLOCK, axis=2)
    slope = slope.reshape(N_KV_HEADS, HEADS_PER_GROUP * Q_BLOCK, LANES)
    kern = functools.partial(_nsa_prompt_kernel, wlen=wlen)
    per_b = lambda bi, i: (bi, 0, 0)
    const2 = lambda bi, i: (0, 0)

    def kv_cols(branch, v):
        return pl.BlockSpec((None, t, kvw), lambda bi, i: (bi, 0, 2 * branch + v))

    return pl.pallas_call(
        kern,
        out_shape=jax.ShapeDtypeStruct((b, t, qw), BF16),
        grid=(b, t // Q_BLOCK),
        in_specs=[pl.BlockSpec((None, Q_BLOCK, qw), lambda bi, i: (bi, i, 0)),
                  pl.BlockSpec((None, Q_BLOCK, LANES), lambda bi, i: (bi, i, qw // LANES)),
                  pl.BlockSpec((None, n_cp, kvw), per_b),
                  pl.BlockSpec((None, n_cp, kvw), per_b),
                  pl.BlockSpec((n_cp, LANES), const2),
                  kv_cols(1, 0), kv_cols(1, 1), kv_cols(2, 0), kv_cols(2, 1),
                  pl.BlockSpec((t, LANES), const2),
                  pl.BlockSpec((t, LANES), const2),
                  pl.BlockSpec((nb, n_cp), const2),
                  pl.BlockSpec(slope.shape, lambda bi, i: (0, 0, 0))],
        out_specs=pl.BlockSpec((None, Q_BLOCK, qw), lambda bi, i: (bi, i, 0)),
        scratch_shapes=[pltpu.SMEM((t // SEL_KEY_TILE,), jnp.int32)],
        compiler_params=_cparams(2, 56),
        name="nsa_prompt",
    )(q_bf16, gates_f32, kc, vc, jnp.asarray(cpos, dtype=BF16), kv_bf16, kv_bf16, kv_bf16, kv_bf16,
      jnp.asarray(spos, dtype=BF16), jnp.asarray(wpos, dtype=BF16), jnp.asarray(mapt), jnp.asarray(slope))


SAMPLE_PAGES = 8
TAKEN = -3e38
TN_DIMS = (((0,), (0,)), ((), ()))


def _lane_to_rows(v):
    return jnp.transpose(jnp.broadcast_to(v, (LANES, LANES)))


def _scale_rows(a, v):
    t = _lane_to_rows(v)
    return a * jnp.concatenate([t] * (a.shape[1] // LANES), axis=1)


def _sample_scores(k_bf16, pos_k, qb, slope, pos_q):
    dist = pos_q - pos_k
    return jnp.dot(k_bf16, qb, preferred_element_type=F32) - slope * dist, dist


def _nsa_sample_select_kernel(qb_ref, kc_ref, vc_ref, mapt_ref, rsum_ref, slope_ref, lq_ref, sel_ref, oc_ref,
                              *, n_prev, nb):
    lq = lq_ref[...]
    n_c = kc_ref.shape[0]
    nbp = sel_ref.shape[0]
    pos_c = (lax.broadcasted_iota(jnp.int32, (n_c, 1), 0) * CMP_STRIDE + (CMP_BLOCK - 1)).astype(F32)
    s, dist = _sample_scores(kc_ref[...], pos_c, qb_ref[...], slope_ref[...], n_prev + lq)
    p = _masked_softmax(s, dist >= 0, 0)
    oc_ref[...] = lax.dot_general(p.astype(BF16), vc_ref[...], TN_DIMS, preferred_element_type=F32)
    p_r = jnp.dot(p, rsum_ref[...], precision=lax.Precision.HIGHEST, preferred_element_type=F32)
    imp = jnp.dot(mapt_ref[...], p_r, precision=lax.Precision.HIGHEST, preferred_element_type=F32)
    blk = lax.broadcasted_iota(jnp.int32, (nbp, 1), 0)
    posq_i = n_prev + lq.astype(jnp.int32)
    cur = posq_i // SEL_BLOCK
    forced = (blk == 0) | (blk == cur) | (blk == cur - 1)
    valid = blk * SEL_BLOCK <= posq_i
    imp = jnp.where(valid, jnp.where(forced, imp + FORCE_BONUS, imp), NEG_INF)
    blk_f = blk.astype(F32)

    def pick(_, carry):
        work, chosen = carry
        top = jnp.max(work, axis=0, keepdims=True)
        first = jnp.min(jnp.where(work == top, blk_f, float(nbp)), axis=0, keepdims=True)
        hit = blk_f == first
        return jnp.where(hit, TAKEN, work), jnp.where(hit, 1.0, chosen)

    _, chosen = lax.fori_loop(0, min(N_SEL, nb), pick, (imp, jnp.zeros((nbp, LANES), F32)))
    sel_ref[...] = chosen


def _nsa_sample_kernel(ph_ref, lg_ref, nu_ref, qb_ref, gt_ref, sel3_ref, selcur_ref, oc_ref, slope_ref, lq_ref,
                       selnew_ref, win_ref, *refs, n_pages, n_prev, q_len, win_start, npg):
    del ph_ref
    pages = refs[:n_pages]
    o_ref, m_scr, l_scr, acc_scr = refs[n_pages:]
    bi = pl.program_id(0)
    j = pl.program_id(1)
    hd = LANES
    kvw = N_KV_HEADS * hd
    n_used = nu_ref[bi]
    qb = qb_ref[...]
    slope = slope_ref[...]
    lq = lq_ref[...]
    pos_q = n_prev + lq

    def scores(k_bf16, pos_k):
        return _sample_scores(k_bf16, pos_k, qb, slope, pos_q)

    def online(s, mask, v_bf16):
        s = jnp.where(mask, s, NEG_INF)
        m_old = m_scr[...]
        m_new = jnp.maximum(m_old, jnp.max(s, axis=0, keepdims=True))
        alpha = jnp.exp(m_old - m_new)
        p = jnp.where(mask, jnp.exp(s - m_new), 0.0)
        l_scr[...] = alpha * l_scr[...] + jnp.sum(p, axis=0, keepdims=True)
        m_scr[...] = m_new
        pv = lax.dot_general(p.astype(BF16), v_bf16, TN_DIMS, preferred_element_type=F32)
        acc_scr[...] = _scale_rows(acc_scr[...], alpha) + pv

    @pl.when(j == 0)
    def _():
        m_scr[...] = jnp.full(m_scr.shape, NEG_INF, F32)
        l_scr[...] = jnp.zeros(l_scr.shape, F32)
        acc_scr[...] = jnp.zeros(acc_scr.shape, F32)

    @pl.when(j * n_pages < n_used)
    def _():
        heads = 2 * N_KV_HEADS

        def head_rows(h):
            return jnp.concatenate([pg[pl.ds(h, PAGE_SIZE, stride=heads), :] for pg in pages], axis=0)

        k_all = jnp.concatenate([head_rows(g) for g in range(N_KV_HEADS)], axis=1).astype(BF16)
        v_all = jnp.concatenate([head_rows(N_KV_HEADS + g) for g in range(N_KV_HEADS)], axis=1).astype(BF16)
        pos_parts, sel_parts = [], []
        for k in range(n_pages):
            slot = j * n_pages + k
            page = lg_ref[bi * npg + jnp.minimum(slot, n_used - 1)]
            live = jnp.where(slot < n_used, 1.0, 0.0)
            pos_parts.append((page * PAGE_SIZE + lax.broadcasted_iota(jnp.int32, (PAGE_SIZE, 1), 0)).astype(F32))
            blocks = sel3_ref[page] * live
            sel_parts.append(jnp.broadcast_to(blocks[:, None, :], (blocks.shape[0], SEL_BLOCK, LANES))
                             .reshape(PAGE_SIZE, LANES))
        s, dist = scores(k_all, jnp.concatenate(pos_parts, axis=0))
        online(s, (dist >= 0) & (jnp.concatenate(sel_parts, axis=0) > 0.5), v_all)

    @pl.when(j == pl.num_programs(1) - 1)
    def _():
        n_new = selnew_ref.shape[0]
        pos_n = (n_prev + lax.broadcasted_iota(jnp.int32, (n_new, 1), 0)).astype(F32)
        s, dist = scores(selnew_ref[:, 0:kvw].astype(BF16), pos_n)
        selrow = selcur_ref[...]
        online(s, (dist >= 0) & (selrow > 0.5), selnew_ref[:, kvw:2 * kvw].astype(BF16))
        l_s = l_scr[...]
        o_s = _scale_rows(acc_scr[...], 1.0 / jnp.where(l_s > 0, l_s, 1.0))

        n_w = win_ref.shape[0]
        pos_w = (win_start + lax.broadcasted_iota(jnp.int32, (n_w, 1), 0)).astype(F32)
        s, dist = scores(win_ref[:, 0:kvw].astype(BF16), pos_w)
        p = _masked_softmax(s, (dist >= 0) & (dist < WINDOW), 0)
        o_w = lax.dot_general(p.astype(BF16), win_ref[:, kvw:2 * kvw].astype(BF16), TN_DIMS,
                              preferred_element_type=F32)
        o_c = oc_ref[...]

        rr = HEADS_PER_GROUP
        for g in range(N_KV_HEADS):
            for r in range(rr):
                h = g * rr + r
                rows = slice((g * rr + r) * q_len, (g * rr + r + 1) * q_len)
                cols = slice(g * hd, (g + 1) * hd)
                gc = gt_ref[:, h * N_BRANCH + 0:h * N_BRANCH + 1]
                gs = gt_ref[:, h * N_BRANCH + 1:h * N_BRANCH + 2]
                gw = gt_ref[:, h * N_BRANCH + 2:h * N_BRANCH + 3]
                o_ref[:, h * hd:(h + 1) * hd] = gc * o_c[rows, cols] + gs * o_s[rows, cols] + gw * o_w[rows, cols]


def nsa_sample(q_f32, gates_f32, kc, vc, sel_pages, page_table, sel_new, win_full, *, n_prev):
    b, q_len, qw = q_f32.shape
    hd = LANES
    rr = HEADS_PER_GROUP
    g_n = N_KV_HEADS
    assert g_n * rr * q_len == LANES and n_prev % SEL_BLOCK == 0 and q_len <= SEL_BLOCK
    npg = page_table.shape[1]
    n_p = min(SAMPLE_PAGES, npg)
    assert npg % n_p == 0 and (n_p * PAGE_SIZE // SEL_BLOCK) % SUBLANES == 0
    n_ch = kc.shape[1]
    nb = -(-(n_prev + q_len) // SEL_BLOCK)
    nbp = -(-nb // SUBLANES) * SUBLANES
    width = sel_new.shape[2]
    page_rows = sel_pages.shape[1]
    assert page_rows == PAGE_SIZE * 2 * g_n and sel_pages.shape[2] == hd

    qt = q_f32.reshape(b, q_len, g_n, rr, hd).transpose(0, 2, 4, 3, 1).reshape(b, g_n, hd, rr * q_len)
    qblk = jnp.einsum('bgdl,gh->bgdhl', qt, jnp.eye(g_n, dtype=F32)).reshape(b, g_n * hd, LANES).astype(BF16)

    lane = np.arange(LANES)
    lane_g, lane_r, lane_q = lane // (rr * q_len), (lane // q_len) % rr, lane % q_len
    slope = np.array([_alibi_slope(h) for h in lane_g * rr + lane_r], np.float32).reshape(1, LANES)
    lq = lane_q.astype(np.float32).reshape(1, LANES)
    rsum = ((lane_g[:, None] == lane_g[None, :]) & (lane_q[:, None] == lane_q[None, :])).astype(np.float32)
    mapt = np.zeros((nbp, n_ch), np.float32)
    mapt[:nb, :n_ch - 1] = _cmp_to_sel_map(n_ch - 1, nb).T

    pad16 = lambda a: jnp.pad(a, ((0, 0), (0, (-a.shape[1]) % BF16_ROWS), (0, 0)))
    sel_new_p = pad16(sel_new)
    win_p = pad16(win_full)

    per_b1 = lambda bi: (bi, 0, 0)
    const1 = lambda bi: (0, 0)
    sel, o_c = pl.pallas_call(
        functools.partial(_nsa_sample_select_kernel, n_prev=n_prev, nb=nb),
        out_shape=(jax.ShapeDtypeStruct((b, nbp, LANES), F32), jax.ShapeDtypeStruct((b, LANES, g_n * hd), F32)),
        grid=(b,),
        in_specs=[pl.BlockSpec((None, g_n * hd, LANES), per_b1),
                  pl.BlockSpec((None, n_ch, g_n * hd), per_b1),
                  pl.BlockSpec((None, n_ch, g_n * hd), per_b1),
                  pl.BlockSpec((nbp, n_ch), const1),
                  pl.BlockSpec((LANES, LANES), const1),
                  pl.BlockSpec((1, LANES), const1),
                  pl.BlockSpec((1, LANES), const1)],
        out_specs=(pl.BlockSpec((None, nbp, LANES), per_b1), pl.BlockSpec((None, LANES, g_n * hd), per_b1)),
        compiler_params=_cparams(1, 48),
        name="nsa_sample_select",
    )(qblk, kc, vc, jnp.asarray(mapt), jnp.asarray(rsum), jnp.asarray(slope), jnp.asarray(lq))

    blocks_per_page = PAGE_SIZE // SEL_BLOCK
    sel_pages_mask = sel[:, :npg * blocks_per_page].reshape(b, npg, blocks_per_page, LANES)
    page_used = jnp.max(sel_pages_mask, axis=(2, 3)) > 0
    order = jnp.argsort(jnp.logical_not(page_used), axis=1, stable=True).astype(jnp.int32)
    n_used = jnp.maximum(jnp.sum(page_used, axis=1), 1).astype(jnp.int32)
    phys = jnp.take_along_axis(page_table, order, axis=1)
    cur_blk = n_prev // SEL_BLOCK
    sel_cur = sel[:, cur_blk:cur_blk + 1, :]

    def page_spec(k):
        return pl.BlockSpec(
            (None, page_rows, hd),
            lambda bi, j, ph, lg, nu: (ph[bi * npg + jnp.minimum(j * n_p + k, nu[bi] - 1)], 0, 0))

    const2 = lambda bi, j, ph, lg, nu: (0, 0)
    per_b = lambda bi, j, ph, lg, nu: (bi, 0, 0)
    grid_spec = pltpu.PrefetchScalarGridSpec(
        num_scalar_prefetch=3,
        grid=(b, npg // n_p),
        in_specs=[pl.BlockSpec((None, g_n * hd, LANES), per_b),
                  pl.BlockSpec((None, q_len, LANES), per_b),
                  pl.BlockSpec((None, npg, blocks_per_page, LANES), lambda bi, j, ph, lg, nu: (bi, 0, 0, 0)),
                  pl.BlockSpec((None, 1, LANES), per_b),
                  pl.BlockSpec((None, LANES, g_n * hd), per_b),
                  pl.BlockSpec((1, LANES), const2),
                  pl.BlockSpec((1, LANES), const2),
                  pl.BlockSpec((None, sel_new_p.shape[1], width), per_b),
                  pl.BlockSpec((None, win_p.shape[1], width), per_b)]
        + [page_spec(k) for k in range(n_p)],
        out_specs=pl.BlockSpec((None, q_len, qw), per_b),
        scratch_shapes=[pltpu.VMEM((1, LANES), F32), pltpu.VMEM((1, LANES), F32),
                        pltpu.VMEM((LANES, g_n * hd), F32)],
    )
    kern = functools.partial(_nsa_sample_kernel, n_pages=n_p, n_prev=n_prev, q_len=q_len,
                             win_start=n_prev + q_len - win_full.shape[1], npg=npg)
    return pl.pallas_call(
        kern,
        out_shape=jax.ShapeDtypeStruct((b, q_len, qw), F32),
        grid_spec=grid_spec,
        compiler_params=_cparams(2, 48),
        name="nsa_sample",
    )(phys.reshape(-1), order.reshape(-1), n_used, qblk, gates_f32, sel_pages_mask, sel_cur, o_c,
      jnp.asarray(slope), jnp.asarray(lq), sel_new_p, win_p, *([sel_pages] * n_p))


PROMPT_TILE = 512
KV_PROJ_TILE = 256


def _trunk(x, params, *, n_prev, pool_prefix, conv_prefix, cmp_pages, cmp_table, sel_pages, sel_table, win_past,
           seq_tile, seqs_per_tile):
    p = params
    b, t, d = x.shape
    depth = p['ln_mix'].shape[0]
    n_a = len(p['w_pool'])
    hd = LANES
    kvw = N_KV_HEADS * hd
    row_w = 2 * kvw
    is_prompt = cmp_pages is None
    no_halo = jnp.zeros((b, max(POOL_HALO, FFN_HALO), d), F32)
    pool_rows, conv_rows = [], []
    for l in range(depth):
        if l < n_a:
            x, rows = pool_layer(x, x if is_prompt else no_halo, pool_prefix[l], p['ln_mix'][l], p['w_pool'][l], p['pool_scale'][l],
                                 n_prev=n_prev, tm=seq_tile)
            pool_rows.append(rows)
        else:
            xm = x.reshape(b * t, d)
            if l == n_a:
                raw = [None] * N_KV_HEADS
                chunk_gain = 2 * raw + [0] * N_KV_HEADS + raw + [1] * N_KV_HEADS + raw
                cmp_f, sel_f, win_f, kv_b = norm_linear(xm, p['ln_kv'], p['w_kv'], p['k_norm'][1:N_BRANCH], chunk_gain,
                                                        n_f32=N_BRANCH, tm=KV_PROJ_TILE)
                cmp_rows = cmp_f.reshape((b, t, 2, N_KV_HEADS, hd))
                if is_prompt:
                    cmp_pages = cmp_rows.reshape(b * t // PAGE_SIZE, PAGE_SIZE * row_w // hd, hd)
                    cmp_table = jnp.arange(b * t // PAGE_SIZE, dtype=jnp.int32).reshape(b, t // PAGE_SIZE)
                    win_full = win_f.reshape(b, t, row_w)
                else:
                    win_full = jnp.concatenate([win_past, win_f.reshape(b, t, row_w)], axis=1)
                pab = compress_partial(cmp_pages, cmp_table, p['w1cat'])
                kc, vc = compress_finish(pab, p['cmp_pos'], p['w1flat'], p['cmp_w2'], p['k_norm'][0])
            j = l - n_a
            q_f, q_b = norm_linear(xm, p['ln_mix'][l], p['w_qg'][j], p['q_norm'][j:j + 1],
                                   [0] * N_HEADS + [None] * (p['w_qg'][j].shape[1] // hd - N_HEADS),
                                   norm_scale=hd ** -0.5, sigmoid_from=N_HEADS)
            qw = N_HEADS * hd
            if is_prompt:
                o = nsa_prompt(q_b.reshape(b, t, -1), q_f.reshape(b, t, -1), kc, vc, kv_b.reshape(b, t, -1))
            else:
                q3 = q_f.reshape(b, t, -1)
                o = nsa_sample(q3[:, :, :qw], q3[:, :, qw:], kc, vc, sel_pages, sel_table,
                               sel_f.reshape(b, t, row_w), win_full, n_prev=n_prev).astype(BF16)
            x = linear_residual(o.reshape(b * t, qw), p['w_o'][j], xm).reshape(b, t, d)
        x, rows = ffn_layer(x, x if is_prompt else no_halo, conv_prefix[l], p['ln_ffn'][l], p['w_in'], p['conv_w'][l],
                            p['conv_b'][l], p['w_out'], layer=l, nb=seqs_per_tile, tm=seq_tile)
        conv_rows.append(rows[:, FFN_HALO - (CONV_WIDTH - 1):])
    row_shape = (2, N_KV_HEADS, hd)
    n_win = min(WINDOW, n_prev + t)
    win_state = win_full[:, win_full.shape[1] - n_win:].reshape((b, n_win) + row_shape)
    return (x, cmp_rows, sel_f.reshape((b, t) + row_shape), win_state,
            jnp.stack(pool_rows), jnp.stack(conv_rows))


def kernel(x_prompt, x_sample, cache_kv_cmp, cache_kv_sel, page_table, state_kv_win, state_pool, state_conv, ln_mix, ln_ffn, w_pool, pool_scale, ln_kv, w_kv, k_norm, cmp_w1, cmp_pos, cmp_w2, w_qg, q_norm, w_o, w_in, conv_w, conv_b, w_out):
    depth = ln_mix.shape[0]
    n_a = w_pool.shape[0]
    d = x_prompt.shape[2]
    ff2 = w_in.shape[2]
    hd = LANES
    row_w = 2 * N_KV_HEADS * hd
    qg_pad = (-w_qg.shape[2]) % LANES
    half = CMP_BLOCK // 2
    params = dict(
        ln_mix=ln_mix, ln_ffn=ln_ffn, pool_scale=pool_scale, ln_kv=ln_kv, k_norm=k_norm, q_norm=q_norm,
        conv_w=conv_w, conv_b=conv_b, cmp_pos=cmp_pos,
        w_pool=[w_pool[l].astype(BF16) for l in range(n_a)], w_kv=w_kv.astype(BF16),
        w_qg=[jnp.pad(w_qg[j], ((0, 0), (0, qg_pad))).astype(BF16) for j in range(depth - n_a)],
        w_o=[w_o[j].astype(BF16) for j in range(depth - n_a)],
        w_in=w_in.astype(BF16), w_out=w_out.astype(BF16),
        cmp_w2=cmp_w2.astype(BF16),
        w1cat=jnp.concatenate([cmp_w1[:, :half], cmp_w1[:, half:]], axis=-1).astype(BF16).reshape(
            2, half // CMP_PAIR, CMP_PAIR * hd, 2 * cmp_w1.shape[3]),
        w1flat=cmp_w1.reshape(2, CMP_BLOCK * hd, cmp_w1.shape[3]).astype(BF16),
    )

    b_p, t_p, _ = x_prompt.shape
    prompt = _trunk(
        x_prompt, params, n_prev=0,
        pool_prefix=jnp.zeros((n_a, b_p, POOL_HALO, d), F32),
        conv_prefix=jnp.zeros((depth, b_p, FFN_HALO, ff2), F32),
        cmp_pages=None, cmp_table=None, sel_pages=None, sel_table=None, win_past=None,
        seq_tile=min(PROMPT_TILE, t_p), seqs_per_tile=1)

    b_s, t_s, _ = x_sample.shape
    n_prev = page_table.shape[1] * PAGE_SIZE
    assert t_s < CMP_STRIDE and cache_kv_cmp.shape[1] == PAGE_SIZE
    sample = _trunk(
        x_sample, params, n_prev=n_prev,
        pool_prefix=jnp.pad(state_pool, ((0, 0), (0, 0), (POOL_HALO - POOL_KEEP, 0), (0, 0))),
        conv_prefix=jnp.pad(state_conv, ((0, 0), (0, 0), (FFN_HALO - (CONV_WIDTH - 1), 0), (0, 0))),
        cmp_pages=cache_kv_cmp.reshape(cache_kv_cmp.shape[0], PAGE_SIZE * row_w // hd, hd), cmp_table=page_table,
        sel_pages=cache_kv_sel.reshape(cache_kv_sel.shape[0], PAGE_SIZE * row_w // hd, hd), sel_table=page_table,
        win_past=state_kv_win.reshape(b_s, state_kv_win.shape[1], row_w),
        seq_tile=t_s, seqs_per_tile=b_s)

    return tuple(leaf for pair in zip(prompt, sample) for leaf in pair)
```

```python
import functools

import numpy as np
import jax
import jax.numpy as jnp
from jax import lax
from jax.experimental import pallas as pl
from jax.experimental.pallas import tpu as pltpu

F32 = jnp.float32
BF16 = jnp.bfloat16

POOL_WINDOWS = (2, 4, 8, 16)
POOL_KEEP = max(POOL_WINDOWS) - 1
N_HEADS = 16
N_KV_HEADS = 4
HEADS_PER_GROUP = N_HEADS // N_KV_HEADS
N_BRANCH = 3
CMP_BLOCK = 32
CMP_STRIDE = 16
SEL_BLOCK = 64
N_SEL = 16
WINDOW = 512
FORCE_BONUS = 1e4
CONV_WIDTH = 3
Q_BLOCK = 128
PAGE_SIZE = 128
EPS = 1e-6
NEG_INF = -1e30

LANES = 128
SUBLANES = 8
BF16_ROWS = 16
MIB = 2 ** 20


def _cparams(n_axes, vmem_mib):
    return pltpu.CompilerParams(dimension_semantics=("arbitrary",) * n_axes,
                                vmem_limit_bytes=vmem_mib * MIB)


def _rms(x, g):
    return (x * lax.rsqrt(jnp.mean(x * x, axis=-1, keepdims=True) + EPS)) * g


def _alibi_slope(h):
    return float(np.float32(2.0 ** (-8.0 * (h + 1) / N_HEADS)))


def _norm_linear_kernel(x_ref, g_ref, w_ref, hg_ref, *out_refs, chunk_gain, norm_scale, sigmoid_from):
    of_refs, ob_ref = out_refs[:-1], out_refs[-1]
    per_out = of_refs[0].shape[1] // LANES
    h = _rms(x_ref[...], g_ref[...]).astype(BF16)
    y = jnp.dot(h, w_ref[...], preferred_element_type=F32)
    for c in range(y.shape[1] // LANES):
        cols = slice(c * LANES, (c + 1) * LANES)
        yc = y[:, cols]
        if chunk_gain[c] is not None:
            yc = _rms(yc, hg_ref[chunk_gain[c]:chunk_gain[c] + 1, :])
            if norm_scale is not None:
                yc = yc * norm_scale
        elif sigmoid_from is not None and c >= sigmoid_from:
            yc = jax.nn.sigmoid(yc)
        of_refs[c // per_out][:, (c % per_out) * LANES:(c % per_out + 1) * LANES] = yc
        ob_ref[:, cols] = yc.astype(BF16)


def norm_linear(x, gain, w_bf16, head_gains, chunk_gain, *, norm_scale=None, sigmoid_from=None, n_f32=1, tm=512):
    m, k = x.shape
    n = w_bf16.shape[1]
    tm = min(tm, m)
    assert m % tm == 0 and n % (n_f32 * LANES) == 0 and len(chunk_gain) == n // LANES
    kern = functools.partial(_norm_linear_kernel, chunk_gain=tuple(chunk_gain), norm_scale=norm_scale,
                             sigmoid_from=sigmoid_from)
    row_tile = lambda width: pl.BlockSpec((tm, width), lambda i: (i, 0))
    return pl.pallas_call(
        kern,
        out_shape=(jax.ShapeDtypeStruct((m, n // n_f32), F32),) * n_f32 + (jax.ShapeDtypeStruct((m, n), BF16),),
        grid=(m // tm,),
        in_specs=[row_tile(k),
                  pl.BlockSpec((1, k), lambda i: (0, 0)),
                  pl.BlockSpec((k, n), lambda i: (0, 0)),
                  pl.BlockSpec(head_gains.shape, lambda i: (0, 0))],
        out_specs=(row_tile(n // n_f32),) * n_f32 + (row_tile(n),),
        compiler_params=_cparams(1, 56),
        name="norm_linear",
    )(x, gain.reshape(1, k), w_bf16, head_gains)


def _linear_residual_kernel(a_ref, w_ref, r_ref, o_ref):
    o_ref[...] = r_ref[...] + jnp.dot(a_ref[...], w_ref[...], preferred_element_type=F32)


def linear_residual(a_bf16, w_bf16, res, *, tm=512):
    m, k = a_bf16.shape
    n = w_bf16.shape[1]
    tm = min(tm, m)
    assert m % tm == 0
    return pl.pallas_call(
        _linear_residual_kernel,
        out_shape=jax.ShapeDtypeStruct((m, n), F32),
        grid=(m // tm,),
        in_specs=[pl.BlockSpec((tm, k), lambda i: (i, 0)),
                  pl.BlockSpec((k, n), lambda i: (0, 0)),
                  pl.BlockSpec((tm, n), lambda i: (i, 0))],
        out_specs=pl.BlockSpec((tm, n), lambda i: (i, 0)),
        compiler_params=_cparams(1, 48),
        name="linear_residual",
    )(a_bf16, w_bf16, res)


POOL_HALO = 16


def _pool_kernel(x_ref, xh_ref, pre_ref, g_ref, w_ref, sc_ref, o_ref, st_ref, ctx_scr, *, tm, n_prev):
    i = pl.program_id(1)
    x = x_ref[...]
    h = _rms(x, g_ref[...])
    halo = _rms(xh_ref[...], g_ref[...])
    ctx_scr[0:POOL_HALO, :] = jnp.where(i == 0, pre_ref[...], halo)
    ctx_scr[POOL_HALO:POOL_HALO + tm, :] = h
    t = i * tm + lax.broadcasted_iota(jnp.int32, (tm, 1), 0)
    gw = x.shape[1] // len(POOL_WINDOWS)
    for gi, w in enumerate(POOL_WINDOWS):
        cols = slice(gi * gw, (gi + 1) * gw)
        hs = h[:, cols]
        acc = hs
        for k in range(1, w):
            acc = acc + ctx_scr[POOL_HALO - k:POOL_HALO - k + tm, cols]
        cnt = jnp.minimum(w, n_prev + t + 1).astype(F32)
        pooled = acc / cnt - hs
        mixed = jnp.dot(pooled.astype(BF16), w_ref[gi], preferred_element_type=F32)
        o_ref[:, cols] = x[:, cols] + mixed * sc_ref[:, cols]
    st_ref[...] = ctx_scr[tm + POOL_HALO - POOL_KEEP:tm + POOL_HALO, :]


def pool_layer(x, x_halo_src, prefix16, gain, w_pool_bf16, scale, *, n_prev, tm):
    b, t, d = x.shape
    assert t % tm == 0 and tm % SUBLANES == 0
    hb = tm // POOL_HALO
    kern = functools.partial(_pool_kernel, tm=tm, n_prev=n_prev)
    return pl.pallas_call(
        kern,
        out_shape=(jax.ShapeDtypeStruct((b, t, d), F32), jax.ShapeDtypeStruct((b, POOL_KEEP, d), F32)),
        grid=(b, t // tm),
        in_specs=[pl.BlockSpec((None, tm, d), lambda bi, i: (bi, i, 0)),
                  pl.BlockSpec((None, POOL_HALO, d), lambda bi, i: (bi, jnp.maximum(i * hb - 1, 0), 0)),
                  pl.BlockSpec((None, POOL_HALO, d), lambda bi, i: (bi, 0, 0)),
                  pl.BlockSpec((1, d), lambda bi, i: (0, 0)),
                  pl.BlockSpec(w_pool_bf16.shape, lambda bi, i: (0, 0, 0)),
                  pl.BlockSpec((1, d), lambda bi, i: (0, 0))],
        out_specs=(pl.BlockSpec((None, tm, d), lambda bi, i: (bi, i, 0)),
                   pl.BlockSpec((None, POOL_KEEP, d), lambda bi, i: (bi, 0, 0))),
        scratch_shapes=[pltpu.VMEM((tm + POOL_HALO, d), F32)],
        compiler_params=_cparams(2, 48),
        name="pool_layer",
    )(x, x_halo_src, prefix16, gain.reshape(1, d), w_pool_bf16, scale.reshape(1, d))


FFN_HALO = 16
FFN_SPLIT = 2


def _ffn_kernel(x_ref, xh_ref, pg_ref, pv_ref, g_ref, wig_ref, wiv_ref, cwg_ref, cwv_ref, cbg_ref, cbv_ref, wo_ref,
                o_ref, sg_ref, sv_ref, u_scr, acc_scr, *h_scrs, nb, tm, te):
    hg_scr, hv_scr = h_scrs[:FFN_SPLIT], h_scrs[FFN_SPLIT:]
    i = pl.program_id(1)
    f = pl.program_id(2)
    n_ext = nb * te
    n_out = n_ext - FFN_HALO

    @pl.when(f == 0)
    def _():
        for s in range(nb):
            u_scr[s * te:s * te + FFN_HALO, :] = _rms(xh_ref[s], g_ref[...]).astype(BF16)
            u_scr[s * te + FFN_HALO:s * te + FFN_HALO + tm, :] = _rms(x_ref[s], g_ref[...]).astype(BF16)
            if te > FFN_HALO + tm:
                u_scr[s * te + FFN_HALO + tm:(s + 1) * te, :] = jnp.zeros((te - FFN_HALO - tm, u_scr.shape[1]), BF16)
        acc_scr[...] = jnp.zeros_like(acc_scr)

    u = u_scr[...]
    n_split = FFN_SPLIT
    pw = hg_scr[0].shape[1]
    first = i == 0

    def up_project(h_scr, w_ref, p_ref, piece):
        cols = slice(piece * pw, (piece + 1) * pw)
        h = jnp.dot(u, w_ref[:, cols], preferred_element_type=F32)
        h_scr[...] = h
        for s in range(nb):
            halo = slice(s * te, s * te + FFN_HALO)
            h_scr[halo, :] = jnp.where(first, p_ref[s, :, cols], h[halo])

    def conv(h_scr, cw_ref, cb_ref, piece):
        cols = slice(piece * pw, (piece + 1) * pw)
        c = cb_ref[:, cols] + cw_ref[0:1, cols] * h_scr[FFN_HALO - 2:FFN_HALO - 2 + n_out, :]
        c = c + cw_ref[1:2, cols] * h_scr[FFN_HALO - 1:FFN_HALO - 1 + n_out, :]
        return c + cw_ref[2:3, cols] * h_scr[FFN_HALO:FFN_HALO + n_out, :]

    acts = []
    for piece in range(n_split):
        up_project(hg_scr[piece], wig_ref, pg_ref, piece)
        up_project(hv_scr[piece], wiv_ref, pv_ref, piece)
        cg = conv(hg_scr[piece], cwg_ref, cbg_ref, piece)
        cv = conv(hv_scr[piece], cwv_ref, cbv_ref, piece)
        acts.append(((cg * jax.nn.sigmoid(cg)) * cv).astype(BF16))
    acc_scr[...] += jnp.dot(jnp.concatenate(acts, axis=1), wo_ref[...], preferred_element_type=F32)

    for s in range(nb):
        for piece in range(n_split):
            cols = slice(piece * pw, (piece + 1) * pw)
            sg_ref[s, :, cols] = hg_scr[piece][s * te + tm:s * te + tm + FFN_HALO, :]
            sv_ref[s, :, cols] = hv_scr[piece][s * te + tm:s * te + tm + FFN_HALO, :]

    @pl.when(f == pl.num_programs(2) - 1)
    def _():
        for s in range(nb):
            o_ref[s] = x_ref[s] + acc_scr[s * te:s * te + tm, :]


def ffn_layer(x, x_halo_src, prefix16, gain, w_in_bf16, conv_w, conv_b, w_out_bf16, *, layer, nb, tm, tf=512):
    b, t, d = x.shape
    ff = w_out_bf16.shape[1]
    assert t % tm == 0 and b % nb == 0 and ff % tf == 0 and tm % SUBLANES == 0
    nf = ff // tf
    hb = tm // FFN_HALO
    te = FFN_HALO + -(-tm // BF16_ROWS) * BF16_ROWS
    n_ext = nb * te
    kern = functools.partial(_ffn_kernel, nb=nb, tm=tm, te=te)
    gate_col = lambda bi, i, f: (0, f)
    val_col = lambda bi, i, f: (0, nf + f)
    y, sg, sv = pl.pallas_call(
        kern,
        out_shape=(jax.ShapeDtypeStruct((b, t, d), F32),
                   jax.ShapeDtypeStruct((b, t // tm, FFN_HALO, ff), F32),
                   jax.ShapeDtypeStruct((b, t // tm, FFN_HALO, ff), F32)),
        grid=(b // nb, t // tm, nf),
        in_specs=[pl.BlockSpec((nb, tm, d), lambda bi, i, f: (bi, i, 0), pipeline_mode=pl.Buffered(1)),
                  pl.BlockSpec((nb, FFN_HALO, d), lambda bi, i, f: (bi, jnp.maximum(i * hb - 1, 0), 0)),
                  pl.BlockSpec((nb, FFN_HALO, tf), lambda bi, i, f: (bi, 0, f)),
                  pl.BlockSpec((nb, FFN_HALO, tf), lambda bi, i, f: (bi, 0, nf + f)),
                  pl.BlockSpec((1, d), lambda bi, i, f: (0, 0)),
                  pl.BlockSpec((None, d, tf), lambda bi, i, f: (layer, 0, f)),
                  pl.BlockSpec((None, d, tf), lambda bi, i, f: (layer, 0, nf + f)),
                  pl.BlockSpec((CONV_WIDTH, tf), gate_col),
                  pl.BlockSpec((CONV_WIDTH, tf), val_col),
                  pl.BlockSpec((1, tf), gate_col),
                  pl.BlockSpec((1, tf), val_col),
                  pl.BlockSpec((None, tf, d), lambda bi, i, f: (layer, f, 0))],
        out_specs=(pl.BlockSpec((nb, tm, d), lambda bi, i, f: (bi, i, 0), pipeline_mode=pl.Buffered(1)),
                   pl.BlockSpec((nb, None, FFN_HALO, tf), lambda bi, i, f: (bi, i, 0, f)),
                   pl.BlockSpec((nb, None, FFN_HALO, tf), lambda bi, i, f: (bi, i, 0, f))),
        scratch_shapes=[pltpu.VMEM((n_ext, d), BF16), pltpu.VMEM((n_ext - FFN_HALO, d), F32)]
        + [pltpu.VMEM((n_ext, tf // FFN_SPLIT), F32)] * (2 * FFN_SPLIT),
        compiler_params=_cparams(3, 56),
        name="ffn_layer",
    )(x, x_halo_src, prefix16, prefix16, gain.reshape(1, d), w_in_bf16, w_in_bf16, conv_w, conv_w,
      conv_b.reshape(1, 2 * ff), conv_b.reshape(1, 2 * ff), w_out_bf16)
    return y, jnp.concatenate([sg[:, -1], sv[:, -1]], axis=-1)


CMP_PAGES = 16
CHUNKS_PER_PAGE = PAGE_SIZE // CMP_STRIDE
CMP_PAIR = 2


def _compress_kernel(pt_ref, *refs, n_pages):
    del pt_ref
    page_refs = refs[:n_pages]
    w_ref, o_ref = refs[n_pages], refs[n_pages + 1]
    heads = 2 * N_KV_HEADS
    k_rows = lax.broadcasted_iota(jnp.int32, (heads, LANES), 0) < N_KV_HEADS
    def head_tiles(s):
        lhs = [[], []]
        for k in range(n_pages):
            for n in range(0, CHUNKS_PER_PAGE, 2):
                t_e = page_refs[k][(CMP_STRIDE * n + s) * heads:(CMP_STRIDE * n + s + 1) * heads, :]
                t_o = page_refs[k][(CMP_STRIDE * (n + 1) + s) * heads:(CMP_STRIDE * (n + 1) + s + 1) * heads, :]
                lhs[0].append(jnp.where(k_rows, t_e, pltpu.roll(t_o, N_KV_HEADS, 0)))
                lhs[1].append(jnp.where(k_rows, pltpu.roll(t_e, N_KV_HEADS, 0), t_o))
        return [jnp.concatenate(rows, axis=0).astype(BF16) for rows in lhs]

    acc = [None, None]
    for s2 in range(CMP_STRIDE // CMP_PAIR):
        tiles = [head_tiles(CMP_PAIR * s2 + i) for i in range(CMP_PAIR)]
        for v in range(2):
            a = jnp.concatenate([t[v] for t in tiles], axis=1)
            d = jnp.dot(a, w_ref[v, s2], preferred_element_type=F32)
            acc[v] = d if acc[v] is None else acc[v] + d
    for v in range(2):
        o_ref[v] = acc[v]


def compress_partial(pages, page_table, w1cat_bf16):
    s, npg = page_table.shape
    page_rows, hd = pages.shape[1:]
    assert page_rows == PAGE_SIZE * 2 * N_KV_HEADS
    hid2 = w1cat_bf16.shape[3]
    n_p = min(CMP_PAGES, npg)
    assert npg % n_p == 0
    n_ch = npg * CHUNKS_PER_PAGE

    def page_spec(k):
        return pl.BlockSpec((None, page_rows, hd), lambda si, j, pt: (pt[si * npg + j * n_p + k], 0, 0))

    grid_spec = pltpu.PrefetchScalarGridSpec(
        num_scalar_prefetch=1,
        grid=(s, npg // n_p),
        in_specs=[page_spec(k) for k in range(n_p)]
        + [pl.BlockSpec(w1cat_bf16.shape, lambda si, j, pt: (0, 0, 0, 0))],
        out_specs=pl.BlockSpec((None, 2, n_p * CHUNKS_PER_PAGE * N_KV_HEADS, hid2), lambda si, j, pt: (si, 0, j, 0)),
    )
    return pl.pallas_call(
        functools.partial(_compress_kernel, n_pages=n_p),
        out_shape=jax.ShapeDtypeStruct((s, 2, n_ch * N_KV_HEADS, hid2), F32),
        grid_spec=grid_spec,
        compiler_params=_cparams(2, 48),
        name="compress_partial",
    )(page_table.reshape(-1), *([pages] * n_p), w1cat_bf16)


def _compress_finish_kernel(pab_ref, pos_ref, w1_ref, w2_ref, kn_ref, kc_ref, vc_ref):
    n = pab_ref.shape[1]
    hid = w2_ref.shape[1]
    for v in range(2):
        blk = pab_ref[v]
        nxt = pltpu.roll(blk[:, hid:], n - N_KV_HEADS, 0)
        pos = jnp.broadcast_to(pos_ref[v], (SUBLANES, pos_ref.shape[2])).astype(BF16)
        bias = jnp.dot(pos, w1_ref[v], preferred_element_type=F32)[0:1, :]
        hdn = jax.nn.gelu(blk[:, :hid] + nxt + bias)
        out = jnp.dot(hdn.astype(BF16), w2_ref[v], preferred_element_type=F32)
        if v == 0:
            kc_ref[...] = _rms(out, kn_ref[...]).astype(BF16)
        else:
            vc_ref[...] = out.astype(BF16)


def compress_finish(pab, cmp_pos, w1flat_bf16, w2_bf16, k_gain):
    s, _, rows, hid2 = pab.shape
    hd = w2_bf16.shape[2]
    kdim = w1flat_bf16.shape[1]
    out = jax.ShapeDtypeStruct((s, rows, hd), BF16)
    kc, vc = pl.pallas_call(
        _compress_finish_kernel,
        out_shape=(out, out),
        grid=(s,),
        in_specs=[pl.BlockSpec((None, 2, rows, hid2), lambda si: (si, 0, 0, 0)),
                  pl.BlockSpec((2, 1, kdim), lambda si: (0, 0, 0)),
                  pl.BlockSpec(w1flat_bf16.shape, lambda si: (0, 0, 0)),
                  pl.BlockSpec(w2_bf16.shape, lambda si: (0, 0, 0)),
                  pl.BlockSpec((1, hd), lambda si: (0, 0))],
        out_specs=(pl.BlockSpec((None, rows, hd), lambda si: (si, 0, 0)),
                   pl.BlockSpec((None, rows, hd), lambda si: (si, 0, 0))),
        compiler_params=_cparams(1, 48),
        name="compress_finish",
    )(pab, cmp_pos.reshape(2, 1, kdim), w1flat_bf16, w2_bf16, k_gain.reshape(1, hd))
    return (kc.reshape(s, rows // N_KV_HEADS, N_KV_HEADS * hd), vc.reshape(s, rows // N_KV_HEADS, N_KV_HEADS * hd))


def _cmp_to_sel_map(nc, nb):
    c_s = np.arange(nc) * CMP_STRIDE
    c_e = c_s + CMP_BLOCK - 1
    s_s = np.arange(nb) * SEL_BLOCK
    s_e = s_s + SEL_BLOCK - 1
    return ((c_s[:, None] <= s_e[None]) & (c_e[:, None] >= s_s[None])).astype(np.float32)


def _masked_softmax(s, mask, axis):
    s = jnp.where(mask, s, NEG_INF)
    m = jnp.max(s, axis=axis, keepdims=True)
    p = jnp.where(mask, jnp.exp(s - m), 0.0)
    den = jnp.sum(p, axis=axis, keepdims=True)
    return p * (1.0 / jnp.where(den > 0, den, 1.0))


SEL_KEY_TILE = 512


POS_SPLIT = 3
BLOCK_LANES = 64
MASK_BIAS = -1e30


def _position_columns(pos, onehot):
    pos = np.asarray(pos)
    assert pos.max() // SEL_BLOCK <= 256
    out = np.zeros((pos.shape[0], LANES), np.float32)
    if onehot:
        assert pos.max() // SEL_BLOCK < BLOCK_LANES
        out[np.arange(pos.shape[0]), pos // SEL_BLOCK] = 1.0
    for j in range(POS_SPLIT):
        out[:, BLOCK_LANES + j] = (pos // SEL_BLOCK) * SEL_BLOCK
        out[:, BLOCK_LANES + POS_SPLIT + j] = pos % SEL_BLOCK
    return out


def _slope_columns():
    out = np.zeros((N_HEADS, LANES), np.float32)
    for h in range(N_HEADS):
        rem = np.float32(_alibi_slope(h))
        for j in range(POS_SPLIT):
            part = np.float32(rem.astype(jnp.bfloat16))
            out[h, BLOCK_LANES + j] = part
            out[h, BLOCK_LANES + POS_SPLIT + j] = part
            rem = np.float32(rem - part)
        assert rem == 0
    return out


def _nsa_prompt_kernel(q_ref, gt_ref, kc_ref, vc_ref, cpos_ref, selk_ref, selv_ref, wink_ref, winv_ref,
                       spos_ref, wpos_ref, mapt_ref, slope_ref, o_ref, used_ref, *, wlen):
    i = pl.program_id(1)
    s0 = i * Q_BLOCK
    hd = LANES
    rr, qb, tk = HEADS_PER_GROUP, Q_BLOCK, SEL_KEY_TILE
    m_rows = rr * qb
    n_cp = kc_ref.shape[0]
    nb = mapt_ref.shape[0]
    nt_dims = (((1,), (1,)), ((), ()))
    kt_last = s0 // tk
    ws = pl.multiple_of(jnp.maximum(s0 - WINDOW, 0), Q_BLOCK)

    row = lax.broadcasted_iota(jnp.int32, (m_rows, 1), 0)
    pos_q = s0 + (row & (qb - 1))
    posq_l = s0 + lax.broadcasted_iota(jnp.int32, (1, qb), 1)
    blk = lax.broadcasted_iota(jnp.int32, (nb, 1), 0)
    cur = posq_l // SEL_BLOCK
    forced = (blk == 0) | (blk == cur) | (blk == cur - 1)
    valid = blk * SEL_BLOCK <= posq_l
    cmask = lax.broadcasted_iota(jnp.int32, (1, n_cp), 1) * CMP_STRIDE + (CMP_BLOCK - 1) <= pos_q
    dist_w = pos_q - (ws + lax.broadcasted_iota(jnp.int32, (1, wlen), 1))
    wmask = (dist_w >= 0) & (dist_w < WINDOW)

    per_group, chosen_any = [], None
    for g in range(N_KV_HEADS):
        cols = slice(g * hd, (g + 1) * hd)
        q = jnp.concatenate([q_ref[:, (g * rr + r) * hd:(g * rr + r + 1) * hd] for r in range(rr)], axis=0)
        slope_cols = slope_ref[g]
        q_pos = jnp.concatenate([q, slope_cols.astype(BF16)], axis=1)

        kc_aug = jnp.concatenate([kc_ref[:, cols], cpos_ref[...]], axis=1)
        s = lax.dot_general(q_pos, kc_aug, nt_dims, preferred_element_type=F32)
        p = _masked_softmax(s, cmask, -1)
        o_c = jnp.dot(p.astype(BF16), vc_ref[:, cols], preferred_element_type=F32)

        p_sum = p[0:qb]
        for r in range(1, rr):
            p_sum = p_sum + p[r * qb:(r + 1) * qb]
        imp = lax.dot_general(mapt_ref[...], p_sum, nt_dims, precision=lax.Precision.HIGHEST,
                              preferred_element_type=F32)
        imp = jnp.where(valid, jnp.where(forced, imp + FORCE_BONUS, imp), NEG_INF)
        cnt = jnp.zeros((nb, qb), F32)
        for c in range(nb):
            rowc = imp[c:c + 1, :]
            beats = (rowc > imp) | ((rowc == imp) & (blk > c))
            cnt = cnt + jnp.where(beats, 1.0, 0.0)
        chosen = cnt < min(N_SEL, nb)
        bias_t = jnp.where(chosen, 0.0, MASK_BIAS)
        chosen_f = jnp.where(chosen, 1.0, 0.0)
        chosen_any = chosen_f if chosen_any is None else jnp.maximum(chosen_any, chosen_f)
        bias = jnp.transpose(jnp.concatenate([bias_t, jnp.zeros((hd - nb, qb), F32)], axis=0))
        q_sel = jnp.concatenate([q, (slope_cols + jnp.concatenate([bias] * rr, axis=0)).astype(BF16)], axis=1)
        per_group.append((q_pos, q_sel, o_c))

    n_used = jnp.int32(0)
    for t in range(nb // (tk // SEL_BLOCK)):
        any_sel = jnp.max(chosen_any[t * (tk // SEL_BLOCK):(t + 1) * (tk // SEL_BLOCK), :]) > 0
        used_ref[n_used] = t
        n_used = n_used + jnp.where(any_sel & (t < kt_last), 1, 0)

    def sel_tile(kt, carry, causal):
        k0 = pl.multiple_of(kt * tk, tk)
        pos_cols = spos_ref[pl.ds(k0, tk), :]
        if causal:
            visible = k0 + lax.broadcasted_iota(jnp.int32, (1, tk), 1) <= pos_q
        out = []
        for g in range(N_KV_HEADS):
            cols = slice(g * hd, (g + 1) * hd)
            m_i, l_i, acc = carry[g]
            k_aug = jnp.concatenate([selk_ref[pl.ds(k0, tk), cols], pos_cols], axis=1)
            s = lax.dot_general(per_group[g][1], k_aug, nt_dims, preferred_element_type=F32)
            if causal:
                s = jnp.where(visible, s, NEG_INF)
            m_new = jnp.maximum(m_i, jnp.max(s, axis=-1, keepdims=True))
            alpha = jnp.exp(m_i - m_new)
            p = jnp.exp(s - m_new)
            l_new = alpha * l_i + jnp.sum(p, axis=-1, keepdims=True)
            acc = alpha * acc + jnp.dot(p.astype(BF16), selv_ref[pl.ds(k0, tk), cols], preferred_element_type=F32)
            out.append((m_new, l_new, acc))
        return tuple(out)

    init = tuple((jnp.full((m_rows, 1), NEG_INF, F32), jnp.zeros((m_rows, 1), F32), jnp.zeros((m_rows, hd), F32))
                 for _ in range(N_KV_HEADS))
    carry = lax.fori_loop(0, n_used, lambda j, c: sel_tile(used_ref[j], c, False), init)
    final = sel_tile(kt_last, carry, True)

    for g in range(N_KV_HEADS):
        cols = slice(g * hd, (g + 1) * hd)
        q_pos, _, o_c = per_group[g]
        _, l_s, acc_s = final[g]
        o_s = acc_s * (1.0 / jnp.where(l_s > 0, l_s, 1.0))

        kw_aug = jnp.concatenate([wink_ref[pl.ds(ws, wlen), cols], wpos_ref[pl.ds(ws, wlen), :]], axis=1)
        s = jnp.where(wmask, lax.dot_general(q_pos, kw_aug, nt_dims, preferred_element_type=F32), NEG_INF)
        p = jnp.exp(s - jnp.max(s, axis=-1, keepdims=True))
        l_w = jnp.sum(p, axis=-1, keepdims=True)
        o_w = jnp.dot(p.astype(BF16), winv_ref[pl.ds(ws, wlen), cols], preferred_element_type=F32)
        o_w = o_w * (1.0 / jnp.where(l_w > 0, l_w, 1.0))

        def gate(branch):
            return jnp.concatenate(
                [gt_ref[:, (g * rr + r) * N_BRANCH + branch:(g * rr + r) * N_BRANCH + branch + 1] for r in range(rr)],
                axis=0)

        o = gate(0) * o_c + gate(1) * o_s + gate(2) * o_w
        for r in range(rr):
            o_ref[:, (g * rr + r) * hd:(g * rr + r + 1) * hd] = o[r * qb:(r + 1) * qb].astype(BF16)


def nsa_prompt(q_bf16, gates_f32, kc, vc, kv_bf16):
    b, t, _ = q_bf16.shape
    qw = N_HEADS * LANES
    n_cp = kc.shape[1]
    nb = t // SEL_BLOCK
    assert t % SEL_KEY_TILE == 0 and t % Q_BLOCK == 0
    wlen = min(WINDOW + Q_BLOCK, t)
    kvw = N_KV_HEADS * LANES
    mapt = np.zeros((nb, n_cp), np.float32)
    mapt[:, :n_cp - 1] = _cmp_to_sel_map(n_cp - 1, nb).T
    cpos = _position_columns(np.arange(n_cp) * CMP_STRIDE + (CMP_BLOCK - 1), onehot=False)
    spos = _position_columns(np.arange(t), onehot=True)
    wpos = _position_columns(np.arange(t), onehot=False)
    slope = np.repeat(_slope_columns().reshape(N_KV_HEADS, HEADS_PER_GROUP, 1, LANES), Q_BLOCK, axis=2)
    slope = slope.reshape(N_KV_HEADS, HEADS_PER_GROUP * Q_BLOCK, LANES)
    kern = functools.partial(_nsa_prompt_kernel, wlen=wlen)
    per_b = lambda bi, i: (bi, 0, 0)
    const2 = lambda bi, i: (0, 0)

    def kv_cols(branch, v):
        return pl.BlockSpec((None, t, kvw), lambda bi, i: (bi, 0, 2 * branch + v))

    return pl.pallas_call(
        kern,
        out_shape=jax.ShapeDtypeStruct((b, t, qw), BF16),
        grid=(b, t // Q_BLOCK),
        in_specs=[pl.BlockSpec((None, Q_BLOCK, qw), lambda bi, i: (bi, i, 0)),
                  pl.BlockSpec((None, Q_BLOCK, LANES), lambda bi, i: (bi, i, qw // LANES)),
                  pl.BlockSpec((None, n_cp, kvw), per_b),
                  pl.BlockSpec((None, n_cp, kvw), per_b),
                  pl.BlockSpec((n_cp, LANES), const2),
                  kv_cols(1, 0), kv_cols(1, 1), kv_cols(2, 0), kv_cols(2, 1),
                  pl.BlockSpec((t, LANES), const2),
                  pl.BlockSpec((t, LANES), const2),
                  pl.BlockSpec((nb, n_cp), const2),
                  pl.BlockSpec(slope.shape, lambda bi, i: (0, 0, 0))],
        out_specs=pl.BlockSpec((None, Q_BLOCK, qw), lambda bi, i: (bi, i, 0)),
        scratch_shapes=[pltpu.SMEM((t // SEL_KEY_TILE,), jnp.int32)],
        compiler_params=_cparams(2, 56),
        name="nsa_prompt",
    )(q_bf16, gates_f32, kc, vc, jnp.asarray(cpos, dtype=BF16), kv_bf16, kv_bf16, kv_bf16, kv_bf16,
      jnp.asarray(spos, dtype=BF16), jnp.asarray(wpos, dtype=BF16), jnp.asarray(mapt), jnp.asarray(slope))


SAMPLE_PAGES = 8
TAKEN = -3e38
TN_DIMS = (((0,), (0,)), ((), ()))


def _lane_to_rows(v):
    return jnp.transpose(jnp.broadcast_to(v, (LANES, LANES)))


def _scale_rows(a, v):
    t = _lane_to_rows(v)
    return a * jnp.concatenate([t] * (a.shape[1] // LANES), axis=1)


def _sample_scores(k_bf16, pos_k, qb, slope, pos_q):
    dist = pos_q - pos_k
    return jnp.dot(k_bf16, qb, preferred_element_type=F32) - slope * dist, dist


def _nsa_sample_select_kernel(qb_ref, kc_ref, vc_ref, mapt_ref, rsum_ref, slope_ref, lq_ref, sel_ref, oc_ref,
                              *, n_prev, nb):
    lq = lq_ref[...]
    n_c = kc_ref.shape[0]
    nbp = sel_ref.shape[0]
    pos_c = (lax.broadcasted_iota(jnp.int32, (n_c, 1), 0) * CMP_STRIDE + (CMP_BLOCK - 1)).astype(F32)
    s, dist = _sample_scores(kc_ref[...], pos_c, qb_ref[...], slope_ref[...], n_prev + lq)
    p = _masked_softmax(s, dist >= 0, 0)
    oc_ref[...] = lax.dot_general(p.astype(BF16), vc_ref[...], TN_DIMS, preferred_element_type=F32)
    p_r = jnp.dot(p, rsum_ref[...], precision=lax.Precision.HIGHEST, preferred_element_type=F32)
    imp = jnp.dot(mapt_ref[...], p_r, precision=lax.Precision.HIGHEST, preferred_element_type=F32)
    blk = lax.broadcasted_iota(jnp.int32, (nbp, 1), 0)
    posq_i = n_prev + lq.astype(jnp.int32)
    cur = posq_i // SEL_BLOCK
    forced = (blk == 0) | (blk == cur) | (blk == cur - 1)
    valid = blk * SEL_BLOCK <= posq_i
    imp = jnp.where(valid, jnp.where(forced, imp + FORCE_BONUS, imp), NEG_INF)
    blk_f = blk.astype(F32)

    def pick(_, carry):
        work, chosen = carry
        top = jnp.max(work, axis=0, keepdims=True)
        first = jnp.min(jnp.where(work == top, blk_f, float(nbp)), axis=0, keepdims=True)
        hit = blk_f == first
        return jnp.where(hit, TAKEN, work), jnp.where(hit, 1.0, chosen)

    _, chosen = lax.fori_loop(0, min(N_SEL, nb), pick, (imp, jnp.zeros((nbp, LANES), F32)))
    sel_ref[...] = chosen


def _nsa_sample_kernel(ph_ref, lg_ref, nu_ref, qb_ref, gt_ref, sel3_ref, selcur_ref, oc_ref, slope_ref, lq_ref,
                       selnew_ref, win_ref, *refs, n_pages, n_prev, q_len, win_start, npg):
    del ph_ref
    pages = refs[:n_pages]
    o_ref, m_scr, l_scr, acc_scr = refs[n_pages:]
    bi = pl.program_id(0)
    j = pl.program_id(1)
    hd = LANES
    kvw = N_KV_HEADS * hd
    n_used = nu_ref[bi]
    qb = qb_ref[...]
    slope = slope_ref[...]
    lq = lq_ref[...]
    pos_q = n_prev + lq

    def scores(k_bf16, pos_k):
        return _sample_scores(k_bf16, pos_k, qb, slope, pos_q)

    def online(s, mask, v_bf16):
        s = jnp.where(mask, s, NEG_INF)
        m_old = m_scr[...]
        m_new = jnp.maximum(m_old, jnp.max(s, axis=0, keepdims=True))
        alpha = jnp.exp(m_old - m_new)
        p = jnp.where(mask, jnp.exp(s - m_new), 0.0)
        l_scr[...] = alpha * l_scr[...] + jnp.sum(p, axis=0, keepdims=True)
        m_scr[...] = m_new
        pv = lax.dot_general(p.astype(BF16), v_bf16, TN_DIMS, preferred_element_type=F32)
        acc_scr[...] = _scale_rows(acc_scr[...], alpha) + pv

    @pl.when(j == 0)
    def _():
        m_scr[...] = jnp.full(m_scr.shape, NEG_INF, F32)
        l_scr[...] = jnp.zeros(l_scr.shape, F32)
        acc_scr[...] = jnp.zeros(acc_scr.shape, F32)

    @pl.when(j * n_pages < n_used)
    def _():
        heads = 2 * N_KV_HEADS

        def head_rows(h):
            return jnp.concatenate([pg[pl.ds(h, PAGE_SIZE, stride=heads), :] for pg in pages], axis=0)

        k_all = jnp.concatenate([head_rows(g) for g in range(N_KV_HEADS)], axis=1).astype(BF16)
        v_all = jnp.concatenate([head_rows(N_KV_HEADS + g) for g in range(N_KV_HEADS)], axis=1).astype(BF16)
        pos_parts, sel_parts = [], []
        for k in range(n_pages):
            slot = j * n_pages + k
            page = lg_ref[bi * npg + jnp.minimum(slot, n_used - 1)]
            live = jnp.where(slot < n_used, 1.0, 0.0)
            pos_parts.append((page * PAGE_SIZE + lax.broadcasted_iota(jnp.int32, (PAGE_SIZE, 1), 0)).astype(F32))
            blocks = sel3_ref[page] * live
            sel_parts.append(jnp.broadcast_to(blocks[:, None, :], (blocks.shape[0], SEL_BLOCK, LANES))
                             .reshape(PAGE_SIZE, LANES))
        s, dist = scores(k_all, jnp.concatenate(pos_parts, axis=0))
        online(s, (dist >= 0) & (jnp.concatenate(sel_parts, axis=0) > 0.5), v_all)

    @pl.when(j == pl.num_programs(1) - 1)
    def _():
        n_new = selnew_ref.shape[0]
        pos_n = (n_prev + lax.broadcasted_iota(jnp.int32, (n_new, 1), 0)).astype(F32)
        s, dist = scores(selnew_ref[:, 0:kvw].astype(BF16), pos_n)
        selrow = selcur_ref[...]
        online(s, (dist >= 0) & (selrow > 0.5), selnew_ref[:, kvw:2 * kvw].astype(BF16))
        l_s = l_scr[...]
        o_s = _scale_rows(acc_scr[...], 1.0 / jnp.where(l_s > 0, l_s, 1.0))

        n_w = win_ref.shape[0]
        pos_w = (win_start + lax.broadcasted_iota(jnp.int32, (n_w, 1), 0)).astype(F32)
        s, dist = scores(win_ref[:, 0:kvw].astype(BF16), pos_w)
        p = _masked_softmax(s, (dist >= 0) & (dist < WINDOW), 0)
        o_w = lax.dot_general(p.astype(BF16), win_ref[:, kvw:2 * kvw].astype(BF16), TN_DIMS,
                              preferred_element_type=F32)
        o_c = oc_ref[...]

        rr = HEADS_PER_GROUP
        for g in range(N_KV_HEADS):
            for r in range(rr):
                h = g * rr + r
                rows = slice((g * rr + r) * q_len, (g * rr + r + 1) * q_len)
                cols = slice(g * hd, (g + 1) * hd)
                gc = gt_ref[:, h * N_BRANCH + 0:h * N_BRANCH + 1]
                gs = gt_ref[:, h * N_BRANCH + 1:h * N_BRANCH + 2]
                gw = gt_ref[:, h * N_BRANCH + 2:h * N_BRANCH + 3]
                o_ref[:, h * hd:(h + 1) * hd] = gc * o_c[rows, cols] + gs * o_s[rows, cols] + gw * o_w[rows, cols]


def nsa_sample(q_f32, gates_f32, kc, vc, sel_pages, page_table, sel_new, win_full, *, n_prev):
    b, q_len, qw = q_f32.shape
    hd = LANES
    rr = HEADS_PER_GROUP
    g_n = N_KV_HEADS
    assert g_n * rr * q_len == LANES and n_prev % SEL_BLOCK == 0 and q_len <= SEL_BLOCK
    npg = page_table.shape[1]
    n_p = min(SAMPLE_PAGES, npg)
    assert npg % n_p == 0 and (n_p * PAGE_SIZE // SEL_BLOCK) % SUBLANES == 0
    n_ch = kc.shape[1]
    nb = -(-(n_prev + q_len) // SEL_BLOCK)
    nbp = -(-nb // SUBLANES) * SUBLANES
    width = sel_new.shape[2]
    page_rows = sel_pages.shape[1]
    assert page_rows == PAGE_SIZE * 2 * g_n and sel_pages.shape[2] == hd

    qt = q_f32.reshape(b, q_len, g_n, rr, hd).transpose(0, 2, 4, 3, 1).reshape(b, g_n, hd, rr * q_len)
    qblk = jnp.einsum('bgdl,gh->bgdhl', qt, jnp.eye(g_n, dtype=F32)).reshape(b, g_n * hd, LANES).astype(BF16)

    lane = np.arange(LANES)
    lane_g, lane_r, lane_q = lane // (rr * q_len), (lane // q_len) % rr, lane % q_len
    slope = np.array([_alibi_slope(h) for h in lane_g * rr + lane_r], np.float32).reshape(1, LANES)
    lq = lane_q.astype(np.float32).reshape(1, LANES)
    rsum = ((lane_g[:, None] == lane_g[None, :]) & (lane_q[:, None] == lane_q[None, :])).astype(np.float32)
    mapt = np.zeros((nbp, n_ch), np.float32)
    mapt[:nb, :n_ch - 1] = _cmp_to_sel_map(n_ch - 1, nb).T

    pad16 = lambda a: jnp.pad(a, ((0, 0), (0, (-a.shape[1]) % BF16_ROWS), (0, 0)))
    sel_new_p = pad16(sel_new)
    win_p = pad16(win_full)

    per_b1 = lambda bi: (bi, 0, 0)
    const1 = lambda bi: (0, 0)
    sel, o_c = pl.pallas_call(
        functools.partial(_nsa_sample_select_kernel, n_prev=n_prev, nb=nb),
        out_shape=(jax.ShapeDtypeStruct((b, nbp, LANES), F32), jax.ShapeDtypeStruct((b, LANES, g_n * hd), F32)),
        grid=(b,),
        in_specs=[pl.BlockSpec((None, g_n * hd, LANES), per_b1),
                  pl.BlockSpec((None, n_ch, g_n * hd), per_b1),
                  pl.BlockSpec((None, n_ch, g_n * hd), per_b1),
                  pl.BlockSpec((nbp, n_ch), const1),
                  pl.BlockSpec((LANES, LANES), const1),
                  pl.BlockSpec((1, LANES), const1),
                  pl.BlockSpec((1, LANES), const1)],
        out_specs=(pl.BlockSpec((None, nbp, LANES), per_b1), pl.BlockSpec((None, LANES, g_n * hd), per_b1)),
        compiler_params=_cparams(1, 48),
        name="nsa_sample_select",
    )(qblk, kc, vc, jnp.asarray(mapt), jnp.asarray(rsum), jnp.asarray(slope), jnp.asarray(lq))

    blocks_per_page = PAGE_SIZE // SEL_BLOCK
    sel_pages_mask = sel[:, :npg * blocks_per_page].reshape(b, npg, blocks_per_page, LANES)
    page_used = jnp.max(sel_pages_mask, axis=(2, 3)) > 0
    order = jnp.argsort(jnp.logical_not(page_used), axis=1, stable=True).astype(jnp.int32)
    n_used = jnp.maximum(jnp.sum(page_used, axis=1), 1).astype(jnp.int32)
    phys = jnp.take_along_axis(page_table, order, axis=1)
    cur_blk = n_prev // SEL_BLOCK
    sel_cur = sel[:, cur_blk:cur_blk + 1, :]

    def page_spec(k):
        return pl.BlockSpec(
            (None, page_rows, hd),
            lambda bi, j, ph, lg, nu: (ph[bi * npg + jnp.minimum(j * n_p + k, nu[bi] - 1)], 0, 0))

    const2 = lambda bi, j, ph, lg, nu: (0, 0)
    per_b = lambda bi, j, ph, lg, nu: (bi, 0, 0)
    grid_spec = pltpu.PrefetchScalarGridSpec(
        num_scalar_prefetch=3,
        grid=(b, npg // n_p),
        in_specs=[pl.BlockSpec((None, g_n * hd, LANES), per_b),
                  pl.BlockSpec((None, q_len, LANES), per_b),
                  pl.BlockSpec((None, npg, blocks_per_page, LANES), lambda bi, j, ph, lg, nu: (bi, 0, 0, 0)),
                  pl.BlockSpec((None, 1, LANES), per_b),
                  pl.BlockSpec((None, LANES, g_n * hd), per_b),
                  pl.BlockSpec((1, LANES), const2),
                  pl.BlockSpec((1, LANES), const2),
                  pl.BlockSpec((None, sel_new_p.shape[1], width), per_b),
                  pl.BlockSpec((None, win_p.shape[1], width), per_b)]
        + [page_spec(k) for k in range(n_p)],
        out_specs=pl.BlockSpec((None, q_len, qw), per_b),
        scratch_shapes=[pltpu.VMEM((1, LANES), F32), pltpu.VMEM((1, LANES), F32),
                        pltpu.VMEM((LANES, g_n * hd), F32)],
    )
    kern = functools.partial(_nsa_sample_kernel, n_pages=n_p, n_prev=n_prev, q_len=q_len,
                             win_start=n_prev + q_len - win_full.shape[1], npg=npg)
    return pl.pallas_call(
        kern,
        out_shape=jax.ShapeDtypeStruct((b, q_len, qw), F32),
        grid_spec=grid_spec,
        compiler_params=_cparams(2, 48),
        name="nsa_sample",
    )(phys.reshape(-1), order.reshape(-1), n_used, qblk, gates_f32, sel_pages_mask, sel_cur, o_c,
      jnp.asarray(slope), jnp.asarray(lq), sel_new_p, win_p, *([sel_pages] * n_p))


PROMPT_TILE = 512
FFN_PROMPT_TILE = 1024
KV_PROJ_TILE = 256


def _trunk(x, params, *, n_prev, pool_prefix, conv_prefix, cmp_pages, cmp_table, sel_pages, sel_table, win_past,
           seq_tile, seqs_per_tile):
    p = params
    b, t, d = x.shape
    depth = p['ln_mix'].shape[0]
    n_a = len(p['w_pool'])
    hd = LANES
    kvw = N_KV_HEADS * hd
    row_w = 2 * kvw
    is_prompt = cmp_pages is None
    no_halo = jnp.zeros((b, max(POOL_HALO, FFN_HALO), d), F32)
    pool_rows, conv_rows = [], []
    for l in range(depth):
        if l < n_a:
            x, rows = pool_layer(x, x if is_prompt else no_halo, pool_prefix[l], p['ln_mix'][l], p['w_pool'][l], p['pool_scale'][l],
                                 n_prev=n_prev, tm=seq_tile)
            pool_rows.append(rows)
        else:
            xm = x.reshape(b * t, d)
            if l == n_a:
                raw = [None] * N_KV_HEADS
                chunk_gain = 2 * raw + [0] * N_KV_HEADS + raw + [1] * N_KV_HEADS + raw
                cmp_f, sel_f, win_f, kv_b = norm_linear(xm, p['ln_kv'], p['w_kv'], p['k_norm'][1:N_BRANCH], chunk_gain,
                                                        n_f32=N_BRANCH, tm=KV_PROJ_TILE)
                cmp_rows = cmp_f.reshape((b, t, 2, N_KV_HEADS, hd))
                if is_prompt:
                    cmp_pages = cmp_rows.reshape(b * t // PAGE_SIZE, PAGE_SIZE * row_w // hd, hd)
                    cmp_table = jnp.arange(b * t // PAGE_SIZE, dtype=jnp.int32).reshape(b, t // PAGE_SIZE)
                    win_full = win_f.reshape(b, t, row_w)
                else:
                    win_full = jnp.concatenate([win_past, win_f.reshape(b, t, row_w)], axis=1)
                pab = compress_partial(cmp_pages, cmp_table, p['w1cat'])
                kc, vc = compress_finish(pab, p['cmp_pos'], p['w1flat'], p['cmp_w2'], p['k_norm'][0])
            j = l - n_a
            q_f, q_b = norm_linear(xm, p['ln_mix'][l], p['w_qg'][j], p['q_norm'][j:j + 1],
                                   [0] * N_HEADS + [None] * (p['w_qg'][j].shape[1] // hd - N_HEADS),
                                   norm_scale=hd ** -0.5, sigmoid_from=N_HEADS)
            qw = N_HEADS * hd
            if is_prompt:
                o = nsa_prompt(q_b.reshape(b, t, -1), q_f.reshape(b, t, -1), kc, vc, kv_b.reshape(b, t, -1))
            else:
                q3 = q_f.reshape(b, t, -1)
                o = nsa_sample(q3[:, :, :qw], q3[:, :, qw:], kc, vc, sel_pages, sel_table,
                               sel_f.reshape(b, t, row_w), win_full, n_prev=n_prev).astype(BF16)
            x = linear_residual(o.reshape(b * t, qw), p['w_o'][j], xm).reshape(b, t, d)
        x, rows = ffn_layer(x, x if is_prompt else no_halo, conv_prefix[l], p['ln_ffn'][l], p['w_in'], p['conv_w'][l],
                            p['conv_b'][l], p['w_out'], layer=l, nb=seqs_per_tile,
                            tm=min(FFN_PROMPT_TILE, t) if is_prompt else seq_tile)
        conv_rows.append(rows[:, FFN_HALO - (CONV_WIDTH - 1):])
    row_shape = (2, N_KV_HEADS, hd)
    n_win = min(WINDOW, n_prev + t)
    win_state = win_full[:, win_full.shape[1] - n_win:].reshape((b, n_win) + row_shape)
    return (x, cmp_rows, sel_f.reshape((b, t) + row_shape), win_state,
            jnp.stack(pool_rows), jnp.stack(conv_rows))


def kernel(x_prompt, x_sample, cache_kv_cmp, cache_kv_sel, page_table, state_kv_win, state_pool, state_conv, ln_mix, ln_ffn, w_pool, pool_scale, ln_kv, w_kv, k_norm, cmp_w1, cmp_pos, cmp_w2, w_qg, q_norm, w_o, w_in, conv_w, conv_b, w_out):
    depth = ln_mix.shape[0]
    n_a = w_pool.shape[0]
    d = x_prompt.shape[2]
    ff2 = w_in.shape[2]
    hd = LANES
    row_w = 2 * N_KV_HEADS * hd
    qg_pad = (-w_qg.shape[2]) % LANES
    half = CMP_BLOCK // 2
    params = dict(
        ln_mix=ln_mix, ln_ffn=ln_ffn, pool_scale=pool_scale, ln_kv=ln_kv, k_norm=k_norm, q_norm=q_norm,
        conv_w=conv_w, conv_b=conv_b, cmp_pos=cmp_pos,
        w_pool=[w_pool[l].astype(BF16) for l in range(n_a)], w_kv=w_kv.astype(BF16),
        w_qg=[jnp.pad(w_qg[j], ((0, 0), (0, qg_pad))).astype(BF16) for j in range(depth - n_a)],
        w_o=[w_o[j].astype(BF16) for j in range(depth - n_a)],
        w_in=w_in.astype(BF16), w_out=w_out.astype(BF16),
        cmp_w2=cmp_w2.astype(BF16),
        w1cat=jnp.concatenate([cmp_w1[:, :half], cmp_w1[:, half:]], axis=-1).astype(BF16).reshape(
            2, half // CMP_PAIR, CMP_PAIR * hd, 2 * cmp_w1.shape[3]),
        w1flat=cmp_w1.reshape(2, CMP_BLOCK * hd, cmp_w1.shape[3]).astype(BF16),
    )

    b_p, t_p, _ = x_prompt.shape
    prompt = _trunk(
        x_prompt, params, n_prev=0,
        pool_prefix=jnp.zeros((n_a, b_p, POOL_HALO, d), F32),
        conv_prefix=jnp.zeros((depth, b_p, FFN_HALO, ff2), F32),
        cmp_pages=None, cmp_table=None, sel_pages=None, sel_table=None, win_past=None,
        seq_tile=min(PROMPT_TILE, t_p), seqs_per_tile=1)

    b_s, t_s, _ = x_sample.shape
    n_prev = page_table.shape[1] * PAGE_SIZE
    assert t_s < CMP_STRIDE and cache_kv_cmp.shape[1] == PAGE_SIZE
    sample = _trunk(
        x_sample, params, n_prev=n_prev,
        pool_prefix=jnp.pad(state_pool, ((0, 0), (0, 0), (POOL_HALO - POOL_KEEP, 0), (0, 0))),
        conv_prefix=jnp.pad(state_conv, ((0, 0), (0, 0), (FFN_HALO - (CONV_WIDTH - 1), 0), (0, 0))),
        cmp_pages=cache_kv_cmp.reshape(cache_kv_cmp.shape[0], PAGE_SIZE * row_w // hd, hd), cmp_table=page_table,
        sel_pages=cache_kv_sel.reshape(cache_kv_sel.shape[0], PAGE_SIZE * row_w // hd, hd), sel_table=page_table,
        win_past=state_kv_win.reshape(b_s, state_kv_win.shape[1], row_w),
        seq_tile=t_s, seqs_per_tile=b_s)

    return tuple(leaf for pair in zip(prompt, sample) for leaf in pair)
```
